```python
import math
import jax, jax.numpy as jnp
from jax import lax
import numpy as np

D_MODEL = 1024
BATCH = 8
SEQ = 4096
DEPTH = 2

SSM_GROUPS = 32
SSM_GROUP_CH = 16
SSM_WIDTH = SSM_GROUPS * SSM_GROUP_CH
SSM_STATE = 64
DT_MIN = 0.001
DT_MAX = 0.1
N_HEADS = 8
HEAD_DIM = 64
ATTN_WIDTH = N_HEADS * HEAD_DIM
MOBA_BLOCK = 256
MOBA_TOPK = 3
Q_BLOCK = 128
ROPE_THETA = 10000.0
D_IN = SSM_WIDTH + 3 * ATTN_WIDTH + 2 * D_MODEL
N_GROUPS = 4
EXPERTS_PER_GROUP = 8
N_EXPERTS = N_GROUPS * EXPERTS_PER_GROUP
EXPERT_TOPK = 2
D_EXPERT = 512
EXPERT_ROWS = 256
NORM_EPS = 1e-6

kernel_name = "hybrid_s5_moba_hmoe_trunk"


def rms_norm(x, gain):
    xf = x.astype(jnp.float32)
    y = xf * lax.rsqrt(jnp.mean(xf * xf, axis=-1, keepdims=True) + NORM_EPS)
    return (y * gain.astype(jnp.float32)).astype(x.dtype)


def rope_tables(seq, dh):
    inv = ROPE_THETA ** (-jnp.arange(0, dh, 2, dtype=jnp.float32) / dh)
    ang = jnp.arange(seq, dtype=jnp.float32)[:, None] * inv[None, :]
    return jnp.cos(ang), jnp.sin(ang)


def apply_rope(x, cos, sin):
    half = x.shape[-1] // 2
    xf = x.astype(jnp.float32)
    x1, x2 = xf[..., :half], xf[..., half:]
    return jnp.concatenate([x1 * cos - x2 * sin, x2 * cos + x1 * sin], axis=-1).astype(x.dtype)


def _complex_linear_combine(left, right):
    ar1, ai1, br1, bi1 = left
    ar2, ai2, br2, bi2 = right
    ar = ar2 * ar1 - ai2 * ai1
    ai = ar2 * ai1 + ai2 * ar1
    br = ar2 * br1 - ai2 * bi1 + br2
    bi = ar2 * bi1 + ai2 * br1 + bi2
    return ar, ai, br, bi


def s5_branch(u, a_re, a_im, log_dt, b_re, b_im, c_re, c_im, d_skip, w_glu):
    bsz, seq, _ = u.shape
    ug = u.astype(jnp.float32).reshape(bsz, seq, SSM_GROUPS, SSM_GROUP_CH)
    dt = jnp.exp(log_dt.astype(jnp.float32))[:, None]
    lam_re = a_re.astype(jnp.float32)
    lam_im = a_im.astype(jnp.float32)
    decay = jnp.exp(lam_re * dt)
    abar_re = decay * jnp.cos(lam_im * dt)
    abar_im = decay * jnp.sin(lam_im * dt)
    den = lam_re * lam_re + lam_im * lam_im
    num_re = abar_re - 1.0
    f_re = (num_re * lam_re + abar_im * lam_im) / den
    f_im = (abar_im * lam_re - num_re * lam_im) / den
    bf_re = b_re.astype(jnp.float32)
    bf_im = b_im.astype(jnp.float32)
    bbar_re = f_re[..., None] * bf_re - f_im[..., None] * bf_im
    bbar_im = f_re[..., None] * bf_im + f_im[..., None] * bf_re
    bu_re = jnp.einsum('bsgh,gph->bsgp', ug, bbar_re)
    bu_im = jnp.einsum('bsgh,gph->bsgp', ug, bbar_im)
    a_r = jnp.broadcast_to(abar_re, bu_re.shape)
    a_i = jnp.broadcast_to(abar_im, bu_re.shape)
    _, _, s_re, s_im = lax.associative_scan(_complex_linear_combine, (a_r, a_i, bu_re, bu_im), axis=1)
    y = (jnp.einsum('bsgp,ghp->bsgh', s_re, c_re.astype(jnp.float32))
         - jnp.einsum('bsgp,ghp->bsgh', s_im, c_im.astype(jnp.float32))
         + d_skip.astype(jnp.float32).reshape(SSM_GROUPS, SSM_GROUP_CH) * ug)
    z = jax.nn.gelu(y.reshape(bsz, seq, SSM_WIDTH)).astype(u.dtype)
    val, gate = jnp.split(z @ w_glu, 2, axis=-1)
    return val * jax.nn.sigmoid(gate)


def moba_attention(q, k, v):
    bsz, nh, seq, dh = q.shape
    nb = -(-seq // MOBA_BLOCK)
    pad = nb * MOBA_BLOCK - seq
    nqb = seq // Q_BLOCK
    n_sel = min(MOBA_TOPK, nb - 1)
    scale = dh ** -0.5
    neg = jnp.finfo(jnp.float32).min
    kp = jnp.pad(k, ((0, 0), (0, 0), (0, pad), (0, 0)))
    vp = jnp.pad(v, ((0, 0), (0, 0), (0, pad), (0, 0)))
    kblk = kp.reshape(bsz, nh, nb, MOBA_BLOCK, dh)
    vblk = vp.reshape(bsz, nh, nb, MOBA_BLOCK, dh)

    def to_qblocks(t):
        return t.reshape(bsz, nh, nqb, Q_BLOCK, t.shape[-1]).transpose(0, 2, 1, 3, 4).reshape(
            bsz * nqb, nh, Q_BLOCK, t.shape[-1])

    qr = to_qblocks(q)
    q_blk = jnp.arange(seq) // MOBA_BLOCK
    if n_sel > 0:
        kmean = jnp.mean(kblk.astype(jnp.float32), axis=3)
        gate = jnp.einsum('bhsd,bhnd->bhsn', q.astype(jnp.float32), kmean)
        fully_past = jnp.arange(nb)[None, :] < q_blk[:, None]
        gate = jnp.where(fully_past, gate, neg)
        _, sel = lax.top_k(gate, n_sel)
        valid = sel < q_blk[:, None]
        sel_r = to_qblocks(sel)
        valid_r = to_qblocks(valid)
    else:
        sel_r = jnp.zeros((bsz * nqb, nh, Q_BLOCK, 0), jnp.int32)
        valid_r = jnp.zeros((bsz * nqb, nh, Q_BLOCK, 0), jnp.bool_)
    b_idx = jnp.repeat(jnp.arange(bsz), nqb)
    j_idx = jnp.tile(jnp.arange(nqb), bsz)
    hidx = jnp.arange(nh)[:, None, None]

    def step(args):
        qb, sel_b, valid_b, b, j = args
        q_start = j * Q_BLOCK
        cur = q_start // MOBA_BLOCK
        kb = kblk[b]
        vb = vblk[b]
        k_cur = kb[:, cur]
        v_cur = vb[:, cur]
        qpos = q_start + jnp.arange(Q_BLOCK)
        kpos = cur * MOBA_BLOCK + jnp.arange(MOBA_BLOCK)
        s_cur = jnp.einsum('hqd,hkd->hqk', qb, k_cur, preferred_element_type=jnp.float32) * scale
        s_cur = jnp.where(kpos[None, None, :] <= qpos[None, :, None], s_cur, neg)
        if n_sel == 0:
            p = jax.nn.softmax(s_cur, axis=-1).astype(vb.dtype)
            out = jnp.einsum('hqk,hkd->hqd', p, v_cur)
        else:
            k_sel = kb[hidx, sel_b]
            v_sel = vb[hidx, sel_b]
            s_past = jnp.einsum('hqd,hqnkd->hqnk', qb, k_sel, preferred_element_type=jnp.float32) * scale
            s_past = jnp.where(valid_b[..., None], s_past, neg)
            s = jnp.concatenate([s_past.reshape(nh, Q_BLOCK, n_sel * MOBA_BLOCK), s_cur], axis=-1)
            p = jax.nn.softmax(s, axis=-1).astype(vb.dtype)
            p_past = p[..., :n_sel * MOBA_BLOCK].reshape(nh, Q_BLOCK, n_sel, MOBA_BLOCK)
            p_cur = p[..., n_sel * MOBA_BLOCK:]
            out = (jnp.einsum('hqnk,hqnkd->hqd', p_past, v_sel)
                   + jnp.einsum('hqk,hkd->hqd', p_cur, v_cur))
        return out.astype(q.dtype)

    outs = lax.map(step, (qr, sel_r, valid_r, b_idx, j_idx))
    outs = outs.reshape(bsz, nqb, nh, Q_BLOCK, dh).transpose(0, 1, 3, 2, 4)
    return outs.reshape(bsz, seq, nh * dh)


def hier_moe(h, w_group, b_group, w_expert, b_expert, w_gate_up, w_down):
    bsz, seq, d = h.shape
    n = bsz * seq
    xt = h.reshape(n, d)
    glog = (xt @ w_group).astype(jnp.float32) + b_group.astype(jnp.float32)
    gprob = jax.nn.softmax(glog, axis=-1)
    g_top = jnp.argmax(glog, axis=-1)
    p_g = jnp.take_along_axis(gprob, g_top[:, None], axis=-1)[:, 0]
    elog = ((xt @ w_expert).astype(jnp.float32) + b_expert.astype(jnp.float32)).reshape(
        n, N_GROUPS, EXPERTS_PER_GROUP)
    elog = jnp.take_along_axis(elog, g_top[:, None, None], axis=1)[:, 0]
    eprob = jax.nn.softmax(elog, axis=-1)
    top_p, top_e = lax.top_k(eprob, EXPERT_TOPK)
    top_p = top_p / jnp.sum(top_p, axis=-1, keepdims=True)
    weights = p_g[:, None] * top_p
    expert_id = g_top[:, None] * EXPERTS_PER_GROUP + top_e
    n_assign = n * EXPERT_TOPK
    flat_e = expert_id.reshape(-1).astype(jnp.int32)
    flat_tok = jnp.repeat(jnp.arange(n, dtype=jnp.int32), EXPERT_TOPK)
    flat_w = weights.reshape(-1)
    order = jnp.argsort(flat_e)
    se, stok, sw = flat_e[order], flat_tok[order], flat_w[order]
    counts = jnp.bincount(flat_e, length=N_EXPERTS)
    padded = (counts + EXPERT_ROWS - 1) // EXPERT_ROWS * EXPERT_ROWS
    start = jnp.cumsum(counts) - counts
    pend = jnp.cumsum(padded)
    pstart = pend - padded
    dest = pstart[se] + (jnp.arange(n_assign) - start[se])
    cap = n_assign + N_EXPERTS * EXPERT_ROWS
    nblk = cap // EXPERT_ROWS
    slot_tok = jnp.zeros((cap,), jnp.int32).at[dest].set(stok)
    slot_w = jnp.zeros((cap,), jnp.float32).at[dest].set(sw)
    blk_e = jnp.minimum(jnp.searchsorted(pend, jnp.arange(nblk) * EXPERT_ROWS, side='right'),
                        N_EXPERTS - 1)

    def expert_block(args):
        tok, e = args
        xb = xt[tok]
        gt, up = jnp.split(xb @ w_gate_up[e], 2, axis=-1)
        return (jax.nn.silu(gt) * up) @ w_down[e]

    yb = lax.map(expert_block, (slot_tok.reshape(nblk, EXPERT_ROWS), blk_e))
    y = yb.reshape(cap, d) * slot_w[:, None].astype(h.dtype)
    out = jax.ops.segment_sum(y, slot_tok, num_segments=n)
    return out.reshape(bsz, seq, d)


def setup_inputs(seed: int = 0) -> dict:
    key = jax.random.key(seed)
    ks = jax.random.split(key, 24)
    L = DEPTH
    f32 = jnp.float32

    def nrm(k, shape, scale):
        return jax.random.normal(k, shape, f32) * scale

    x = jax.random.normal(ks[0], (BATCH, SEQ, D_MODEL), f32)
    norm1_g = 1.0 + nrm(ks[1], (L, D_MODEL), 0.01)
    w_in = nrm(ks[2], (L, D_MODEL, D_IN), D_MODEL ** -0.5)
    ssm_a_re = -0.5 + nrm(ks[3], (L, SSM_GROUPS, SSM_STATE), 0.01)
    ssm_a_im = math.pi * jnp.arange(SSM_STATE, dtype=f32)[None, None, :] + nrm(
        ks[4], (L, SSM_GROUPS, SSM_STATE), 0.01)
    ssm_log_dt = jax.random.uniform(ks[5], (L, SSM_GROUPS), f32, math.log(DT_MIN), math.log(DT_MAX))
    ssm_b_re = nrm(ks[6], (L, SSM_GROUPS, SSM_STATE, SSM_GROUP_CH), (2 * SSM_GROUP_CH) ** -0.5)
    ssm_b_im = nrm(ks[7], (L, SSM_GROUPS, SSM_STATE, SSM_GROUP_CH), (2 * SSM_GROUP_CH) ** -0.5)
    ssm_c_re = nrm(ks[8], (L, SSM_GROUPS, SSM_GROUP_CH, SSM_STATE), (2 * SSM_STATE) ** -0.5)
    ssm_c_im = nrm(ks[9], (L, SSM_GROUPS, SSM_GROUP_CH, SSM_STATE), (2 * SSM_STATE) ** -0.5)
    ssm_d = nrm(ks[10], (L, SSM_WIDTH), 1.0)
    w_glu = nrm(ks[11], (L, SSM_WIDTH, 2 * D_MODEL), SSM_WIDTH ** -0.5)
    q_norm_g = 1.0 + nrm(ks[12], (L, HEAD_DIM), 0.01)
    k_norm_g = 1.0 + nrm(ks[13], (L, HEAD_DIM), 0.01)
    w_attn = nrm(ks[14], (L, ATTN_WIDTH, D_MODEL), ATTN_WIDTH ** -0.5)
    w_out = nrm(ks[15], (L, D_MODEL, D_MODEL), D_MODEL ** -0.5)
    norm2_g = 1.0 + nrm(ks[16], (L, D_MODEL), 0.01)
    router_w_group = nrm(ks[17], (L, D_MODEL, N_GROUPS), D_MODEL ** -0.5)
    router_b_group = nrm(ks[18], (L, N_GROUPS), 0.01)
    router_w_expert = nrm(ks[19], (L, D_MODEL, N_EXPERTS), D_MODEL ** -0.5)
    router_b_expert = nrm(ks[20], (L, N_EXPERTS), 0.01)
    w_gate_up = nrm(ks[21], (L, N_EXPERTS, D_MODEL, 2 * D_EXPERT), D_MODEL ** -0.5)
    w_down = nrm(ks[22], (L, N_EXPERTS, D_EXPERT, D_MODEL), D_EXPERT ** -0.5)
    return {"x": x, "norm1_g": norm1_g, "w_in": w_in, "ssm_a_re": ssm_a_re, "ssm_a_im": ssm_a_im,
            "ssm_log_dt": ssm_log_dt, "ssm_b_re": ssm_b_re, "ssm_b_im": ssm_b_im, "ssm_c_re": ssm_c_re,
            "ssm_c_im": ssm_c_im, "ssm_d": ssm_d, "w_glu": w_glu, "q_norm_g": q_norm_g,
            "k_norm_g": k_norm_g, "w_attn": w_attn, "w_out": w_out, "norm2_g": norm2_g,
            "router_w_group": router_w_group, "router_b_group": router_b_group,
            "router_w_expert": router_w_expert, "router_b_expert": router_b_expert,
            "w_gate_up": w_gate_up, "w_down": w_down}


def reference(x, norm1_g, w_in, ssm_a_re, ssm_a_im, ssm_log_dt, ssm_b_re, ssm_b_im, ssm_c_re, ssm_c_im,
              ssm_d, w_glu, q_norm_g, k_norm_g, w_attn, w_out, norm2_g, router_w_group, router_b_group,
              router_w_expert, router_b_expert, w_gate_up, w_down):
    bsz, seq, _ = x.shape
    cos, sin = rope_tables(seq, HEAD_DIM)
    splits = [SSM_WIDTH, SSM_WIDTH + ATTN_WIDTH, SSM_WIDTH + 2 * ATTN_WIDTH,
              SSM_WIDTH + 3 * ATTN_WIDTH, SSM_WIDTH + 3 * ATTN_WIDTH + D_MODEL]
    for l in range(DEPTH):
        h = rms_norm(x, norm1_g[l])
        proj = h @ w_in[l]
        u, q, k, v, g_ssm, g_attn = jnp.split(proj, splits, axis=-1)
        y_ssm = s5_branch(u, ssm_a_re[l], ssm_a_im[l], ssm_log_dt[l], ssm_b_re[l], ssm_b_im[l],
                          ssm_c_re[l], ssm_c_im[l], ssm_d[l], w_glu[l])
        q = q.reshape(bsz, seq, N_HEADS, HEAD_DIM).transpose(0, 2, 1, 3)
        k = k.reshape(bsz, seq, N_HEADS, HEAD_DIM).transpose(0, 2, 1, 3)
        v = v.reshape(bsz, seq, N_HEADS, HEAD_DIM).transpose(0, 2, 1, 3)
        q = apply_rope(rms_norm(q, q_norm_g[l]), cos, sin)
        k = apply_rope(rms_norm(k, k_norm_g[l]), cos, sin)
        y_attn = moba_attention(q, k, v) @ w_attn[l]
        mixed = jax.nn.sigmoid(g_ssm) * y_ssm + jax.nn.sigmoid(g_attn) * y_attn
        x = x + mixed @ w_out[l]
        h2 = rms_norm(x, norm2_g[l])
        x = x + hier_moe(h2, router_w_group[l], router_b_group[l], router_w_expert[l],
                         router_b_expert[l], w_gate_up[l], w_down[l])
    return x
```

```python
import functools
import math

import jax
import jax.numpy as jnp
from jax import lax
from jax.experimental import pallas as pl
from jax.experimental.pallas import tpu as pltpu

F32 = jnp.float32
BF16 = jnp.bfloat16

D_MODEL = 1024
SSM_GROUPS = 32
SSM_GROUP_CH = 16
SSM_WIDTH = SSM_GROUPS * SSM_GROUP_CH
SSM_STATE = 64
N_HEADS = 8
HEAD_DIM = 64
ATTN_WIDTH = N_HEADS * HEAD_DIM
MOBA_BLOCK = 256
MOBA_TOPK = 3
ROPE_THETA = 10000.0
D_IN = SSM_WIDTH + 3 * ATTN_WIDTH + 2 * D_MODEL
N_GROUPS = 4
EXPERTS_PER_GROUP = 8
N_EXPERTS = N_GROUPS * EXPERTS_PER_GROUP
EXPERT_TOPK = 2
D_EXPERT = 512
EXPERT_ROWS = 256
NORM_EPS = 1e-6

LANES = 128
SUBLANES = 8
VMEM_LIMIT = 56 * 1024 * 1024
MASK_NEG = -1e30

TOKEN_TILE = 512
SSM_STEPS = 64
SSM_HALF = SSM_WIDTH // 2
SSM_HALF_STATE = SSM_GROUPS * SSM_STATE // 2
ROUTER_LANES = 128
COMBINE_TILE = 256


def _params(*sem):
    return pltpu.CompilerParams(dimension_semantics=sem, vmem_limit_bytes=VMEM_LIMIT)


def _dot(a, b):
    return jnp.dot(a, b, preferred_element_type=F32)


def _dot_nt(a, b):
    return lax.dot_general(a, b, (((1,), (1,)), ((), ())), preferred_element_type=F32)


def _rms(x, gain):
    return x * lax.rsqrt(jnp.mean(x * x, axis=-1, keepdims=True) + NORM_EPS) * gain


def _in_proj_kernel(x_ref, g1_ref, w_ref, gs_ref, qg_ref, kg_ref, cos_ref, sin_ref,
                    u_ref, q_ref, k_ref, v_ref, g_ref):
    hb = _rms(x_ref[...], g1_ref[...]).astype(BF16)

    def seg(lo, hi):
        return _dot(hb, w_ref[:, lo:hi])

    o_q = SSM_WIDTH
    o_k = o_q + ATTN_WIDTH
    o_v = o_k + ATTN_WIDTH
    o_g = o_v + ATTN_WIDTH
    u_ref[...] = seg(0, o_q).astype(BF16)
    v_ref[...] = seg(o_v, o_g).astype(BF16)
    g_ref[...] = seg(o_g, D_IN).astype(BF16)

    reps = ATTN_WIDTH // LANES
    cos = jnp.concatenate([cos_ref[...]] * reps, axis=1)
    sin = jnp.concatenate([sin_ref[...]] * reps, axis=1)
    lane = lax.broadcasted_iota(jnp.int32, cos.shape, 1)
    first_half = (lane % HEAD_DIM) < (HEAD_DIM // 2)

    def norm_rope(t, gain):
        ss = _dot((t * t).astype(BF16), gs_ref[...])
        tn = t * lax.rsqrt(ss * (1.0 / HEAD_DIM) + NORM_EPS) * gain
        partner = jnp.where(first_half,
                            pltpu.roll(tn, ATTN_WIDTH - HEAD_DIM // 2, 1),
                            pltpu.roll(tn, HEAD_DIM // 2, 1))
        return tn * cos + partner * sin

    q_ref[...] = norm_rope(seg(o_q, o_k), qg_ref[...]).astype(BF16)
    k_ref[...] = norm_rope(seg(o_k, o_v), kg_ref[...]).astype(BF16)


def _in_proj(x, g1, w_in, gsum, qg, kg, cos, sin):
    n = x.shape[0]
    tm = TOKEN_TILE
    row = lambda i: (i, 0)
    fix = lambda i: (0, 0)
    outs = [jax.ShapeDtypeStruct((n, w), BF16)
            for w in (SSM_WIDTH, ATTN_WIDTH, ATTN_WIDTH, ATTN_WIDTH, 2 * D_MODEL)]
    return pl.pallas_call(
        _in_proj_kernel,
        grid=(n // tm,),
        in_specs=[pl.BlockSpec((tm, D_MODEL), row),
                  pl.BlockSpec((1, D_MODEL), fix),
                  pl.BlockSpec((D_MODEL, D_IN), fix),
                  pl.BlockSpec((ATTN_WIDTH, ATTN_WIDTH), fix),
                  pl.BlockSpec((1, ATTN_WIDTH), fix),
                  pl.BlockSpec((1, ATTN_WIDTH), fix),
                  pl.BlockSpec((tm, LANES), row),
                  pl.BlockSpec((tm, LANES), row)],
        out_specs=[pl.BlockSpec((tm, o.shape[1]), row) for o in outs],
        out_shape=outs,
        compiler_params=_params("parallel"),
        name="in_proj",
    )(x, g1, w_in, gsum, qg, kg, cos, sin)


def _ssm_kernel(u_ref, bre_ref, bim_ref, cre_ref, cim_ref, are_ref, aim_ref, d_ref,
                z_ref, h_ref, sre_ref, sim_ref, *, batch):
    @pl.when(pl.program_id(0) == 0)
    def _():
        h_ref[...] = jnp.zeros_like(h_ref)

    steps = u_ref.shape[0] // batch
    ys = []
    for c in range(2):
        uc = u_ref[:, c * SSM_HALF:(c + 1) * SSM_HALF]
        sre_ref[...] = _dot(uc, bre_ref[c])
        sim_ref[...] = _dot(uc, bim_ref[c])
        a_re = jnp.broadcast_to(are_ref[c], (batch, SSM_HALF_STATE))
        a_im = jnp.broadcast_to(aim_ref[c], (batch, SSM_HALF_STATE))

        def step(t, carry):
            h_re, h_im = carry
            r0 = pl.multiple_of(t * batch, batch)
            n_re = a_re * h_re - a_im * h_im + sre_ref[pl.ds(r0, batch), :]
            n_im = a_re * h_im + a_im * h_re + sim_ref[pl.ds(r0, batch), :]
            sre_ref[pl.ds(r0, batch), :] = n_re
            sim_ref[pl.ds(r0, batch), :] = n_im
            return n_re, n_im

        h_re, h_im = lax.fori_loop(0, steps, step, (h_ref[c, 0], h_ref[c, 1]), unroll=4)
        h_ref[c, 0] = h_re
        h_ref[c, 1] = h_im
        ys.append(_dot(sre_ref[...].astype(BF16), cre_ref[c])
                  + _dot(sim_ref[...].astype(BF16), cim_ref[c]))
    y = jnp.concatenate(ys, axis=1) + d_ref[...] * u_ref[...].astype(F32)
    z_ref[...] = jax.nn.gelu(y).astype(BF16)


def _ssm(u, bre, bim, cre, cim, are, aim, d, batch):
    n = u.shape[0]
    rows = SSM_STEPS * batch
    fix3 = lambda i: (0, 0, 0)
    return pl.pallas_call(
        functools.partial(_ssm_kernel, batch=batch),
        grid=(n // rows,),
        in_specs=[pl.BlockSpec((rows, SSM_WIDTH), lambda i: (i, 0)),
                  pl.BlockSpec(bre.shape, fix3), pl.BlockSpec(bim.shape, fix3),
                  pl.BlockSpec(cre.shape, fix3), pl.BlockSpec(cim.shape, fix3),
                  pl.BlockSpec(are.shape, fix3), pl.BlockSpec(aim.shape, fix3),
                  pl.BlockSpec((1, SSM_WIDTH), lambda i: (0, 0))],
        out_specs=pl.BlockSpec((rows, SSM_WIDTH), lambda i: (i, 0)),
        out_shape=jax.ShapeDtypeStruct((n, SSM_WIDTH), BF16),
        scratch_shapes=[pltpu.VMEM((2, 2, batch, SSM_HALF_STATE), F32),
                        pltpu.VMEM((rows, SSM_HALF_STATE), F32),
                        pltpu.VMEM((rows, SSM_HALF_STATE), F32)],
        compiler_params=_params("arbitrary"),
        name="s5_scan",
    )(u, bre, bim, cre, cim, are, aim, d)


def _moba_kernel(q_ref, k_ref, v_ref, o_ref, kpad_ref, km_ref):
    i = pl.program_id(1)
    tq = q_ref.shape[0]
    blk = MOBA_BLOCK
    nblk = k_ref.shape[0] // blk
    lane = lax.broadcasted_iota(jnp.int32, (tq, LANES), 1)
    is_head_lane = lane < HEAD_DIM

    @pl.when(i == 0)
    def _build():
        km_ref[...] = jnp.zeros_like(km_ref)

        def per_block(j, _):
            r0 = pl.multiple_of(j * blk, blk)
            for h in range(N_HEADS):
                p = h // 2
                kk = k_ref[pl.ds(r0, blk), p * LANES:(p + 1) * LANES].astype(F32)
                if h % 2:
                    kk = pltpu.roll(kk, HEAD_DIM, 1)
                kk = jnp.where(is_head_lane, kk, 0.0)
                km_ref[h, pl.ds(HEAD_DIM + j, 1), :] = jnp.mean(kk, axis=0, keepdims=True)
                kk = jnp.where(lane == HEAD_DIM + j, 1.0, kk)
                kpad_ref[pl.ds(r0, blk), h * LANES:(h + 1) * LANES] = kk.astype(BF16)
            return 0

        lax.fori_loop(0, nblk, per_block, 0)

    past = (lane >= HEAD_DIM) & (lane < HEAD_DIM + i)
    row_pos = lax.broadcasted_iota(jnp.int32, (tq, blk), 0)
    col_pos = lax.broadcasted_iota(jnp.int32, (tq, blk), 1)
    causal = col_pos <= row_pos
    cur0 = pl.multiple_of(i * blk, blk)
    n_sel = min(MOBA_TOPK, nblk - 1)

    outs = []
    for h in range(N_HEADS):
        p = h // 2
        qa = q_ref[:, p * LANES:(p + 1) * LANES].astype(F32)
        if h % 2:
            qa = pltpu.roll(qa, HEAD_DIM, 1)
        qa = jnp.where(is_head_lane, qa, 0.0)
        qb = qa.astype(BF16)
        km = km_ref[h]
        km_hi = km.astype(BF16)
        km_lo = (km - km_hi.astype(F32)).astype(BF16)
        gate = jnp.where(past, _dot_nt(qb, km_hi) + _dot_nt(qb, km_lo), -jnp.inf)
        sel = jnp.zeros((tq, LANES), jnp.bool_)
        for _ in range(n_sel):
            best = jnp.max(gate, axis=1, keepdims=True)
            cand = (gate == best) & (best > -jnp.inf)
            first = jnp.min(jnp.where(cand, lane, LANES), axis=1, keepdims=True)
            pick = lane == first
            sel = sel | pick
            gate = jnp.where(pick, -jnp.inf, gate)
        q_aug = (qa + jnp.where(past & ~sel, MASK_NEG, 0.0)).astype(BF16)

        def kv_step(r0, carry, mask):
            m, l, acc = carry
            s = _dot_nt(q_aug, kpad_ref[pl.ds(r0, blk), h * LANES:(h + 1) * LANES])
            if mask is not None:
                s = jnp.where(mask, s, MASK_NEG)
            m_new = jnp.maximum(m, jnp.max(s, axis=1, keepdims=True))
            alpha = jnp.exp(m - m_new)
            pexp = jnp.exp(s - m_new)
            l = alpha * l + jnp.sum(pexp, axis=1, keepdims=True)
            acc = alpha * acc + _dot(pexp.astype(BF16),
                                     v_ref[pl.ds(r0, blk), p * LANES:(p + 1) * LANES])
            return m_new, l, acc

        init = (jnp.full((tq, 1), -jnp.inf, F32), jnp.zeros((tq, 1), F32),
                jnp.zeros((tq, LANES), F32))
        carry = lax.fori_loop(
            0, i, lambda j, c: kv_step(pl.multiple_of(j * blk, blk), c, None), init)
        _, l, acc = kv_step(cur0, carry, causal)
        outs.append(acc / l)

    for p in range(N_HEADS // 2):
        pair = jnp.where(is_head_lane, outs[2 * p], outs[2 * p + 1])
        o_ref[:, p * LANES:(p + 1) * LANES] = pair.astype(BF16)


def _moba(q, k, v, batch):
    n = q.shape[0]
    seq = n // batch
    tq = MOBA_BLOCK
    view = lambda t: t.reshape(seq, batch * ATTN_WIDTH)
    out = pl.pallas_call(
        _moba_kernel,
        grid=(batch, seq // tq),
        in_specs=[pl.BlockSpec((tq, ATTN_WIDTH), lambda b, i: (i, b)),
                  pl.BlockSpec((seq, ATTN_WIDTH), lambda b, i: (0, b)),
                  pl.BlockSpec((seq, ATTN_WIDTH), lambda b, i: (0, b))],
        out_specs=pl.BlockSpec((tq, ATTN_WIDTH), lambda b, i: (i, b)),
        out_shape=jax.ShapeDtypeStruct((seq, batch * ATTN_WIDTH), BF16),
        scratch_shapes=[pltpu.VMEM((seq, N_HEADS * LANES), BF16),
                        pltpu.VMEM((N_HEADS, LANES, LANES), F32)],
        compiler_params=_params("parallel", "arbitrary"),
        name="moba_attention",
    )(view(q), view(k), view(v))
    return out.reshape(n, ATTN_WIDTH)


def _mix_kernel(z_ref, a_ref, g_ref, x_ref, wglu_ref, wattn_ref, wout_ref, g2_ref,
                wrh_ref, wrl_ref, rb_ref, xo_ref, lg_ref):
    glu = _dot(z_ref[...], wglu_ref[...])
    y_ssm = glu[:, :D_MODEL] * jax.nn.sigmoid(glu[:, D_MODEL:])
    y_attn = _dot(a_ref[...], wattn_ref[...])
    g = g_ref[...].astype(F32)
    mixed = jax.nn.sigmoid(g[:, :D_MODEL]) * y_ssm + jax.nn.sigmoid(g[:, D_MODEL:]) * y_attn
    x = x_ref[...] + _dot(mixed.astype(BF16), wout_ref[...])
    xo_ref[...] = x
    h2 = _rms(x, g2_ref[...])
    hi = h2.astype(BF16)
    lo = (h2 - hi.astype(F32)).astype(BF16)
    lg_ref[...] = (_dot(hi, wrh_ref[...]) + _dot(lo, wrh_ref[...]) + _dot(hi, wrl_ref[...])
                   + rb_ref[...])


def _mix(z, a, g, x, wglu, wattn, wout, g2, wrh, wrl, rb):
    n = x.shape[0]
    tm = TOKEN_TILE
    row = lambda i: (i, 0)
    fix = lambda i: (0, 0)
    full = lambda t: pl.BlockSpec(t.shape, fix)
    return pl.pallas_call(
        _mix_kernel,
        grid=(n // tm,),
        in_specs=[pl.BlockSpec((tm, SSM_WIDTH), row), pl.BlockSpec((tm, ATTN_WIDTH), row),
                  pl.BlockSpec((tm, 2 * D_MODEL), row), pl.BlockSpec((tm, D_MODEL), row),
                  full(wglu), full(wattn), full(wout), full(g2), full(wrh), full(wrl), full(rb)],
        out_specs=[pl.BlockSpec((tm, D_MODEL), row), pl.BlockSpec((tm, ROUTER_LANES), row)],
        out_shape=[jax.ShapeDtypeStruct((n, D_MODEL), F32),
                   jax.ShapeDtypeStruct((n, ROUTER_LANES), F32)],
        compiler_params=_params("parallel"),
        name="mix_out_router",
    )(z, a, g, x, wglu, wattn, wout, g2, wrh, wrl, rb)


def _row_copy(src_hbm, row, buf, slot, r, sem):
    return pltpu.make_async_copy(src_hbm.at[pl.ds(row, 1), :],
                                 buf.at[slot, pl.ds(r, 1), :], sem.at[slot])


def _expert_kernel(blk_e_ref, nused_ref, tok_ref, x_hbm, g2_ref, wgu_ref, wd_ref, y_ref,
                   xbuf, sem):
    b = pl.program_id(0)
    nused = nused_ref[0]
    slot = b % 2
    rows = EXPERT_ROWS

    def issue(blk, slot):
        def body(r, _):
            _row_copy(x_hbm, tok_ref[blk * rows + r], xbuf, slot, r, sem).start()
            return 0
        lax.fori_loop(0, rows, body, 0, unroll=8)

    @pl.when(b == 0)
    def _():
        issue(0, 0)

    @pl.when(b + 1 < nused)
    def _():
        issue(b + 1, 1 - slot)

    @pl.when(b < nused)
    def _():
        def wait(r, _):
            _row_copy(x_hbm, 0, xbuf, slot, r, sem).wait()
            return 0
        lax.fori_loop(0, rows, wait, 0, unroll=8)
        h = _rms(xbuf[slot], g2_ref[...]).astype(BF16)
        gu = _dot(h, wgu_ref[0])
        act = jax.nn.silu(gu[:, :D_EXPERT]) * gu[:, D_EXPERT:]
        y_ref[...] = _dot(act.astype(BF16), wd_ref[0])

    @pl.when(b >= nused)
    def _():
        y_ref[...] = jnp.zeros_like(y_ref)


def _experts(blk_e, nused, slot_tok, x, g2, wgu, wd):
    nblk = blk_e.shape[0]
    rows = EXPERT_ROWS
    grid_spec = pltpu.PrefetchScalarGridSpec(
        num_scalar_prefetch=3,
        grid=(nblk,),
        in_specs=[pl.BlockSpec(memory_space=pl.ANY),
                  pl.BlockSpec((1, D_MODEL), lambda b, e, n, t: (0, 0)),
                  pl.BlockSpec((1, D_MODEL, 2 * D_EXPERT), lambda b, e, n, t: (e[b], 0, 0)),
                  pl.BlockSpec((1, D_EXPERT, D_MODEL), lambda b, e, n, t: (e[b], 0, 0))],
        out_specs=pl.BlockSpec((rows, D_MODEL), lambda b, e, n, t: (b, 0)),
        scratch_shapes=[pltpu.VMEM((2, rows, D_MODEL), F32),
                        pltpu.SemaphoreType.DMA((2,))],
    )
    return pl.pallas_call(
        _expert_kernel,
        grid_spec=grid_spec,
        out_shape=jax.ShapeDtypeStruct((nblk * rows, D_MODEL), F32),
        compiler_params=_params("arbitrary"),
        name="experts",
    )(blk_e, nused, slot_tok, x, g2, wgu, wd)


def _combine_kernel(pos_ref, x_ref, w_ref, y_hbm, o_ref, ybuf, sem):
    s = pl.program_id(0)
    nsteps = pl.num_programs(0)
    slot = s % 2
    tt = COMBINE_TILE
    rows = EXPERT_TOPK * tt

    def issue(step, slot):
        def body(r, _):
            _row_copy(y_hbm, pos_ref[step * rows + r], ybuf, slot, r, sem).start()
            return 0
        lax.fori_loop(0, rows, body, 0, unroll=8)

    @pl.when(s == 0)
    def _():
        issue(0, 0)

    @pl.when(s + 1 < nsteps)
    def _():
        issue(s + 1, 1 - slot)

    def wait(r, _):
        _row_copy(y_hbm, 0, ybuf, slot, r, sem).wait()
        return 0
    lax.fori_loop(0, rows, wait, 0, unroll=8)
    w = w_ref[...]
    o_ref[...] = (x_ref[...] + w[:, 0:1] * ybuf[slot, 0:tt, :]
                  + w[:, 1:2] * ybuf[slot, tt:rows, :])


def _combine(pos, x, w, y):
    n = x.shape[0]
    tt = COMBINE_TILE
    grid_spec = pltpu.PrefetchScalarGridSpec(
        num_scalar_prefetch=1,
        grid=(n // tt,),
        in_specs=[pl.BlockSpec((tt, D_MODEL), lambda s, p: (s, 0)),
                  pl.BlockSpec((tt, EXPERT_TOPK), lambda s, p: (s, 0)),
                  pl.BlockSpec(memory_space=pl.ANY)],
        out_specs=pl.BlockSpec((tt, D_MODEL), lambda s, p: (s, 0)),
        scratch_shapes=[pltpu.VMEM((2, EXPERT_TOPK * tt, D_MODEL), F32),
                        pltpu.SemaphoreType.DMA((2,))],
    )
    return pl.pallas_call(
        _combine_kernel,
        grid_spec=grid_spec,
        out_shape=jax.ShapeDtypeStruct((n, D_MODEL), F32),
        compiler_params=_params("arbitrary"),
        name="moe_combine",
    )(pos, x, w, y)


def _rope_tables(seq, batch):
    inv = ROPE_THETA ** (-jnp.arange(0, HEAD_DIM, 2, dtype=F32) / HEAD_DIM)
    ang = jnp.arange(seq, dtype=F32)[:, None] * inv[None, :]
    cos, sin = jnp.cos(ang), jnp.sin(ang)
    cos = jnp.concatenate([cos, cos] * (LANES // HEAD_DIM), axis=1)
    sin = jnp.concatenate([-sin, sin] * (LANES // HEAD_DIM), axis=1)
    rep = lambda t: jnp.repeat(t, batch, axis=0)
    return rep(cos), rep(sin)


def _ssm_weights(a_re, a_im, log_dt, b_re, b_im, c_re, c_im):
    dt = jnp.exp(log_dt)[:, None]
    decay = jnp.exp(a_re * dt)
    abar_re = decay * jnp.cos(a_im * dt)
    abar_im = decay * jnp.sin(a_im * dt)
    den = a_re * a_re + a_im * a_im
    num_re = abar_re - 1.0
    f_re = (num_re * a_re + abar_im * a_im) / den
    f_im = (abar_im * a_re - num_re * a_im) / den
    bbar_re = f_re[..., None] * b_re - f_im[..., None] * b_im
    bbar_im = f_re[..., None] * b_im + f_im[..., None] * b_re
    gh = SSM_GROUPS // 2
    eye = jnp.eye(gh, dtype=F32)

    def b_mat(t):
        t = t.reshape(2, gh, SSM_STATE, SSM_GROUP_CH)
        return jnp.einsum('cgph,gk->cghkp', t, eye).reshape(2, SSM_HALF, SSM_HALF_STATE).astype(BF16)

    def c_mat(t):
        t = t.reshape(2, gh, SSM_GROUP_CH, SSM_STATE)
        return jnp.einsum('cghp,gk->cgpkh', t, eye).reshape(2, SSM_HALF_STATE, SSM_HALF).astype(BF16)

    a_vec = lambda t: t.reshape(2, 1, SSM_HALF_STATE)
    return (b_mat(bbar_re), b_mat(bbar_im), c_mat(c_re), c_mat(-c_im), a_vec(abar_re), a_vec(abar_im))


def _route(logits, n):
    glog = logits[:, :N_GROUPS]
    gprob = jax.nn.softmax(glog, axis=-1)
    g_top = jnp.argmax(glog, axis=-1)
    p_g = jnp.take_along_axis(gprob, g_top[:, None], axis=-1)[:, 0]
    elog = logits[:, N_GROUPS:N_GROUPS + N_EXPERTS].reshape(n, N_GROUPS, EXPERTS_PER_GROUP)
    elog = jnp.take_along_axis(elog, g_top[:, None, None], axis=1)[:, 0]
    eprob = jax.nn.softmax(elog, axis=-1)
    top_p, top_e = lax.top_k(eprob, EXPERT_TOPK)
    top_p = top_p / jnp.sum(top_p, axis=-1, keepdims=True)
    weights = p_g[:, None] * top_p
    expert_id = (g_top[:, None] * EXPERTS_PER_GROUP + top_e).astype(jnp.int32)

    n_assign = n * EXPERT_TOPK
    flat_e = expert_id.reshape(-1)
    onehot = (flat_e[:, None] == jnp.arange(N_EXPERTS, dtype=jnp.int32)[None, :]).astype(jnp.int32)
    csum = jnp.cumsum(onehot, axis=0)
    rank = jnp.sum(csum * onehot, axis=1) - 1
    counts = csum[-1]
    padded = (counts + EXPERT_ROWS - 1) // EXPERT_ROWS * EXPERT_ROWS
    pend = jnp.cumsum(padded)
    pstart = pend - padded
    dest = (pstart[flat_e] + rank).astype(jnp.int32)
    cap = n_assign + N_EXPERTS * EXPERT_ROWS
    nblk = cap // EXPERT_ROWS
    flat_tok = jnp.repeat(jnp.arange(n, dtype=jnp.int32), EXPERT_TOPK)
    slot_tok = jnp.zeros((cap,), jnp.int32).at[dest].set(flat_tok)
    blk_e = jnp.minimum(jnp.searchsorted(pend, jnp.arange(nblk) * EXPERT_ROWS, side='right'),
                        N_EXPERTS - 1).astype(jnp.int32)
    nused = (pend[-1:] // EXPERT_ROWS).astype(jnp.int32)
    pos = dest.reshape(n // COMBINE_TILE, COMBINE_TILE, EXPERT_TOPK).transpose(0, 2, 1).reshape(-1)
    return weights, slot_tok, blk_e, nused, pos


def kernel(x, norm1_g, w_in, ssm_a_re, ssm_a_im, ssm_log_dt, ssm_b_re, ssm_b_im, ssm_c_re, ssm_c_im,
           ssm_d, w_glu, q_norm_g, k_norm_g, w_attn, w_out, norm2_g, router_w_group, router_b_group,
           router_w_expert, router_b_expert, w_gate_up, w_down):
    batch, seq, _ = x.shape
    depth = w_in.shape[0]
    n = batch * seq
    assert batch == SUBLANES and seq % MOBA_BLOCK == 0 and n % TOKEN_TILE == 0
    assert seq // MOBA_BLOCK <= LANES - HEAD_DIM and seq % SSM_STEPS == 0

    xt = x.transpose(1, 0, 2).reshape(n, D_MODEL)
    cos, sin = _rope_tables(seq, batch)
    idx = jnp.arange(ATTN_WIDTH)
    gsum = (idx[:, None] // HEAD_DIM == idx[None, :] // HEAD_DIM).astype(BF16)
    row = lambda t: t.reshape(1, -1).astype(F32)

    for l in range(depth):
        qg = row(jnp.tile(q_norm_g[l], N_HEADS)) * (HEAD_DIM ** -0.5)
        kg = row(jnp.tile(k_norm_g[l], N_HEADS))
        u, q, k, v, g = _in_proj(xt, row(norm1_g[l]), w_in[l].astype(BF16), gsum, qg, kg, cos, sin)
        ssm_w = _ssm_weights(ssm_a_re[l], ssm_a_im[l], ssm_log_dt[l], ssm_b_re[l], ssm_b_im[l],
                             ssm_c_re[l], ssm_c_im[l])
        z = _ssm(u, *ssm_w, row(ssm_d[l]), batch)
        attn = _moba(q, k, v, batch)

        w_r = jnp.concatenate([router_w_group[l], router_w_expert[l]], axis=1)
        w_r = jnp.pad(w_r, ((0, 0), (0, ROUTER_LANES - w_r.shape[1])))
        w_rh = w_r.astype(BF16)
        w_rl = (w_r - w_rh.astype(F32)).astype(BF16)
        b_r = jnp.concatenate([router_b_group[l], router_b_expert[l]])
        b_r = row(jnp.pad(b_r, (0, ROUTER_LANES - b_r.shape[0])))
        g2 = row(norm2_g[l])
        xt, logits = _mix(z, attn, g, xt, w_glu[l].astype(BF16), w_attn[l].astype(BF16),
                          w_out[l].astype(BF16), g2, w_rh, w_rl, b_r)

        weights, slot_tok, blk_e, nused, pos = _route(logits, n)
        y = _experts(blk_e, nused, slot_tok, xt, g2, w_gate_up[l].astype(BF16),
                     w_down[l].astype(BF16))
        xt = _combine(pos, xt, weights, y)

    return xt.reshape(seq, batch, D_MODEL).transpose(1, 0, 2)
```

```python
import functools
import math

import jax
import jax.numpy as jnp
from jax import lax
from jax.experimental import pallas as pl
from jax.experimental.pallas import tpu as pltpu

F32 = jnp.float32
BF16 = jnp.bfloat16

D_MODEL = 1024
SSM_GROUPS = 32
SSM_GROUP_CH = 16
SSM_WIDTH = SSM_GROUPS * SSM_GROUP_CH
SSM_STATE = 64
N_HEADS = 8
HEAD_DIM = 64
ATTN_WIDTH = N_HEADS * HEAD_DIM
MOBA_BLOCK = 256
MOBA_TOPK = 3
ROPE_THETA = 10000.0
D_IN = SSM_WIDTH + 3 * ATTN_WIDTH + 2 * D_MODEL
N_GROUPS = 4
EXPERTS_PER_GROUP = 8
N_EXPERTS = N_GROUPS * EXPERTS_PER_GROUP
EXPERT_TOPK = 2
D_EXPERT = 512
EXPERT_ROWS = 256
NORM_EPS = 1e-6

LANES = 128
SUBLANES = 8
VMEM_LIMIT = 56 * 1024 * 1024
MASK_NEG = -1e30

TOKEN_TILE = 512
SSM_STEPS = 64
SSM_HALF = SSM_WIDTH // 2
SSM_HALF_STATE = SSM_GROUPS * SSM_STATE // 2
GATE_ROWS = 16
V_ROWS = HEAD_DIM + 16
ROUTER_LANES = 128
COMBINE_TILE = 256


def _params(*sem, flags=None):
    return pltpu.CompilerParams(dimension_semantics=sem, vmem_limit_bytes=VMEM_LIMIT, flags=flags)


def _dot(a, b):
    return jnp.dot(a, b, preferred_element_type=F32)


def _dot_nt(a, b):
    return lax.dot_general(a, b, (((1,), (1,)), ((), ())), preferred_element_type=F32)


def _rms(x, gain):
    return x * lax.rsqrt(jnp.mean(x * x, axis=-1, keepdims=True) + NORM_EPS) * gain


def _in_proj_kernel(x_ref, g1_ref, w_ref, gs_ref, qg_ref, kg_ref, cos_ref, sin_ref,
                    u_ref, q_ref, k_ref, v_ref, g_ref):
    hb = _rms(x_ref[...], g1_ref[...]).astype(BF16)

    def seg(lo, hi):
        return _dot(hb, w_ref[:, lo:hi])

    o_q = SSM_WIDTH
    o_k = o_q + ATTN_WIDTH
    o_v = o_k + ATTN_WIDTH
    o_g = o_v + ATTN_WIDTH
    u_ref[...] = seg(0, o_q).astype(BF16)
    v_ref[...] = seg(o_v, o_g).astype(BF16)
    g_ref[...] = seg(o_g, D_IN).astype(BF16)

    reps = ATTN_WIDTH // LANES
    cos = jnp.concatenate([cos_ref[...]] * reps, axis=1)
    sin = jnp.concatenate([sin_ref[...]] * reps, axis=1)
    lane = lax.broadcasted_iota(jnp.int32, cos.shape, 1)
    first_half = (lane % HEAD_DIM) < (HEAD_DIM // 2)

    def norm_rope(t, gain):
        ss = _dot((t * t).astype(BF16), gs_ref[...])
        tn = t * lax.rsqrt(ss * (1.0 / HEAD_DIM) + NORM_EPS) * gain
        partner = jnp.where(first_half,
                            pltpu.roll(tn, ATTN_WIDTH - HEAD_DIM // 2, 1),
                            pltpu.roll(tn, HEAD_DIM // 2, 1))
        return tn * cos + partner * sin

    q_ref[...] = norm_rope(seg(o_q, o_k), qg_ref[...]).astype(BF16)
    k_ref[...] = norm_rope(seg(o_k, o_v), kg_ref[...]).astype(BF16)


def _in_proj(x, g1, w_in, gsum, qg, kg, cos, sin):
    n = x.shape[0]
    tm = TOKEN_TILE
    row = lambda i: (i, 0)
    fix = lambda i: (0, 0)
    outs = [jax.ShapeDtypeStruct((n, w), BF16)
            for w in (SSM_WIDTH, ATTN_WIDTH, ATTN_WIDTH, ATTN_WIDTH, 2 * D_MODEL)]
    return pl.pallas_call(
        _in_proj_kernel,
        grid=(n // tm,),
        in_specs=[pl.BlockSpec((tm, D_MODEL), row),
                  pl.BlockSpec((1, D_MODEL), fix),
                  pl.BlockSpec((D_MODEL, D_IN), fix),
                  pl.BlockSpec((ATTN_WIDTH, ATTN_WIDTH), fix),
                  pl.BlockSpec((1, ATTN_WIDTH), fix),
                  pl.BlockSpec((1, ATTN_WIDTH), fix),
                  pl.BlockSpec((tm, LANES), row),
                  pl.BlockSpec((tm, LANES), row)],
        out_specs=[pl.BlockSpec((tm, o.shape[1]), row) for o in outs],
        out_shape=outs,
        compiler_params=_params("parallel"),
        name="in_proj",
    )(x, g1, w_in, gsum, qg, kg, cos, sin)


def _ssm_kernel(u_ref, bre_ref, bim_ref, cre_ref, cim_ref, are_ref, aim_ref, d_ref,
                z_ref, h_ref, sre_ref, sim_ref, *, batch):
    @pl.when(pl.program_id(0) == 0)
    def _():
        h_ref[...] = jnp.zeros_like(h_ref)

    steps = u_ref.shape[0] // batch
    ys = []
    for c in range(2):
        uc = u_ref[:, c * SSM_HALF:(c + 1) * SSM_HALF]
        sre_ref[...] = _dot(uc, bre_ref[c])
        sim_ref[...] = _dot(uc, bim_ref[c])
        a_re = jnp.broadcast_to(are_ref[c], (batch, SSM_HALF_STATE))
        a_im = jnp.broadcast_to(aim_ref[c], (batch, SSM_HALF_STATE))

        def step(t, carry):
            h_re, h_im = carry
            r0 = pl.multiple_of(t * batch, batch)
            n_re = a_re * h_re - a_im * h_im + sre_ref[pl.ds(r0, batch), :]
            n_im = a_re * h_im + a_im * h_re + sim_ref[pl.ds(r0, batch), :]
            sre_ref[pl.ds(r0, batch), :] = n_re
            sim_ref[pl.ds(r0, batch), :] = n_im
            return n_re, n_im

        h_re, h_im = lax.fori_loop(0, steps, step, (h_ref[c, 0], h_ref[c, 1]), unroll=4)
        h_ref[c, 0] = h_re
        h_ref[c, 1] = h_im
        ys.append(_dot(sre_ref[...].astype(BF16), cre_ref[c])
                  + _dot(sim_ref[...].astype(BF16), cim_ref[c]))
    y = jnp.concatenate(ys, axis=1) + d_ref[...] * u_ref[...].astype(F32)
    z_ref[...] = jax.nn.gelu(y).astype(BF16)


def _ssm(u, bre, bim, cre, cim, are, aim, d, batch):
    n = u.shape[0]
    rows = SSM_STEPS * batch
    fix3 = lambda i: (0, 0, 0)
    return pl.pallas_call(
        functools.partial(_ssm_kernel, batch=batch),
        grid=(n // rows,),
        in_specs=[pl.BlockSpec((rows, SSM_WIDTH), lambda i: (i, 0)),
                  pl.BlockSpec(bre.shape, fix3), pl.BlockSpec(bim.shape, fix3),
                  pl.BlockSpec(cre.shape, fix3), pl.BlockSpec(cim.shape, fix3),
                  pl.BlockSpec(are.shape, fix3), pl.BlockSpec(aim.shape, fix3),
                  pl.BlockSpec((1, SSM_WIDTH), lambda i: (0, 0))],
        out_specs=pl.BlockSpec((rows, SSM_WIDTH), lambda i: (i, 0)),
        out_shape=jax.ShapeDtypeStruct((n, SSM_WIDTH), BF16),
        scratch_shapes=[pltpu.VMEM((2, 2, batch, SSM_HALF_STATE), F32),
                        pltpu.VMEM((rows, SSM_HALF_STATE), F32),
                        pltpu.VMEM((rows, SSM_HALF_STATE), F32)],
        compiler_params=_params("arbitrary"),
        name="s5_scan",
    )(u, bre, bim, cre, cim, are, aim, d)


def _moba_kernel(q_ref, k_ref, v_ref, o_ref, kpad_ref, vt_ref, km_ref, qt_ref, m_ref, alpha_ref, acc_ref,
                 s_ref, p_ref):
    i = pl.program_id(1)
    blk = MOBA_BLOCK
    nblk = k_ref.shape[0] // blk
    n_sel = min(MOBA_TOPK, nblk - 1)

    @pl.when(i == 0)
    def _build():
        lane = lax.broadcasted_iota(jnp.int32, (blk, LANES), 1)
        is_head_lane = lane < HEAD_DIM
        km_ref[...] = jnp.zeros_like(km_ref)
        ones_row = (lax.broadcasted_iota(jnp.int32, (V_ROWS - HEAD_DIM, blk), 0) == 0)

        def per_block(j, _):
            r0 = pl.multiple_of(j * blk, blk)
            vt = v_ref[pl.ds(r0, blk), :].astype(F32).T
            for h in range(N_HEADS):
                p = h // 2
                kk = k_ref[pl.ds(r0, blk), p * LANES:(p + 1) * LANES].astype(F32)
                if h % 2:
                    kk = pltpu.roll(kk, HEAD_DIM, 1)
                kk = jnp.where(is_head_lane, kk, 0.0)
                km_ref[h, pl.ds(j, 1), :] = jnp.mean(kk, axis=0, keepdims=True)
                kk = jnp.where(lane == HEAD_DIM + j, 1.0, kk)
                kpad_ref[j, :, h * LANES:(h + 1) * LANES] = kk.astype(BF16)
                vt_ref[j, h, 0:HEAD_DIM, :] = vt[h * HEAD_DIM:(h + 1) * HEAD_DIM, :].astype(BF16)
                vt_ref[j, h, HEAD_DIM:V_ROWS, :] = ones_row.astype(BF16)
            return 0

        lax.fori_loop(0, nblk, per_block, 0)

    qt = q_ref[...].astype(F32).T
    blk_row = lax.broadcasted_iota(jnp.int32, (GATE_ROWS, blk), 0)
    past = blk_row < i
    blk_row_f = blk_row.astype(F32)
    zeros_q = jnp.zeros((LANES - HEAD_DIM, blk), F32)
    zeros_pad = jnp.zeros((LANES - HEAD_DIM - GATE_ROWS, blk), F32)
    for h in range(N_HEADS):
        qh = qt[h * HEAD_DIM:(h + 1) * HEAD_DIM, :]
        q_pad = jnp.concatenate([qh, zeros_q], axis=0).astype(BF16)
        km = km_ref[h]
        km_hi = km.astype(BF16)
        km_lo = (km - km_hi.astype(F32)).astype(BF16)
        gate = jnp.where(past, _dot(km_hi, q_pad) + _dot(km_lo, q_pad), -jnp.inf)
        sel = jnp.zeros(gate.shape, jnp.bool_)
        for _ in range(n_sel):
            best = jnp.max(gate, axis=0, keepdims=True)
            cand = (gate == best) & (best > -jnp.inf)
            first = jnp.min(jnp.where(cand, blk_row_f, float(GATE_ROWS)), axis=0, keepdims=True)
            pick = blk_row_f == first
            sel = sel | pick
            gate = jnp.where(pick, -jnp.inf, gate)
        bias = jnp.where(past & ~sel, MASK_NEG, 0.0)
        qt_ref[h] = jnp.concatenate([qh, bias, zeros_pad], axis=0).astype(BF16)
        m_ref[h] = jnp.full((1, blk), -jnp.inf, F32)
        acc_ref[h] = jnp.zeros((V_ROWS, blk), F32)

    n_s, n_p = s_ref.shape[0], p_ref.shape[0]

    def scores(j, h):
        s_ref[h % n_s] = _dot(kpad_ref[j, :, h * LANES:(h + 1) * LANES], qt_ref[h])

    def softmax(h, mask):
        st = s_ref[h % n_s]
        if mask is not None:
            st = jnp.where(mask, st, MASK_NEG)
        m_old = m_ref[h]
        m_new = jnp.maximum(m_old, jnp.max(st, axis=0, keepdims=True))
        alpha_ref[h] = jnp.exp2(m_old - m_new)
        p_ref[h % n_p] = jnp.exp2((st - m_new).astype(BF16))
        m_ref[h] = m_new

    def weighted_values(j, h):
        acc_ref[h] = alpha_ref[h] * acc_ref[h] + _dot(vt_ref[j, h], p_ref[h % n_p])

    def kv_block(j, mask):
        for step in range(N_HEADS + 2):
            if step < N_HEADS:
                scores(j, step)
            if 0 <= step - 1 < N_HEADS:
                softmax(step - 1, mask)
            if 0 <= step - 2 < N_HEADS:
                weighted_values(j, step - 2)

    def past_block(j, _):
        kv_block(j, None)
        return 0

    lax.fori_loop(0, i, past_block, 0)
    key_pos = lax.broadcasted_iota(jnp.int32, (blk, blk), 0)
    qry_pos = lax.broadcasted_iota(jnp.int32, (blk, blk), 1)
    kv_block(i, key_pos <= qry_pos)
    outs = []
    for h in range(N_HEADS):
        acc = acc_ref[h]
        outs.append(acc[0:HEAD_DIM, :] / acc[HEAD_DIM:HEAD_DIM + 1, :])
    o_ref[...] = jnp.concatenate(outs, axis=0).T.astype(BF16)


def _moba(q, k, v, batch):
    n = q.shape[0]
    seq = n // batch
    tq = MOBA_BLOCK
    nblk = seq // MOBA_BLOCK
    view = lambda t: t.reshape(seq, batch * ATTN_WIDTH)
    out = pl.pallas_call(
        _moba_kernel,
        grid=(batch, seq // tq),
        in_specs=[pl.BlockSpec((tq, ATTN_WIDTH), lambda b, i: (i, b)),
                  pl.BlockSpec((seq, ATTN_WIDTH), lambda b, i: (0, b)),
                  pl.BlockSpec((seq, ATTN_WIDTH), lambda b, i: (0, b))],
        out_specs=pl.BlockSpec((tq, ATTN_WIDTH), lambda b, i: (i, b)),
        out_shape=jax.ShapeDtypeStruct((seq, batch * ATTN_WIDTH), BF16),
        scratch_shapes=[pltpu.VMEM((nblk, MOBA_BLOCK, N_HEADS * LANES), BF16),
                        pltpu.VMEM((nblk, N_HEADS, V_ROWS, MOBA_BLOCK), BF16),
                        pltpu.VMEM((N_HEADS, GATE_ROWS, LANES), F32),
                        pltpu.VMEM((N_HEADS, LANES, MOBA_BLOCK), BF16),
                        pltpu.VMEM((N_HEADS, 1, MOBA_BLOCK), F32),
                        pltpu.VMEM((N_HEADS, 1, MOBA_BLOCK), F32),
                        pltpu.VMEM((N_HEADS, V_ROWS, MOBA_BLOCK), F32),
                        pltpu.VMEM((3, MOBA_BLOCK, MOBA_BLOCK), F32),
                        pltpu.VMEM((2, MOBA_BLOCK, MOBA_BLOCK), BF16)],
        compiler_params=_params("parallel", "arbitrary"),
        name="moba_attention",
    )(view(q), view(k), view(v))
    return out.reshape(n, ATTN_WIDTH)


def _mix_kernel(z_ref, a_ref, g_ref, x_ref, wglu_ref, wattn_ref, wout_ref, g2_ref,
                wrh_ref, wrl_ref, rb_ref, xo_ref, lg_ref):
    glu = _dot(z_ref[...], wglu_ref[...])
    y_ssm = glu[:, :D_MODEL] * jax.nn.sigmoid(glu[:, D_MODEL:])
    y_attn = _dot(a_ref[...], wattn_ref[...])
    g = g_ref[...].astype(F32)
    mixed = jax.nn.sigmoid(g[:, :D_MODEL]) * y_ssm + jax.nn.sigmoid(g[:, D_MODEL:]) * y_attn
    x = x_ref[...] + _dot(mixed.astype(BF16), wout_ref[...])
    xo_ref[...] = x
    h2 = _rms(x, g2_ref[...])
    hi = h2.astype(BF16)
    lo = (h2 - hi.astype(F32)).astype(BF16)
    lg_ref[...] = (_dot(hi, wrh_ref[...]) + _dot(lo, wrh_ref[...]) + _dot(hi, wrl_ref[...])
                   + rb_ref[...])


def _mix(z, a, g, x, wglu, wattn, wout, g2, wrh, wrl, rb):
    n = x.shape[0]
    tm = TOKEN_TILE
    row = lambda i: (i, 0)
    fix = lambda i: (0, 0)
    full = lambda t: pl.BlockSpec(t.shape, fix)
    return pl.pallas_call(
        _mix_kernel,
        grid=(n // tm,),
        in_specs=[pl.BlockSpec((tm, SSM_WIDTH), row), pl.BlockSpec((tm, ATTN_WIDTH), row),
                  pl.BlockSpec((tm, 2 * D_MODEL), row), pl.BlockSpec((tm, D_MODEL), row),
                  full(wglu), full(wattn), full(wout), full(g2), full(wrh), full(wrl), full(rb)],
        out_specs=[pl.BlockSpec((tm, D_MODEL), row), pl.BlockSpec((tm, ROUTER_LANES), row)],
        out_shape=[jax.ShapeDtypeStruct((n, D_MODEL), F32),
                   jax.ShapeDtypeStruct((n, ROUTER_LANES), F32)],
        compiler_params=_params("parallel"),
        name="mix_out_router",
    )(z, a, g, x, wglu, wattn, wout, g2, wrh, wrl, rb)


def _row_copy(src_hbm, row, buf, slot, r, sem):
    return pltpu.make_async_copy(src_hbm.at[pl.ds(row, 1), :],
                                 buf.at[slot, pl.ds(r, 1), :], sem.at[slot])


def _expert_kernel(blk_e_ref, nused_ref, tok_ref, x_hbm, g2_ref, wgu_ref, wd_ref, y_ref,
                   xbuf, sem):
    b = pl.program_id(0)
    nused = nused_ref[0]
    slot = b % 2
    rows = EXPERT_ROWS

    def issue(blk, slot):
        def body(r, _):
            _row_copy(x_hbm, tok_ref[blk * rows + r], xbuf, slot, r, sem).start()
            return 0
        lax.fori_loop(0, rows, body, 0, unroll=8)

    @pl.when(b == 0)
    def _():
        issue(0, 0)

    @pl.when(b + 1 < nused)
    def _():
        issue(b + 1, 1 - slot)

    @pl.when(b < nused)
    def _():
        def wait(r, _):
            _row_copy(x_hbm, 0, xbuf, slot, r, sem).wait()
            return 0
        lax.fori_loop(0, rows, wait, 0, unroll=8)
        h = _rms(xbuf[slot], g2_ref[...]).astype(BF16)
        gu = _dot(h, wgu_ref[0])
        act = jax.nn.silu(gu[:, :D_EXPERT]) * gu[:, D_EXPERT:]
        y_ref[...] = _dot(act.astype(BF16), wd_ref[0])

    @pl.when(b >= nused)
    def _():
        y_ref[...] = jnp.zeros_like(y_ref)


def _experts(blk_e, nused, slot_tok, x, g2, wgu, wd):
    nblk = blk_e.shape[0]
    rows = EXPERT_ROWS
    grid_spec = pltpu.PrefetchScalarGridSpec(
        num_scalar_prefetch=3,
        grid=(nblk,),
        in_specs=[pl.BlockSpec(memory_space=pl.ANY),
                  pl.BlockSpec((1, D_MODEL), lambda b, e, n, t: (0, 0)),
                  pl.BlockSpec((1, D_MODEL, 2 * D_EXPERT), lambda b, e, n, t: (e[b], 0, 0)),
                  pl.BlockSpec((1, D_EXPERT, D_MODEL), lambda b, e, n, t: (e[b], 0, 0))],
        out_specs=pl.BlockSpec((rows, D_MODEL), lambda b, e, n, t: (b, 0)),
        scratch_shapes=[pltpu.VMEM((2, rows, D_MODEL), F32),
                        pltpu.SemaphoreType.DMA((2,))],
    )
    return pl.pallas_call(
        _expert_kernel,
        grid_spec=grid_spec,
        out_shape=jax.ShapeDtypeStruct((nblk * rows, D_MODEL), F32),
        compiler_params=_params("arbitrary"),
        name="experts",
    )(blk_e, nused, slot_tok, x, g2, wgu, wd)


def _combine_kernel(pos_ref, x_ref, w_ref, y_hbm, o_ref, ybuf, sem):
    s = pl.program_id(0)
    nsteps = pl.num_programs(0)
    slot = s % 2
    tt = COMBINE_TILE
    rows = EXPERT_TOPK * tt

    def issue(step, slot):
        def body(r, _):
            _row_copy(y_hbm, pos_ref[step * rows + r], ybuf, slot, r, sem).start()
            return 0
        lax.fori_loop(0, rows, body, 0, unroll=8)

    @pl.when(s == 0)
    def _():
        issue(0, 0)

    @pl.when(s + 1 < nsteps)
    def _():
        issue(s + 1, 1 - slot)

    def wait(r, _):
        _row_copy(y_hbm, 0, ybuf, slot, r, sem).wait()
        return 0
    lax.fori_loop(0, rows, wait, 0, unroll=8)
    w = w_ref[...]
    o_ref[...] = (x_ref[...] + w[:, 0:1] * ybuf[slot, 0:tt, :]
                  + w[:, 1:2] * ybuf[slot, tt:rows, :])


def _combine(pos, x, w, y):
    n = x.shape[0]
    tt = COMBINE_TILE
    grid_spec = pltpu.PrefetchScalarGridSpec(
        num_scalar_prefetch=1,
        grid=(n // tt,),
        in_specs=[pl.BlockSpec((tt, D_MODEL), lambda s, p: (s, 0)),
                  pl.BlockSpec((tt, EXPERT_TOPK), lambda s, p: (s, 0)),
                  pl.BlockSpec(memory_space=pl.ANY)],
        out_specs=pl.BlockSpec((tt, D_MODEL), lambda s, p: (s, 0)),
        scratch_shapes=[pltpu.VMEM((2, EXPERT_TOPK * tt, D_MODEL), F32),
                        pltpu.SemaphoreType.DMA((2,))],
    )
    return pl.pallas_call(
        _combine_kernel,
        grid_spec=grid_spec,
        out_shape=jax.ShapeDtypeStruct((n, D_MODEL), F32),
        compiler_params=_params("arbitrary"),
        name="moe_combine",
    )(pos, x, w, y)


def _rope_tables(seq, batch):
    inv = ROPE_THETA ** (-jnp.arange(0, HEAD_DIM, 2, dtype=F32) / HEAD_DIM)
    ang = jnp.arange(seq, dtype=F32)[:, None] * inv[None, :]
    cos, sin = jnp.cos(ang), jnp.sin(ang)
    cos = jnp.concatenate([cos, cos] * (LANES // HEAD_DIM), axis=1)
    sin = jnp.concatenate([-sin, sin] * (LANES // HEAD_DIM), axis=1)
    rep = lambda t: jnp.repeat(t, batch, axis=0)
    return rep(cos), rep(sin)


def _ssm_weights(a_re, a_im, log_dt, b_re, b_im, c_re, c_im):
    dt = jnp.exp(log_dt)[:, None]
    decay = jnp.exp(a_re * dt)
    abar_re = decay * jnp.cos(a_im * dt)
    abar_im = decay * jnp.sin(a_im * dt)
    den = a_re * a_re + a_im * a_im
    num_re = abar_re - 1.0
    f_re = (num_re * a_re + abar_im * a_im) / den
    f_im = (abar_im * a_re - num_re * a_im) / den
    bbar_re = f_re[..., None] * b_re - f_im[..., None] * b_im
    bbar_im = f_re[..., None] * b_im + f_im[..., None] * b_re
    gh = SSM_GROUPS // 2
    eye = jnp.eye(gh, dtype=F32)

    def b_mat(t):
        t = t.reshape(2, gh, SSM_STATE, SSM_GROUP_CH)
        return jnp.einsum('cgph,gk->cghkp', t, eye).reshape(2, SSM_HALF, SSM_HALF_STATE).astype(BF16)

    def c_mat(t):
        t = t.reshape(2, gh, SSM_GROUP_CH, SSM_STATE)
        return jnp.einsum('cghp,gk->cgpkh', t, eye).reshape(2, SSM_HALF_STATE, SSM_HALF).astype(BF16)

    a_vec = lambda t: t.reshape(2, 1, SSM_HALF_STATE)
    return (b_mat(bbar_re), b_mat(bbar_im), c_mat(c_re), c_mat(-c_im), a_vec(abar_re), a_vec(abar_im))


def _route(logits, n):
    glog = logits[:, :N_GROUPS]
    gprob = jax.nn.softmax(glog, axis=-1)
    g_top = jnp.argmax(glog, axis=-1)
    p_g = jnp.take_along_axis(gprob, g_top[:, None], axis=-1)[:, 0]
    elog = logits[:, N_GROUPS:N_GROUPS + N_EXPERTS].reshape(n, N_GROUPS, EXPERTS_PER_GROUP)
    elog = jnp.take_along_axis(elog, g_top[:, None, None], axis=1)[:, 0]
    eprob = jax.nn.softmax(elog, axis=-1)
    top_p, top_e = lax.top_k(eprob, EXPERT_TOPK)
    top_p = top_p / jnp.sum(top_p, axis=-1, keepdims=True)
    weights = p_g[:, None] * top_p
    expert_id = (g_top[:, None] * EXPERTS_PER_GROUP + top_e).astype(jnp.int32)

    n_assign = n * EXPERT_TOPK
    flat_e = expert_id.reshape(-1)
    onehot = (flat_e[:, None] == jnp.arange(N_EXPERTS, dtype=jnp.int32)[None, :]).astype(jnp.int32)
    csum = jnp.cumsum(onehot, axis=0)
    rank = jnp.sum(csum * onehot, axis=1) - 1
    counts = csum[-1]
    padded = (counts + EXPERT_ROWS - 1) // EXPERT_ROWS * EXPERT_ROWS
    pend = jnp.cumsum(padded)
    pstart = pend - padded
    dest = (pstart[flat_e] + rank).astype(jnp.int32)
    cap = n_assign + N_EXPERTS * EXPERT_ROWS
    nblk = cap // EXPERT_ROWS
    flat_tok = jnp.repeat(jnp.arange(n, dtype=jnp.int32), EXPERT_TOPK)
    slot_tok = jnp.zeros((cap,), jnp.int32).at[dest].set(flat_tok)
    blk_e = jnp.minimum(jnp.searchsorted(pend, jnp.arange(nblk) * EXPERT_ROWS, side='right'),
                        N_EXPERTS - 1).astype(jnp.int32)
    nused = (pend[-1:] // EXPERT_ROWS).astype(jnp.int32)
    pos = dest.reshape(n // COMBINE_TILE, COMBINE_TILE, EXPERT_TOPK).transpose(0, 2, 1).reshape(-1)
    return weights, slot_tok, blk_e, nused, pos


def kernel(x, norm1_g, w_in, ssm_a_re, ssm_a_im, ssm_log_dt, ssm_b_re, ssm_b_im, ssm_c_re, ssm_c_im,
           ssm_d, w_glu, q_norm_g, k_norm_g, w_attn, w_out, norm2_g, router_w_group, router_b_group,
           router_w_expert, router_b_expert, w_gate_up, w_down):
    batch, seq, _ = x.shape
    depth = w_in.shape[0]
    n = batch * seq
    assert batch == SUBLANES and seq % MOBA_BLOCK == 0 and n % TOKEN_TILE == 0
    assert seq // MOBA_BLOCK <= GATE_ROWS and seq % SSM_STEPS == 0

    xt = x.transpose(1, 0, 2).reshape(n, D_MODEL)
    cos, sin = _rope_tables(seq, batch)
    idx = jnp.arange(ATTN_WIDTH)
    gsum = (idx[:, None] // HEAD_DIM == idx[None, :] // HEAD_DIM).astype(BF16)
    row = lambda t: t.reshape(1, -1).astype(F32)

    for l in range(depth):
        qg = row(jnp.tile(q_norm_g[l], N_HEADS)) * (HEAD_DIM ** -0.5 * math.log2(math.e))
        kg = row(jnp.tile(k_norm_g[l], N_HEADS))
        u, q, k, v, g = _in_proj(xt, row(norm1_g[l]), w_in[l].astype(BF16), gsum, qg, kg, cos, sin)
        ssm_w = _ssm_weights(ssm_a_re[l], ssm_a_im[l], ssm_log_dt[l], ssm_b_re[l], ssm_b_im[l],
                             ssm_c_re[l], ssm_c_im[l])
        z = _ssm(u, *ssm_w, row(ssm_d[l]), batch)
        attn = _moba(q, k, v, batch)

        w_r = jnp.concatenate([router_w_group[l], router_w_expert[l]], axis=1)
        w_r = jnp.pad(w_r, ((0, 0), (0, ROUTER_LANES - w_r.shape[1])))
        w_rh = w_r.astype(BF16)
        w_rl = (w_r - w_rh.astype(F32)).astype(BF16)
        b_r = jnp.concatenate([router_b_group[l], router_b_expert[l]])
        b_r = row(jnp.pad(b_r, (0, ROUTER_LANES - b_r.shape[0])))
        g2 = row(norm2_g[l])
        xt, logits = _mix(z, attn, g, xt, w_glu[l].astype(BF16), w_attn[l].astype(BF16),
                          w_out[l].astype(BF16), g2, w_rh, w_rl, b_r)

        weights, slot_tok, blk_e, nused, pos = _route(logits, n)
        y = _experts(blk_e, nused, slot_tok, xt, g2, w_gate_up[l].astype(BF16),
                     w_down[l].astype(BF16))
        xt = _combine(pos, xt, weights, y)

    return xt.reshape(seq, batch, D_MODEL).transpose(1, 0, 2)
```

```python
import functools
import math

import jax
import jax.numpy as jnp
from jax import lax
from jax.experimental import pallas as pl
from jax.experimental.pallas import tpu as pltpu

F32 = jnp.float32
BF16 = jnp.bfloat16

D_MODEL = 1024
SSM_GROUPS = 32
SSM_GROUP_CH = 16
SSM_WIDTH = SSM_GROUPS * SSM_GROUP_CH
SSM_STATE = 64
N_HEADS = 8
HEAD_DIM = 64
ATTN_WIDTH = N_HEADS * HEAD_DIM
MOBA_BLOCK = 256
MOBA_TOPK = 3
ROPE_THETA = 10000.0
D_IN = SSM_WIDTH + 3 * ATTN_WIDTH + 2 * D_MODEL
N_GROUPS = 4
EXPERTS_PER_GROUP = 8
N_EXPERTS = N_GROUPS * EXPERTS_PER_GROUP
EXPERT_TOPK = 2
D_EXPERT = 512
EXPERT_ROWS = 256
NORM_EPS = 1e-6

LANES = 128
SUBLANES = 8
VMEM_LIMIT = 56 * 1024 * 1024
MASK_NEG = -1e30

TOKEN_TILE = 512
SSM_STEPS = 64
SSM_HALF = SSM_WIDTH // 2
SSM_HALF_STATE = SSM_GROUPS * SSM_STATE // 2
GATE_ROWS = 16
V_ROWS = HEAD_DIM + 16
SOFTMAX_LAG = 3
VALUES_LAG = 6
ROUTER_LANES = 128
COMBINE_TILE = 256


def _params(*sem, flags=None):
    return pltpu.CompilerParams(dimension_semantics=sem, vmem_limit_bytes=VMEM_LIMIT, flags=flags)


def _dot(a, b):
    return jnp.dot(a, b, preferred_element_type=F32)


def _dot_nt(a, b):
    return lax.dot_general(a, b, (((1,), (1,)), ((), ())), preferred_element_type=F32)


def _rms(x, gain):
    return x * lax.rsqrt(jnp.mean(x * x, axis=-1, keepdims=True) + NORM_EPS) * gain


def _in_proj_kernel(x_ref, g1_ref, w_ref, gs_ref, qg_ref, kg_ref, cos_ref, sin_ref,
                    u_ref, q_ref, k_ref, v_ref, g_ref):
    hb = _rms(x_ref[...], g1_ref[...]).astype(BF16)

    def seg(lo, hi):
        return _dot(hb, w_ref[:, lo:hi])

    o_q = SSM_WIDTH
    o_k = o_q + ATTN_WIDTH
    o_v = o_k + ATTN_WIDTH
    o_g = o_v + ATTN_WIDTH
    u_ref[...] = seg(0, o_q).astype(BF16)
    v_ref[...] = seg(o_v, o_g).astype(BF16)
    g_ref[...] = seg(o_g, D_IN).astype(BF16)

    reps = ATTN_WIDTH // LANES
    cos = jnp.concatenate([cos_ref[...]] * reps, axis=1)
    sin = jnp.concatenate([sin_ref[...]] * reps, axis=1)
    lane = lax.broadcasted_iota(jnp.int32, cos.shape, 1)
    first_half = (lane % HEAD_DIM) < (HEAD_DIM // 2)

    def norm_rope(t, gain):
        ss = _dot((t * t).astype(BF16), gs_ref[...])
        tn = t * lax.rsqrt(ss * (1.0 / HEAD_DIM) + NORM_EPS) * gain
        partner = jnp.where(first_half,
                            pltpu.roll(tn, ATTN_WIDTH - HEAD_DIM // 2, 1),
                            pltpu.roll(tn, HEAD_DIM // 2, 1))
        return tn * cos + partner * sin

    q_ref[...] = norm_rope(seg(o_q, o_k), qg_ref[...]).astype(BF16)
    k_ref[...] = norm_rope(seg(o_k, o_v), kg_ref[...]).astype(BF16)


def _in_proj(x, g1, w_in, gsum, qg, kg, cos, sin):
    n = x.shape[0]
    tm = TOKEN_TILE
    row = lambda i: (i, 0)
    fix = lambda i: (0, 0)
    tiles_per_seq = cos.shape[0] // tm
    pos = lambda i: (i % tiles_per_seq, 0)
    outs = [jax.ShapeDtypeStruct((n, w), BF16)
            for w in (SSM_WIDTH, ATTN_WIDTH, ATTN_WIDTH, ATTN_WIDTH, 2 * D_MODEL)]
    return pl.pallas_call(
        _in_proj_kernel,
        grid=(n // tm,),
        in_specs=[pl.BlockSpec((tm, D_MODEL), row),
                  pl.BlockSpec((1, D_MODEL), fix),
                  pl.BlockSpec((D_MODEL, D_IN), fix),
                  pl.BlockSpec((ATTN_WIDTH, ATTN_WIDTH), fix),
                  pl.BlockSpec((1, ATTN_WIDTH), fix),
                  pl.BlockSpec((1, ATTN_WIDTH), fix),
                  pl.BlockSpec((tm, LANES), pos),
                  pl.BlockSpec((tm, LANES), pos)],
        out_specs=[pl.BlockSpec((tm, o.shape[1]), row) for o in outs],
        out_shape=outs,
        compiler_params=_params("parallel"),
        name="in_proj",
    )(x, g1, w_in, gsum, qg, kg, cos, sin)


def _ssm_kernel(u_ref, bre_ref, bim_ref, cre_ref, cim_ref, are_ref, aim_ref, d_ref,
                z_ref, h_ref, sre_ref, sim_ref, *, batch):
    @pl.when(pl.program_id(0) == 0)
    def _():
        h_ref[...] = jnp.zeros_like(h_ref)

    steps = u_ref.shape[0] // batch
    ys = []
    for c in range(2):
        uc = u_ref[:, c * SSM_HALF:(c + 1) * SSM_HALF]
        sre_ref[...] = _dot(uc, bre_ref[c])
        sim_ref[...] = _dot(uc, bim_ref[c])
        a_re = jnp.broadcast_to(are_ref[c], (batch, SSM_HALF_STATE))
        a_im = jnp.broadcast_to(aim_ref[c], (batch, SSM_HALF_STATE))

        def step(t, carry):
            h_re, h_im = carry
            r0 = pl.multiple_of(t * batch, batch)
            n_re = a_re * h_re - a_im * h_im + sre_ref[pl.ds(r0, batch), :]
            n_im = a_re * h_im + a_im * h_re + sim_ref[pl.ds(r0, batch), :]
            sre_ref[pl.ds(r0, batch), :] = n_re
            sim_ref[pl.ds(r0, batch), :] = n_im
            return n_re, n_im

        h_re, h_im = lax.fori_loop(0, steps, step, (h_ref[c, 0], h_ref[c, 1]), unroll=4)
        h_ref[c, 0] = h_re
        h_ref[c, 1] = h_im
        ys.append(_dot(sre_ref[...].astype(BF16), cre_ref[c])
                  + _dot(sim_ref[...].astype(BF16), cim_ref[c]))
    y = jnp.concatenate(ys, axis=1) + d_ref[...] * u_ref[...].astype(F32)
    z_ref[...] = jax.nn.gelu(y).astype(BF16)


def _ssm(u, bre, bim, cre, cim, are, aim, d, batch):
    n = u.shape[0]
    rows = SSM_STEPS * batch
    fix3 = lambda i: (0, 0, 0)
    return pl.pallas_call(
        functools.partial(_ssm_kernel, batch=batch),
        grid=(n // rows,),
        in_specs=[pl.BlockSpec((rows, SSM_WIDTH), lambda i: (i, 0)),
                  pl.BlockSpec(bre.shape, fix3), pl.BlockSpec(bim.shape, fix3),
                  pl.BlockSpec(cre.shape, fix3), pl.BlockSpec(cim.shape, fix3),
                  pl.BlockSpec(are.shape, fix3), pl.BlockSpec(aim.shape, fix3),
                  pl.BlockSpec((1, SSM_WIDTH), lambda i: (0, 0))],
        out_specs=pl.BlockSpec((rows, SSM_WIDTH), lambda i: (i, 0)),
        out_shape=jax.ShapeDtypeStruct((n, SSM_WIDTH), BF16),
        scratch_shapes=[pltpu.VMEM((2, 2, batch, SSM_HALF_STATE), F32),
                        pltpu.VMEM((rows, SSM_HALF_STATE), F32),
                        pltpu.VMEM((rows, SSM_HALF_STATE), F32)],
        compiler_params=_params("arbitrary"),
        name="s5_scan",
    )(u, bre, bim, cre, cim, are, aim, d)


def _moba_kernel(q_ref, k_ref, v_ref, o_ref, kpad_ref, vt_ref, km_ref, qt_ref, m_ref, alpha_ref, acc_ref,
                 s_ref, p_ref):
    i = pl.program_id(1)
    blk = MOBA_BLOCK
    nblk = k_ref.shape[0] // blk
    n_sel = min(MOBA_TOPK, nblk - 1)

    @pl.when(i == 0)
    def _build():
        lane = lax.broadcasted_iota(jnp.int32, (blk, LANES), 1)
        is_head_lane = lane < HEAD_DIM
        km_ref[...] = jnp.zeros_like(km_ref)
        ones_row = (lax.broadcasted_iota(jnp.int32, (V_ROWS - HEAD_DIM, blk), 0) == 0)

        def per_block(j, _):
            r0 = pl.multiple_of(j * blk, blk)
            vt = v_ref[pl.ds(r0, blk), :].astype(F32).T
            for h in range(N_HEADS):
                p = h // 2
                kk = k_ref[pl.ds(r0, blk), p * LANES:(p + 1) * LANES].astype(F32)
                if h % 2:
                    kk = pltpu.roll(kk, HEAD_DIM, 1)
                kk = jnp.where(is_head_lane, kk, 0.0)
                km_ref[h, pl.ds(j, 1), :] = jnp.mean(kk, axis=0, keepdims=True)
                kk = jnp.where(lane == HEAD_DIM + j, 1.0, kk)
                kpad_ref[j, :, h * LANES:(h + 1) * LANES] = kk.astype(BF16)
                vt_ref[j, h, 0:HEAD_DIM, :] = vt[h * HEAD_DIM:(h + 1) * HEAD_DIM, :].astype(BF16)
                vt_ref[j, h, HEAD_DIM:V_ROWS, :] = ones_row.astype(BF16)
            return 0

        lax.fori_loop(0, nblk, per_block, 0)

    qt = q_ref[...].astype(F32).T
    blk_row = lax.broadcasted_iota(jnp.int32, (GATE_ROWS, blk), 0)
    past = blk_row < i
    blk_row_f = blk_row.astype(F32)
    zeros_q = jnp.zeros((LANES - HEAD_DIM, blk), F32)
    zeros_pad = jnp.zeros((LANES - HEAD_DIM - GATE_ROWS, blk), F32)
    for h in range(N_HEADS):
        qh = qt[h * HEAD_DIM:(h + 1) * HEAD_DIM, :]
        q_pad = jnp.concatenate([qh, zeros_q], axis=0).astype(BF16)
        km = km_ref[h]
        km_hi = km.astype(BF16)
        km_lo = (km - km_hi.astype(F32)).astype(BF16)
        gate = jnp.where(past, _dot(km_hi, q_pad) + _dot(km_lo, q_pad), -jnp.inf)
        sel = jnp.zeros(gate.shape, jnp.bool_)
        for _ in range(n_sel):
            best = jnp.max(gate, axis=0, keepdims=True)
            cand = (gate == best) & (best > -jnp.inf)
            first = jnp.min(jnp.where(cand, blk_row_f, float(GATE_ROWS)), axis=0, keepdims=True)
            pick = blk_row_f == first
            sel = sel | pick
            gate = jnp.where(pick, -jnp.inf, gate)
        bias = jnp.where(past & ~sel, MASK_NEG, 0.0)
        qt_ref[h] = jnp.concatenate([qh, bias, zeros_pad], axis=0).astype(BF16)
        m_ref[h] = jnp.full((1, blk), -jnp.inf, F32)
        acc_ref[h] = jnp.zeros((V_ROWS, blk), F32)

    n_s, n_p = s_ref.shape[0], p_ref.shape[0]

    def scores(j, h):
        s_ref[h % n_s] = _dot(kpad_ref[j, :, h * LANES:(h + 1) * LANES], qt_ref[h])

    def softmax(h, mask):
        st = s_ref[h % n_s]
        if mask is not None:
            st = jnp.where(mask, st, MASK_NEG)
        m_old = m_ref[h]
        m_new = jnp.maximum(m_old, jnp.max(st, axis=0, keepdims=True))
        alpha_ref[h] = jnp.exp2(m_old - m_new)
        p_ref[h % n_p] = jnp.exp2((st - m_new).astype(BF16))
        m_ref[h] = m_new

    def weighted_values(j, h):
        acc_ref[h] = alpha_ref[h] * acc_ref[h] + _dot(vt_ref[j, h], p_ref[h % n_p])

    def kv_block(j, mask):
        for step in range(N_HEADS + VALUES_LAG):
            if step < N_HEADS:
                scores(j, step)
            if 0 <= step - SOFTMAX_LAG < N_HEADS:
                softmax(step - SOFTMAX_LAG, mask)
            if 0 <= step - VALUES_LAG < N_HEADS:
                weighted_values(j, step - VALUES_LAG)

    def past_block(j, _):
        kv_block(j, None)
        return 0

    lax.fori_loop(0, i, past_block, 0)
    key_pos = lax.broadcasted_iota(jnp.int32, (blk, blk), 0)
    qry_pos = lax.broadcasted_iota(jnp.int32, (blk, blk), 1)
    kv_block(i, key_pos <= qry_pos)
    outs = []
    for h in range(N_HEADS):
        acc = acc_ref[h]
        outs.append(acc[0:HEAD_DIM, :] / acc[HEAD_DIM:HEAD_DIM + 1, :])
    o_ref[...] = jnp.concatenate(outs, axis=0).T.astype(BF16)


def _moba(q, k, v, batch):
    n = q.shape[0]
    seq = n // batch
    tq = MOBA_BLOCK
    nblk = seq // MOBA_BLOCK
    return pl.pallas_call(
        _moba_kernel,
        grid=(batch, nblk),
        in_specs=[pl.BlockSpec((tq, ATTN_WIDTH), lambda b, i: (b * nblk + i, 0)),
                  pl.BlockSpec((seq, ATTN_WIDTH), lambda b, i: (b, 0)),
                  pl.BlockSpec((seq, ATTN_WIDTH), lambda b, i: (b, 0))],
        out_specs=pl.BlockSpec((tq, ATTN_WIDTH), lambda b, i: (b * nblk + i, 0)),
        out_shape=jax.ShapeDtypeStruct((n, ATTN_WIDTH), BF16),
        scratch_shapes=[pltpu.VMEM((nblk, MOBA_BLOCK, N_HEADS * LANES), BF16),
                        pltpu.VMEM((nblk, N_HEADS, V_ROWS, MOBA_BLOCK), BF16),
                        pltpu.VMEM((N_HEADS, GATE_ROWS, LANES), F32),
                        pltpu.VMEM((N_HEADS, LANES, MOBA_BLOCK), BF16),
                        pltpu.VMEM((N_HEADS, 1, MOBA_BLOCK), F32),
                        pltpu.VMEM((N_HEADS, 1, MOBA_BLOCK), F32),
                        pltpu.VMEM((N_HEADS, V_ROWS, MOBA_BLOCK), F32),
                        pltpu.VMEM((VALUES_LAG, MOBA_BLOCK, MOBA_BLOCK), F32),
                        pltpu.VMEM((VALUES_LAG - SOFTMAX_LAG + 1, MOBA_BLOCK, MOBA_BLOCK), BF16)],
        compiler_params=_params("parallel", "arbitrary"),
        name="moba_attention",
    )(q, k, v)


def _mix_kernel(z_ref, a_ref, g_ref, x_ref, wglu_ref, wattn_ref, wout_ref, g2_ref,
                wrh_ref, wrl_ref, rb_ref, xo_ref, lg_ref):
    glu = _dot(z_ref[...], wglu_ref[...])
    y_ssm = glu[:, :D_MODEL] * jax.nn.sigmoid(glu[:, D_MODEL:])
    y_attn = _dot(a_ref[...], wattn_ref[...])
    g = g_ref[...].astype(F32)
    mixed = jax.nn.sigmoid(g[:, :D_MODEL]) * y_ssm + jax.nn.sigmoid(g[:, D_MODEL:]) * y_attn
    x = x_ref[...] + _dot(mixed.astype(BF16), wout_ref[...])
    xo_ref[...] = x
    h2 = _rms(x, g2_ref[...])
    hi = h2.astype(BF16)
    lo = (h2 - hi.astype(F32)).astype(BF16)
    lg_ref[...] = (_dot(hi, wrh_ref[...]) + _dot(lo, wrh_ref[...]) + _dot(hi, wrl_ref[...])
                   + rb_ref[...])


def _mix(z, a, g, x, wglu, wattn, wout, g2, wrh, wrl, rb):
    n = x.shape[0]
    tm = TOKEN_TILE
    row = lambda i: (i, 0)
    fix = lambda i: (0, 0)
    full = lambda t: pl.BlockSpec(t.shape, fix)
    return pl.pallas_call(
        _mix_kernel,
        grid=(n // tm,),
        in_specs=[pl.BlockSpec((tm, SSM_WIDTH), row), pl.BlockSpec((tm, ATTN_WIDTH), row),
                  pl.BlockSpec((tm, 2 * D_MODEL), row), pl.BlockSpec((tm, D_MODEL), row),
                  full(wglu), full(wattn), full(wout), full(g2), full(wrh), full(wrl), full(rb)],
        out_specs=[pl.BlockSpec((tm, D_MODEL), row), pl.BlockSpec((tm, ROUTER_LANES), row)],
        out_shape=[jax.ShapeDtypeStruct((n, D_MODEL), F32),
                   jax.ShapeDtypeStruct((n, ROUTER_LANES), F32)],
        compiler_params=_params("parallel"),
        name="mix_out_router",
    )(z, a, g, x, wglu, wattn, wout, g2, wrh, wrl, rb)


def _rank_kernel(e_ref, rank_ref, cnt_ref, carry_ref):
    @pl.when(pl.program_id(0) == 0)
    def _():
        carry_ref[...] = jnp.zeros_like(carry_ref)

    t = e_ref.shape[0]
    lane = lax.broadcasted_iota(jnp.int32, (t, LANES), 1)
    earlier = (lax.broadcasted_iota(jnp.int32, (t, t), 1)
               < lax.broadcasted_iota(jnp.int32, (t, t), 0)).astype(BF16)
    carry = carry_ref[...]
    ranks = []
    for c in range(EXPERT_TOPK):
        hit = e_ref[:, c:c + 1] == lane
        onehot = hit.astype(F32)
        before = _dot(earlier, onehot.astype(BF16)) + carry
        ranks.append(jnp.sum(jnp.where(hit, before, 0.0), axis=1, keepdims=True))
        carry = carry + jnp.sum(onehot, axis=0, keepdims=True)
    carry_ref[...] = carry
    rank_ref[...] = jnp.concatenate(ranks, axis=1).astype(jnp.int32)
    cnt_ref[...] = carry.astype(jnp.int32)


def _rank(expert_id):
    n = expert_id.shape[0]
    t = TOKEN_TILE
    return pl.pallas_call(
        _rank_kernel,
        grid=(n // t,),
        in_specs=[pl.BlockSpec((t, EXPERT_TOPK), lambda i: (i, 0))],
        out_specs=[pl.BlockSpec((t, EXPERT_TOPK), lambda i: (i, 0)),
                   pl.BlockSpec((1, LANES), lambda i: (0, 0))],
        out_shape=[jax.ShapeDtypeStruct((n, EXPERT_TOPK), jnp.int32),
                   jax.ShapeDtypeStruct((1, LANES), jnp.int32)],
        scratch_shapes=[pltpu.VMEM((1, LANES), F32)],
        compiler_params=_params("arbitrary"),
        name="expert_rank",
    )(expert_id)


def _row_dma(src, src_row, dst, dst_row, sem):
    return pltpu.make_async_copy(src.at[pl.ds(src_row, 1), :], dst.at[pl.ds(dst_row, 1), :], sem)


def _wait_rows(src, dst, sem, count):
    def body(r, _):
        _row_dma(src, 0, dst, 0, sem).wait()
        return 0
    lax.fori_loop(0, count, body, 0, unroll=8 if isinstance(count, int) else 1)


def _dispatch_kernel(dest_ref, pad0_ref, npad_ref, x_ref, g2_ref, xg_hbm, hbuf, zrow, sem, zsem):
    s = pl.program_id(0)
    last = pl.num_programs(0) - 1
    slot = s % 2
    tt = x_ref.shape[0]
    rows = EXPERT_TOPK * tt

    @pl.when(s == 0)
    def _():
        zrow[...] = jnp.zeros_like(zrow)

        def per_expert(e, total):
            def fill(r, _):
                _row_dma(zrow, 0, xg_hbm, pad0_ref[e] + r, zsem).start()
                return 0
            lax.fori_loop(0, npad_ref[e], fill, 0)
            return total + npad_ref[e]
        total = lax.fori_loop(0, N_EXPERTS, per_expert, 0)
        _wait_rows(zrow, xg_hbm, zsem, total)

        def unused(r):
            row0 = pl.multiple_of(pad0_ref[N_EXPERTS] + r * SUBLANES, SUBLANES)
            return pltpu.make_async_copy(zrow, xg_hbm.at[pl.ds(row0, SUBLANES), :], zsem)

        def fill_unused(r, _):
            unused(r).start()
            return 0

        def wait_unused(r, _):
            unused(0).wait()
            return 0
        lax.fori_loop(0, npad_ref[N_EXPERTS], fill_unused, 0)
        lax.fori_loop(0, npad_ref[N_EXPERTS], wait_unused, 0)

    hbuf[slot] = _rms(x_ref[...], g2_ref[...])
    src = hbuf.at[slot]
    for r in range(rows):
        _row_dma(src, r % tt, xg_hbm, dest_ref[s * rows + r], sem.at[slot]).start(priority=r % 2)

    @pl.when(s > 0)
    def _():
        _wait_rows(hbuf.at[1 - slot], xg_hbm, sem.at[1 - slot], rows)

    @pl.when(s == last)
    def _():
        _wait_rows(src, xg_hbm, sem.at[slot], rows)


def _dispatch(dest, pad0, npad, x, g2, cap):
    n = x.shape[0]
    tt = COMBINE_TILE
    grid_spec = pltpu.PrefetchScalarGridSpec(
        num_scalar_prefetch=3,
        grid=(n // tt,),
        in_specs=[pl.BlockSpec((tt, D_MODEL), lambda s, d, p, c: (s, 0)),
                  pl.BlockSpec((1, D_MODEL), lambda s, d, p, c: (0, 0))],
        out_specs=pl.BlockSpec(memory_space=pl.ANY),
        scratch_shapes=[pltpu.VMEM((2, tt, D_MODEL), F32),
                        pltpu.VMEM((SUBLANES, D_MODEL), F32),
                        pltpu.SemaphoreType.DMA((2,)),
                        pltpu.SemaphoreType.DMA],
    )
    return pl.pallas_call(
        _dispatch_kernel,
        grid_spec=grid_spec,
        out_shape=jax.ShapeDtypeStruct((cap, D_MODEL), F32),
        compiler_params=_params("arbitrary"),
        name="expert_dispatch",
    )(dest, pad0, npad, x, g2)


def _expert_kernel(blk_e_ref, nused_ref, xg_ref, wgu_ref, wd_ref, y_ref, wgu_bf, wd_bf):
    b = pl.program_id(0)
    prev = blk_e_ref[jnp.maximum(b, 1) - 1]

    @pl.when((b == 0) | (blk_e_ref[b] != prev))
    def _():
        wgu_bf[...] = wgu_ref[0, 0].astype(BF16)
        wd_bf[...] = wd_ref[0, 0].astype(BF16)

    @pl.when(b < nused_ref[0])
    def _():
        gu = _dot(xg_ref[...].astype(BF16), wgu_bf[...])
        act = jax.nn.silu(gu[:, :D_EXPERT]) * gu[:, D_EXPERT:]
        y_ref[...] = _dot(act.astype(BF16), wd_bf[...])

    @pl.when(b >= nused_ref[0])
    def _():
        y_ref[...] = jnp.zeros_like(y_ref)


def _experts(blk_e, nused, xg, w_gate_up, w_down, layer):
    rows = EXPERT_ROWS
    nblk = xg.shape[0] // rows
    used = lambda b, n: jnp.minimum(b, n[0] - 1)
    grid_spec = pltpu.PrefetchScalarGridSpec(
        num_scalar_prefetch=2,
        grid=(nblk,),
        in_specs=[pl.BlockSpec((rows, D_MODEL), lambda b, e, n: (used(b, n), 0)),
                  pl.BlockSpec((1, 1, D_MODEL, 2 * D_EXPERT), lambda b, e, n: (layer, e[b], 0, 0)),
                  pl.BlockSpec((1, 1, D_EXPERT, D_MODEL), lambda b, e, n: (layer, e[b], 0, 0))],
        out_specs=pl.BlockSpec((rows, D_MODEL), lambda b, e, n: (b, 0)),
        scratch_shapes=[pltpu.VMEM((D_MODEL, 2 * D_EXPERT), BF16),
                        pltpu.VMEM((D_EXPERT, D_MODEL), BF16)],
    )
    return pl.pallas_call(
        _expert_kernel,
        grid_spec=grid_spec,
        out_shape=jax.ShapeDtypeStruct((nblk * rows, D_MODEL), F32),
        compiler_params=_params("arbitrary"),
        name="experts",
    )(blk_e, nused, xg, w_gate_up, w_down)


def _combine_kernel(pos_ref, x_ref, w_ref, y_hbm, o_ref, ybuf, sem):
    s = pl.program_id(0)
    nsteps = pl.num_programs(0)
    slot = s % 2
    tt = COMBINE_TILE
    rows = EXPERT_TOPK * tt

    def issue(step, slot):
        dst = ybuf.at[slot]
        for r in range(rows):
            _row_dma(y_hbm, pos_ref[step * rows + r], dst, r, sem.at[slot]).start(priority=r % 2)

    @pl.when(s == 0)
    def _():
        issue(0, 0)

    @pl.when(s + 1 < nsteps)
    def _():
        issue(s + 1, 1 - slot)

    _wait_rows(y_hbm, ybuf.at[slot], sem.at[slot], rows)
    w = w_ref[...]
    o_ref[...] = (x_ref[...] + w[:, 0:1] * ybuf[slot, 0:tt, :]
                  + w[:, 1:2] * ybuf[slot, tt:rows, :])


def _combine(pos, x, w, y):
    n = x.shape[0]
    tt = COMBINE_TILE
    grid_spec = pltpu.PrefetchScalarGridSpec(
        num_scalar_prefetch=1,
        grid=(n // tt,),
        in_specs=[pl.BlockSpec((tt, D_MODEL), lambda s, p: (s, 0)),
                  pl.BlockSpec((tt, EXPERT_TOPK), lambda s, p: (s, 0)),
                  pl.BlockSpec(memory_space=pl.ANY)],
        out_specs=pl.BlockSpec((tt, D_MODEL), lambda s, p: (s, 0)),
        scratch_shapes=[pltpu.VMEM((2, EXPERT_TOPK * tt, D_MODEL), F32),
                        pltpu.SemaphoreType.DMA((2,))],
    )
    return pl.pallas_call(
        _combine_kernel,
        grid_spec=grid_spec,
        out_shape=jax.ShapeDtypeStruct((n, D_MODEL), F32),
        compiler_params=_params("arbitrary"),
        name="moe_combine",
    )(pos, x, w, y)


def _rope_tables(seq):
    inv = ROPE_THETA ** (-jnp.arange(0, HEAD_DIM, 2, dtype=F32) / HEAD_DIM)
    ang = jnp.arange(seq, dtype=F32)[:, None] * inv[None, :]
    cos, sin = jnp.cos(ang), jnp.sin(ang)
    cos = jnp.concatenate([cos, cos] * (LANES // HEAD_DIM), axis=1)
    sin = jnp.concatenate([-sin, sin] * (LANES // HEAD_DIM), axis=1)
    return cos, sin


def _ssm_weights(a_re, a_im, log_dt, b_re, b_im, c_re, c_im):
    dt = jnp.exp(log_dt)[:, None]
    decay = jnp.exp(a_re * dt)
    abar_re = decay * jnp.cos(a_im * dt)
    abar_im = decay * jnp.sin(a_im * dt)
    den = a_re * a_re + a_im * a_im
    num_re = abar_re - 1.0
    f_re = (num_re * a_re + abar_im * a_im) / den
    f_im = (abar_im * a_re - num_re * a_im) / den
    bbar_re = f_re[..., None] * b_re - f_im[..., None] * b_im
    bbar_im = f_re[..., None] * b_im + f_im[..., None] * b_re
    gh = SSM_GROUPS // 2
    eye = jnp.eye(gh, dtype=F32)

    def b_mat(t):
        t = t.reshape(2, gh, SSM_STATE, SSM_GROUP_CH)
        return jnp.einsum('cgph,gk->cghkp', t, eye).reshape(2, SSM_HALF, SSM_HALF_STATE).astype(BF16)

    def c_mat(t):
        t = t.reshape(2, gh, SSM_GROUP_CH, SSM_STATE)
        return jnp.einsum('cghp,gk->cgpkh', t, eye).reshape(2, SSM_HALF_STATE, SSM_HALF).astype(BF16)

    a_vec = lambda t: t.reshape(2, 1, SSM_HALF_STATE)
    return (b_mat(bbar_re), b_mat(bbar_im), c_mat(c_re), c_mat(-c_im), a_vec(abar_re), a_vec(abar_im))


def _route(logits, n):
    glog = logits[:, :N_GROUPS]
    gprob = jax.nn.softmax(glog, axis=-1)
    g_top = jnp.argmax(glog, axis=-1)
    p_g = jnp.take_along_axis(gprob, g_top[:, None], axis=-1)[:, 0]
    elog = logits[:, N_GROUPS:N_GROUPS + N_EXPERTS].reshape(n, N_GROUPS, EXPERTS_PER_GROUP)
    elog = jnp.take_along_axis(elog, g_top[:, None, None], axis=1)[:, 0]
    eprob = jax.nn.softmax(elog, axis=-1)
    top_p, top_e = lax.top_k(eprob, EXPERT_TOPK)
    top_p = top_p / jnp.sum(top_p, axis=-1, keepdims=True)
    weights = p_g[:, None] * top_p
    expert_id = (g_top[:, None] * EXPERTS_PER_GROUP + top_e).astype(jnp.int32)

    rank, cnt = _rank(expert_id)
    counts = cnt[0, :N_EXPERTS]
    padded = (counts + EXPERT_ROWS - 1) // EXPERT_ROWS * EXPERT_ROWS
    pend = jnp.cumsum(padded)
    pstart = pend - padded
    experts = jnp.arange(N_EXPERTS, dtype=jnp.int32)
    start_of = jnp.sum(jnp.where(expert_id[..., None] == experts, pstart, 0), axis=-1)
    dest = (start_of + rank).astype(jnp.int32)
    cap = n * EXPERT_TOPK + N_EXPERTS * EXPERT_ROWS
    nblk = cap // EXPERT_ROWS
    blk_e = jnp.minimum(jnp.searchsorted(pend, jnp.arange(nblk) * EXPERT_ROWS, side='right'),
                        N_EXPERTS - 1).astype(jnp.int32)
    nused = (pend[-1:] // EXPERT_ROWS).astype(jnp.int32)
    pos = dest.reshape(n // COMBINE_TILE, COMBINE_TILE, EXPERT_TOPK).transpose(0, 2, 1).reshape(-1)
    pad0 = jnp.concatenate([pstart + counts, pend[-1:]]).astype(jnp.int32)
    npad = jnp.concatenate([padded - counts, (cap - pend[-1:]) // SUBLANES]).astype(jnp.int32)
    return weights, pos, pad0, npad, blk_e, nused, cap


def kernel(x, norm1_g, w_in, ssm_a_re, ssm_a_im, ssm_log_dt, ssm_b_re, ssm_b_im, ssm_c_re, ssm_c_im,
           ssm_d, w_glu, q_norm_g, k_norm_g, w_attn, w_out, norm2_g, router_w_group, router_b_group,
           router_w_expert, router_b_expert, w_gate_up, w_down):
    batch, seq, _ = x.shape
    depth = w_in.shape[0]
    n = batch * seq
    assert batch == SUBLANES and seq % MOBA_BLOCK == 0 and n % TOKEN_TILE == 0
    assert seq // MOBA_BLOCK <= GATE_ROWS and seq % SSM_STEPS == 0

    xt = x.reshape(n, D_MODEL)
    cos, sin = _rope_tables(seq)
    time_major = lambda t: t.reshape(batch, seq, -1).transpose(1, 0, 2).reshape(n, -1)
    batch_major = lambda t: t.reshape(seq, batch, -1).transpose(1, 0, 2).reshape(n, -1)
    idx = jnp.arange(ATTN_WIDTH)
    gsum = (idx[:, None] // HEAD_DIM == idx[None, :] // HEAD_DIM).astype(BF16)
    row = lambda t: t.reshape(1, -1).astype(F32)

    for l in range(depth):
        qg = row(jnp.tile(q_norm_g[l], N_HEADS)) * (HEAD_DIM ** -0.5 * math.log2(math.e))
        kg = row(jnp.tile(k_norm_g[l], N_HEADS))
        u, q, k, v, g = _in_proj(xt, row(norm1_g[l]), w_in[l].astype(BF16), gsum, qg, kg, cos, sin)
        ssm_w = _ssm_weights(ssm_a_re[l], ssm_a_im[l], ssm_log_dt[l], ssm_b_re[l], ssm_b_im[l],
                             ssm_c_re[l], ssm_c_im[l])
        z = batch_major(_ssm(time_major(u), *ssm_w, row(ssm_d[l]), batch))
        attn = _moba(q, k, v, batch)

        w_r = jnp.concatenate([router_w_group[l], router_w_expert[l]], axis=1)
        w_r = jnp.pad(w_r, ((0, 0), (0, ROUTER_LANES - w_r.shape[1])))
        w_rh = w_r.astype(BF16)
        w_rl = (w_r - w_rh.astype(F32)).astype(BF16)
        b_r = jnp.concatenate([router_b_group[l], router_b_expert[l]])
        b_r = row(jnp.pad(b_r, (0, ROUTER_LANES - b_r.shape[0])))
        g2 = row(norm2_g[l])
        xt, logits = _mix(z, attn, g, xt, w_glu[l].astype(BF16), w_attn[l].astype(BF16),
                          w_out[l].astype(BF16), g2, w_rh, w_rl, b_r)

        weights, pos, pad0, npad, blk_e, nused, cap = _route(logits, n)
        xg = _dispatch(pos, pad0, npad, xt, g2, cap)
        y = _experts(blk_e, nused, xg, w_gate_up, w_down, l)
        xt = _combine(pos, xt, weights, y)

    return xt.reshape(batch, seq, D_MODEL)
```

```python
import functools
import math

import jax
import jax.numpy as jnp
from jax import lax
from jax.experimental import pallas as pl
from jax.experimental.pallas import tpu as pltpu

F32 = jnp.float32
BF16 = jnp.bfloat16

D_MODEL = 1024
SSM_GROUPS = 32
SSM_GROUP_CH = 16
SSM_WIDTH = SSM_GROUPS * SSM_GROUP_CH
SSM_STATE = 64
N_HEADS = 8
HEAD_DIM = 64
ATTN_WIDTH = N_HEADS * HEAD_DIM
MOBA_BLOCK = 256
MOBA_TOPK = 3
ROPE_THETA = 10000.0
D_IN = SSM_WIDTH + 3 * ATTN_WIDTH + 2 * D_MODEL
N_GROUPS = 4
EXPERTS_PER_GROUP = 8
N_EXPERTS = N_GROUPS * EXPERTS_PER_GROUP
EXPERT_TOPK = 2
D_EXPERT = 512
EXPERT_ROWS = 256
NORM_EPS = 1e-6

LANES = 128
SUBLANES = 8
VMEM_LIMIT = 56 * 1024 * 1024
MASK_NEG = -1e30

TOKEN_TILE = 512
SSM_STEPS = 64
SSM_HALF = SSM_WIDTH // 2
SSM_HALF_STATE = SSM_GROUPS * SSM_STATE // 2
GATE_ROWS = 16
V_ROWS = HEAD_DIM + 16
SOFTMAX_LAG = 3
VALUES_LAG = 6
ROUTER_LANES = 128
COMBINE_TILE = 256


def _params(*sem, flags=None):
    return pltpu.CompilerParams(dimension_semantics=sem, vmem_limit_bytes=VMEM_LIMIT, flags=flags)


def _dot(a, b):
    return jnp.dot(a, b, preferred_element_type=F32)


def _dot_nt(a, b):
    return lax.dot_general(a, b, (((1,), (1,)), ((), ())), preferred_element_type=F32)


def _rms(x, gain):
    return x * lax.rsqrt(jnp.mean(x * x, axis=-1, keepdims=True) + NORM_EPS) * gain


def _in_proj_kernel(x_ref, g1_ref, w_ref, gs_ref, qg_ref, kg_ref, cos_ref, sin_ref,
                    u_ref, q_ref, k_ref, v_ref, g_ref):
    hb = _rms(x_ref[...], g1_ref[...]).astype(BF16)

    def seg(lo, hi):
        return _dot(hb, w_ref[:, lo:hi])

    o_q = SSM_WIDTH
    o_k = o_q + ATTN_WIDTH
    o_v = o_k + ATTN_WIDTH
    o_g = o_v + ATTN_WIDTH
    u_ref[...] = seg(0, o_q).astype(BF16)
    v_ref[...] = seg(o_v, o_g).astype(BF16)
    g_ref[...] = seg(o_g, D_IN).astype(BF16)

    reps = ATTN_WIDTH // LANES
    cos = jnp.concatenate([cos_ref[...]] * reps, axis=1)
    sin = jnp.concatenate([sin_ref[...]] * reps, axis=1)
    lane = lax.broadcasted_iota(jnp.int32, cos.shape, 1)
    first_half = (lane % HEAD_DIM) < (HEAD_DIM // 2)

    def norm_rope(t, gain):
        ss = _dot((t * t).astype(BF16), gs_ref[...])
        tn = t * lax.rsqrt(ss * (1.0 / HEAD_DIM) + NORM_EPS) * gain
        partner = jnp.where(first_half,
                            pltpu.roll(tn, ATTN_WIDTH - HEAD_DIM // 2, 1),
                            pltpu.roll(tn, HEAD_DIM // 2, 1))
        return tn * cos + partner * sin

    q_ref[...] = norm_rope(seg(o_q, o_k), qg_ref[...]).astype(BF16)
    k_ref[...] = norm_rope(seg(o_k, o_v), kg_ref[...]).astype(BF16)


def _in_proj(x, g1, w_in, gsum, qg, kg, cos, sin):
    n = x.shape[0]
    tm = TOKEN_TILE
    row = lambda i: (i, 0)
    fix = lambda i: (0, 0)
    tiles_per_seq = cos.shape[0] // tm
    pos = lambda i: (i % tiles_per_seq, 0)
    outs = [jax.ShapeDtypeStruct((n, w), BF16)
            for w in (SSM_WIDTH, ATTN_WIDTH, ATTN_WIDTH, ATTN_WIDTH, 2 * D_MODEL)]
    return pl.pallas_call(
        _in_proj_kernel,
        grid=(n // tm,),
        in_specs=[pl.BlockSpec((tm, D_MODEL), row),
                  pl.BlockSpec((1, D_MODEL), fix),
                  pl.BlockSpec((D_MODEL, D_IN), fix),
                  pl.BlockSpec((ATTN_WIDTH, ATTN_WIDTH), fix),
                  pl.BlockSpec((1, ATTN_WIDTH), fix),
                  pl.BlockSpec((1, ATTN_WIDTH), fix),
                  pl.BlockSpec((tm, LANES), pos),
                  pl.BlockSpec((tm, LANES), pos)],
        out_specs=[pl.BlockSpec((tm, o.shape[1]), row) for o in outs],
        out_shape=outs,
        compiler_params=_params("parallel"),
        name="in_proj",
    )(x, g1, w_in, gsum, qg, kg, cos, sin)


def _ssm_kernel(u_ref, bre_ref, bim_ref, cre_ref, cim_ref, are_ref, aim_ref, d_ref,
                z_ref, h_ref, sre_ref, sim_ref, *, batch):
    @pl.when(pl.program_id(0) == 0)
    def _():
        h_ref[...] = jnp.zeros_like(h_ref)

    steps = u_ref.shape[0] // batch
    ys = []
    for c in range(2):
        uc = u_ref[:, c * SSM_HALF:(c + 1) * SSM_HALF]
        sre_ref[...] = _dot(uc, bre_ref[c])
        sim_ref[...] = _dot(uc, bim_ref[c])
        a_re = jnp.broadcast_to(are_ref[c], (batch, SSM_HALF_STATE))
        a_im = jnp.broadcast_to(aim_ref[c], (batch, SSM_HALF_STATE))

        def step(t, carry):
            h_re, h_im = carry
            r0 = pl.multiple_of(t * batch, batch)
            n_re = a_re * h_re - a_im * h_im + sre_ref[pl.ds(r0, batch), :]
            n_im = a_re * h_im + a_im * h_re + sim_ref[pl.ds(r0, batch), :]
            sre_ref[pl.ds(r0, batch), :] = n_re
            sim_ref[pl.ds(r0, batch), :] = n_im
            return n_re, n_im

        h_re, h_im = lax.fori_loop(0, steps, step, (h_ref[c, 0], h_ref[c, 1]), unroll=4)
        h_ref[c, 0] = h_re
        h_ref[c, 1] = h_im
        ys.append(_dot(sre_ref[...].astype(BF16), cre_ref[c])
                  + _dot(sim_ref[...].astype(BF16), cim_ref[c]))
    y = jnp.concatenate(ys, axis=1) + d_ref[...] * u_ref[...].astype(F32)
    z_ref[...] = jax.nn.gelu(y).astype(BF16)


def _ssm(u, bre, bim, cre, cim, are, aim, d, batch):
    n = u.shape[0]
    rows = SSM_STEPS * batch
    fix3 = lambda i: (0, 0, 0)
    return pl.pallas_call(
        functools.partial(_ssm_kernel, batch=batch),
        grid=(n // rows,),
        in_specs=[pl.BlockSpec((rows, SSM_WIDTH), lambda i: (i, 0)),
                  pl.BlockSpec(bre.shape, fix3), pl.BlockSpec(bim.shape, fix3),
                  pl.BlockSpec(cre.shape, fix3), pl.BlockSpec(cim.shape, fix3),
                  pl.BlockSpec(are.shape, fix3), pl.BlockSpec(aim.shape, fix3),
                  pl.BlockSpec((1, SSM_WIDTH), lambda i: (0, 0))],
        out_specs=pl.BlockSpec((rows, SSM_WIDTH), lambda i: (i, 0)),
        out_shape=jax.ShapeDtypeStruct((n, SSM_WIDTH), BF16),
        scratch_shapes=[pltpu.VMEM((2, 2, batch, SSM_HALF_STATE), F32),
                        pltpu.VMEM((rows, SSM_HALF_STATE), F32),
                        pltpu.VMEM((rows, SSM_HALF_STATE), F32)],
        compiler_params=_params("arbitrary"),
        name="s5_scan",
    )(u, bre, bim, cre, cim, are, aim, d)


def _moba_kernel(q_ref, k_ref, v_ref, o_ref, kpad_ref, vt_ref, km_ref, qt_ref, m_ref, alpha_ref, acc_ref,
                 s_ref, p_ref):
    i = pl.program_id(1)
    blk = MOBA_BLOCK
    nblk = k_ref.shape[0] // blk
    n_sel = min(MOBA_TOPK, nblk - 1)

    @pl.when(i == 0)
    def _build():
        lane = lax.broadcasted_iota(jnp.int32, (blk, LANES), 1)
        is_head_lane = lane < HEAD_DIM
        km_ref[...] = jnp.zeros_like(km_ref)
        ones_row = (lax.broadcasted_iota(jnp.int32, (V_ROWS - HEAD_DIM, blk), 0) == 0)

        def per_block(j, _):
            r0 = pl.multiple_of(j * blk, blk)
            vt = v_ref[pl.ds(r0, blk), :].astype(F32).T
            for h in range(N_HEADS):
                p = h // 2
                kk = k_ref[pl.ds(r0, blk), p * LANES:(p + 1) * LANES].astype(F32)
                if h % 2:
                    kk = pltpu.roll(kk, HEAD_DIM, 1)
                kk = jnp.where(is_head_lane, kk, 0.0)
                km_ref[h, pl.ds(j, 1), :] = jnp.mean(kk, axis=0, keepdims=True)
                kk = jnp.where(lane == HEAD_DIM + j, 1.0, kk)
                kpad_ref[j, :, h * LANES:(h + 1) * LANES] = kk.astype(BF16)
                vt_ref[j, h, 0:HEAD_DIM, :] = vt[h * HEAD_DIM:(h + 1) * HEAD_DIM, :].astype(BF16)
                vt_ref[j, h, HEAD_DIM:V_ROWS, :] = ones_row.astype(BF16)
            return 0

        lax.fori_loop(0, nblk, per_block, 0)

    qt = q_ref[...].astype(F32).T
    blk_row = lax.broadcasted_iota(jnp.int32, (GATE_ROWS, blk), 0)
    past = blk_row < i
    blk_row_f = blk_row.astype(F32)
    zeros_q = jnp.zeros((LANES - HEAD_DIM, blk), F32)
    zeros_pad = jnp.zeros((LANES - HEAD_DIM - GATE_ROWS, blk), F32)
    for h in range(N_HEADS):
        qh = qt[h * HEAD_DIM:(h + 1) * HEAD_DIM, :]
        q_pad = jnp.concatenate([qh, zeros_q], axis=0).astype(BF16)
        km = km_ref[h]
        km_hi = km.astype(BF16)
        km_lo = (km - km_hi.astype(F32)).astype(BF16)
        gate = jnp.where(past, _dot(km_hi, q_pad) + _dot(km_lo, q_pad), -jnp.inf)
        sel = jnp.zeros(gate.shape, jnp.bool_)
        for _ in range(n_sel):
            best = jnp.max(gate, axis=0, keepdims=True)
            cand = (gate == best) & (best > -jnp.inf)
            first = jnp.min(jnp.where(cand, blk_row_f, float(GATE_ROWS)), axis=0, keepdims=True)
            pick = blk_row_f == first
            sel = sel | pick
            gate = jnp.where(pick, -jnp.inf, gate)
        bias = jnp.where(past & ~sel, MASK_NEG, 0.0)
        qt_ref[h] = jnp.concatenate([qh, bias, zeros_pad], axis=0).astype(BF16)
        m_ref[h] = jnp.full((1, blk), -jnp.inf, F32)
        acc_ref[h] = jnp.zeros((V_ROWS, blk), F32)

    n_s, n_p = s_ref.shape[0], p_ref.shape[0]

    def scores(j, h, u):
        s_ref[u % n_s] = _dot(kpad_ref[j, :, h * LANES:(h + 1) * LANES], qt_ref[h])

    def softmax(h, u, mask):
        st = s_ref[u % n_s]
        if mask is not None:
            st = jnp.where(mask, st, MASK_NEG)
        m_old = m_ref[h]
        m_new = jnp.maximum(m_old, jnp.max(st, axis=0, keepdims=True))
        alpha_ref[h] = jnp.exp2(m_old - m_new)
        p_ref[u % n_p] = jnp.exp2((st - m_new).astype(BF16))
        m_ref[h] = m_new

    def weighted_values(j, h, u):
        acc_ref[h] = alpha_ref[h] * acc_ref[h] + _dot(vt_ref[j, h], p_ref[u % n_p])

    def kv_blocks(js, mask):
        units = [(j, h) for j in js for h in range(N_HEADS)]
        for step in range(len(units) + VALUES_LAG):
            if step < len(units):
                scores(*units[step], step)
            u = step - SOFTMAX_LAG
            if 0 <= u < len(units):
                softmax(units[u][1], u, mask)
            u = step - VALUES_LAG
            if 0 <= u < len(units):
                weighted_values(*units[u], u)

    def quad(g, _):
        kv_blocks([g * 4 + d for d in range(4)], None)
        return 0

    n_quads = i // 4
    lax.fori_loop(0, n_quads, quad, 0)

    @pl.when(i % 4 >= 2)
    def _():
        kv_blocks([n_quads * 4, n_quads * 4 + 1], None)

    @pl.when(i % 2 == 1)
    def _():
        kv_blocks([i - 1], None)

    key_pos = lax.broadcasted_iota(jnp.int32, (blk, blk), 0)
    qry_pos = lax.broadcasted_iota(jnp.int32, (blk, blk), 1)
    kv_blocks([i], key_pos <= qry_pos)
    outs = []
    for h in range(N_HEADS):
        acc = acc_ref[h]
        outs.append(acc[0:HEAD_DIM, :] / acc[HEAD_DIM:HEAD_DIM + 1, :])
    o_ref[...] = jnp.concatenate(outs, axis=0).T.astype(BF16)


def _moba(q, k, v, batch):
    n = q.shape[0]
    seq = n // batch
    tq = MOBA_BLOCK
    nblk = seq // MOBA_BLOCK
    return pl.pallas_call(
        _moba_kernel,
        grid=(batch, nblk),
        in_specs=[pl.BlockSpec((tq, ATTN_WIDTH), lambda b, i: (b * nblk + i, 0)),
                  pl.BlockSpec((seq, ATTN_WIDTH), lambda b, i: (b, 0)),
                  pl.BlockSpec((seq, ATTN_WIDTH), lambda b, i: (b, 0))],
        out_specs=pl.BlockSpec((tq, ATTN_WIDTH), lambda b, i: (b * nblk + i, 0)),
        out_shape=jax.ShapeDtypeStruct((n, ATTN_WIDTH), BF16),
        scratch_shapes=[pltpu.VMEM((nblk, MOBA_BLOCK, N_HEADS * LANES), BF16),
                        pltpu.VMEM((nblk, N_HEADS, V_ROWS, MOBA_BLOCK), BF16),
                        pltpu.VMEM((N_HEADS, GATE_ROWS, LANES), F32),
                        pltpu.VMEM((N_HEADS, LANES, MOBA_BLOCK), BF16),
                        pltpu.VMEM((N_HEADS, 1, MOBA_BLOCK), F32),
                        pltpu.VMEM((N_HEADS, 1, MOBA_BLOCK), F32),
                        pltpu.VMEM((N_HEADS, V_ROWS, MOBA_BLOCK), F32),
                        pltpu.VMEM((VALUES_LAG, MOBA_BLOCK, MOBA_BLOCK), F32),
                        pltpu.VMEM((VALUES_LAG - SOFTMAX_LAG + 1, MOBA_BLOCK, MOBA_BLOCK), BF16)],
        compiler_params=_params("parallel", "arbitrary"),
        name="moba_attention",
    )(q, k, v)


def _mix_kernel(z_ref, a_ref, g_ref, x_ref, wglu_ref, wattn_ref, wout_ref, g2_ref,
                wrh_ref, wrl_ref, rb_ref, xo_ref, lg_ref):
    glu = _dot(z_ref[...], wglu_ref[...])
    y_ssm = glu[:, :D_MODEL] * jax.nn.sigmoid(glu[:, D_MODEL:])
    y_attn = _dot(a_ref[...], wattn_ref[...])
    g = g_ref[...].astype(F32)
    mixed = jax.nn.sigmoid(g[:, :D_MODEL]) * y_ssm + jax.nn.sigmoid(g[:, D_MODEL:]) * y_attn
    x = x_ref[...] + _dot(mixed.astype(BF16), wout_ref[...])
    xo_ref[...] = x
    h2 = _rms(x, g2_ref[...])
    hi = h2.astype(BF16)
    lo = (h2 - hi.astype(F32)).astype(BF16)
    lg_ref[...] = (_dot(hi, wrh_ref[...]) + _dot(lo, wrh_ref[...]) + _dot(hi, wrl_ref[...])
                   + rb_ref[...])


def _mix(z, a, g, x, wglu, wattn, wout, g2, wrh, wrl, rb):
    n = x.shape[0]
    tm = TOKEN_TILE
    row = lambda i: (i, 0)
    fix = lambda i: (0, 0)
    full = lambda t: pl.BlockSpec(t.shape, fix)
    return pl.pallas_call(
        _mix_kernel,
        grid=(n // tm,),
        in_specs=[pl.BlockSpec((tm, SSM_WIDTH), row), pl.BlockSpec((tm, ATTN_WIDTH), row),
                  pl.BlockSpec((tm, 2 * D_MODEL), row), pl.BlockSpec((tm, D_MODEL), row),
                  full(wglu), full(wattn), full(wout), full(g2), full(wrh), full(wrl), full(rb)],
        out_specs=[pl.BlockSpec((tm, D_MODEL), row), pl.BlockSpec((tm, ROUTER_LANES), row)],
        out_shape=[jax.ShapeDtypeStruct((n, D_MODEL), F32),
                   jax.ShapeDtypeStruct((n, ROUTER_LANES), F32)],
        compiler_params=_params("parallel"),
        name="mix_out_router",
    )(z, a, g, x, wglu, wattn, wout, g2, wrh, wrl, rb)


def _route_kernel(lg_ref, eid_ref, rank_ref, w_ref, cnt_ref, carry_ref):
    @pl.when(pl.program_id(0) == 0)
    def _():
        carry_ref[...] = jnp.zeros_like(carry_ref)

    lg = lg_ref[...]
    t = lg.shape[0]
    lane = lax.broadcasted_iota(jnp.int32, (t, LANES), 1).astype(F32)

    def first_lane(hit):
        return jnp.min(jnp.where(hit, lane, float(LANES)), axis=1, keepdims=True)

    g_log = jnp.where(lane < N_GROUPS, lg, -jnp.inf)
    g_max = jnp.max(g_log, axis=1, keepdims=True)
    g_top = first_lane(g_log == g_max)
    p_g = 1.0 / jnp.sum(jnp.exp(g_log - g_max), axis=1, keepdims=True)

    lo = N_GROUPS + g_top * EXPERTS_PER_GROUP
    in_group = (lane >= lo) & (lane < lo + EXPERTS_PER_GROUP)
    e_log = jnp.where(in_group, lg, -jnp.inf)
    e_exp = jnp.exp(e_log - jnp.max(e_log, axis=1, keepdims=True))
    prob = jnp.where(in_group, e_exp / jnp.sum(e_exp, axis=1, keepdims=True), -1.0)
    p1 = jnp.max(prob, axis=1, keepdims=True)
    l1 = first_lane(prob == p1)
    prob = jnp.where(lane == l1, -1.0, prob)
    p2 = jnp.max(prob, axis=1, keepdims=True)
    l2 = first_lane(prob == p2)
    w_ref[...] = jnp.concatenate([p_g * (p1 / (p1 + p2)), p_g * (p2 / (p1 + p2))], axis=1)
    eid = [l1 - N_GROUPS, l2 - N_GROUPS]
    eid_ref[...] = jnp.concatenate(eid, axis=1).astype(jnp.int32)

    earlier = (lax.broadcasted_iota(jnp.int32, (t, t), 1)
               < lax.broadcasted_iota(jnp.int32, (t, t), 0)).astype(BF16)
    carry = carry_ref[...]
    ranks = []
    for e in eid:
        hit = lane == e
        onehot = hit.astype(F32)
        before = _dot(earlier, onehot.astype(BF16)) + carry
        ranks.append(jnp.sum(jnp.where(hit, before, 0.0), axis=1, keepdims=True))
        carry = carry + jnp.sum(onehot, axis=0, keepdims=True)
    carry_ref[...] = carry
    rank_ref[...] = jnp.concatenate(ranks, axis=1).astype(jnp.int32)
    cnt_ref[...] = carry.astype(jnp.int32)


def _route_rank(logits):
    n = logits.shape[0]
    t = TOKEN_TILE
    pair = pl.BlockSpec((t, EXPERT_TOPK), lambda i: (i, 0))
    return pl.pallas_call(
        _route_kernel,
        grid=(n // t,),
        in_specs=[pl.BlockSpec((t, ROUTER_LANES), lambda i: (i, 0))],
        out_specs=[pair, pair, pair, pl.BlockSpec((1, LANES), lambda i: (0, 0))],
        out_shape=[jax.ShapeDtypeStruct((n, EXPERT_TOPK), jnp.int32),
                   jax.ShapeDtypeStruct((n, EXPERT_TOPK), jnp.int32),
                   jax.ShapeDtypeStruct((n, EXPERT_TOPK), F32),
                   jax.ShapeDtypeStruct((1, LANES), jnp.int32)],
        scratch_shapes=[pltpu.VMEM((1, LANES), F32)],
        compiler_params=_params("arbitrary"),
        name="route_rank",
    )(logits)


def _row_dma(src, src_row, dst, dst_row, sem):
    return pltpu.make_async_copy(src.at[pl.ds(src_row, 1), :], dst.at[pl.ds(dst_row, 1), :], sem)


def _wait_rows(src, dst, sem, count):
    def body(r, _):
        _row_dma(src, 0, dst, 0, sem).wait()
        return 0
    lax.fori_loop(0, count, body, 0, unroll=8 if isinstance(count, int) else 1)


def _dispatch_kernel(dest_ref, pad0_ref, npad_ref, x_ref, g2_ref, xg_hbm, hbuf, zrow, sem, zsem):
    s = pl.program_id(0)
    last = pl.num_programs(0) - 1
    slot = s % 2
    tt = x_ref.shape[0]
    rows = EXPERT_TOPK * tt

    @pl.when(s == 0)
    def _():
        zrow[...] = jnp.zeros_like(zrow)

        def per_expert(e, total):
            def fill(r, _):
                _row_dma(zrow, 0, xg_hbm, pad0_ref[e] + r, zsem).start()
                return 0
            lax.fori_loop(0, npad_ref[e], fill, 0)
            return total + npad_ref[e]
        total = lax.fori_loop(0, N_EXPERTS, per_expert, 0)
        _wait_rows(zrow, xg_hbm, zsem, total)

        def unused(r):
            row0 = pl.multiple_of(pad0_ref[N_EXPERTS] + r * SUBLANES, SUBLANES)
            return pltpu.make_async_copy(zrow, xg_hbm.at[pl.ds(row0, SUBLANES), :], zsem)

        def fill_unused(r, _):
            unused(r).start()
            return 0

        def wait_unused(r, _):
            unused(0).wait()
            return 0
        lax.fori_loop(0, npad_ref[N_EXPERTS], fill_unused, 0)
        lax.fori_loop(0, npad_ref[N_EXPERTS], wait_unused, 0)

    hbuf[slot] = _rms(x_ref[...], g2_ref[...])
    src = hbuf.at[slot]
    for r in range(rows):
        _row_dma(src, r % tt, xg_hbm, dest_ref[s * rows + r], sem.at[slot]).start(priority=r % 2)

    @pl.when(s > 0)
    def _():
        _wait_rows(hbuf.at[1 - slot], xg_hbm, sem.at[1 - slot], rows)

    @pl.when(s == last)
    def _():
        _wait_rows(src, xg_hbm, sem.at[slot], rows)


def _dispatch(dest, pad0, npad, x, g2, cap):
    n = x.shape[0]
    tt = COMBINE_TILE
    grid_spec = pltpu.PrefetchScalarGridSpec(
        num_scalar_prefetch=3,
        grid=(n // tt,),
        in_specs=[pl.BlockSpec((tt, D_MODEL), lambda s, d, p, c: (s, 0)),
                  pl.BlockSpec((1, D_MODEL), lambda s, d, p, c: (0, 0))],
        out_specs=pl.BlockSpec(memory_space=pl.ANY),
        scratch_shapes=[pltpu.VMEM((2, tt, D_MODEL), F32),
                        pltpu.VMEM((SUBLANES, D_MODEL), F32),
                        pltpu.SemaphoreType.DMA((2,)),
                        pltpu.SemaphoreType.DMA],
    )
    return pl.pallas_call(
        _dispatch_kernel,
        grid_spec=grid_spec,
        out_shape=jax.ShapeDtypeStruct((cap, D_MODEL), F32),
        compiler_params=_params("arbitrary"),
        name="expert_dispatch",
    )(dest, pad0, npad, x, g2)


def _expert_kernel(blk_e_ref, nused_ref, xg_ref, wgu_ref, wd_ref, y_ref, wgu_bf, wd_bf):
    b = pl.program_id(0)
    prev = blk_e_ref[jnp.maximum(b, 1) - 1]

    @pl.when((b == 0) | (blk_e_ref[b] != prev))
    def _():
        wgu_bf[...] = wgu_ref[0, 0].astype(BF16)
        wd_bf[...] = wd_ref[0, 0].astype(BF16)

    @pl.when(b < nused_ref[0])
    def _():
        gu = _dot(xg_ref[...].astype(BF16), wgu_bf[...])
        act = jax.nn.silu(gu[:, :D_EXPERT]) * gu[:, D_EXPERT:]
        y_ref[...] = _dot(act.astype(BF16), wd_bf[...])

    @pl.when(b >= nused_ref[0])
    def _():
        y_ref[...] = jnp.zeros_like(y_ref)


def _experts(blk_e, nused, xg, w_gate_up, w_down, layer):
    rows = EXPERT_ROWS
    nblk = xg.shape[0] // rows
    used = lambda b, n: jnp.minimum(b, n[0] - 1)
    grid_spec = pltpu.PrefetchScalarGridSpec(
        num_scalar_prefetch=2,
        grid=(nblk,),
        in_specs=[pl.BlockSpec((rows, D_MODEL), lambda b, e, n: (used(b, n), 0)),
                  pl.BlockSpec((1, 1, D_MODEL, 2 * D_EXPERT), lambda b, e, n: (layer, e[b], 0, 0)),
                  pl.BlockSpec((1, 1, D_EXPERT, D_MODEL), lambda b, e, n: (layer, e[b], 0, 0))],
        out_specs=pl.BlockSpec((rows, D_MODEL), lambda b, e, n: (b, 0)),
        scratch_shapes=[pltpu.VMEM((D_MODEL, 2 * D_EXPERT), BF16),
                        pltpu.VMEM((D_EXPERT, D_MODEL), BF16)],
    )
    return pl.pallas_call(
        _expert_kernel,
        grid_spec=grid_spec,
        out_shape=jax.ShapeDtypeStruct((nblk * rows, D_MODEL), F32),
        compiler_params=_params("arbitrary"),
        name="experts",
    )(blk_e, nused, xg, w_gate_up, w_down)


def _combine_kernel(pos_ref, x_ref, w_ref, y_hbm, o_ref, ybuf, sem):
    s = pl.program_id(0)
    nsteps = pl.num_programs(0)
    slot = s % 2
    tt = COMBINE_TILE
    rows = EXPERT_TOPK * tt

    def issue(step, slot):
        dst = ybuf.at[slot]
        for r in range(rows):
            _row_dma(y_hbm, pos_ref[step * rows + r], dst, r, sem.at[slot]).start(priority=r % 2)

    @pl.when(s == 0)
    def _():
        issue(0, 0)

    @pl.when(s + 1 < nsteps)
    def _():
        issue(s + 1, 1 - slot)

    _wait_rows(y_hbm, ybuf.at[slot], sem.at[slot], rows)
    w = w_ref[...]
    o_ref[...] = (x_ref[...] + w[:, 0:1] * ybuf[slot, 0:tt, :]
                  + w[:, 1:2] * ybuf[slot, tt:rows, :])


def _combine(pos, x, w, y):
    n = x.shape[0]
    tt = COMBINE_TILE
    grid_spec = pltpu.PrefetchScalarGridSpec(
        num_scalar_prefetch=1,
        grid=(n // tt,),
        in_specs=[pl.BlockSpec((tt, D_MODEL), lambda s, p: (s, 0)),
                  pl.BlockSpec((tt, EXPERT_TOPK), lambda s, p: (s, 0)),
                  pl.BlockSpec(memory_space=pl.ANY)],
        out_specs=pl.BlockSpec((tt, D_MODEL), lambda s, p: (s, 0)),
        scratch_shapes=[pltpu.VMEM((2, EXPERT_TOPK * tt, D_MODEL), F32),
                        pltpu.SemaphoreType.DMA((2,))],
    )
    return pl.pallas_call(
        _combine_kernel,
        grid_spec=grid_spec,
        out_shape=jax.ShapeDtypeStruct((n, D_MODEL), F32),
        compiler_params=_params("arbitrary"),
        name="moe_combine",
    )(pos, x, w, y)


def _rope_tables(seq):
    inv = ROPE_THETA ** (-jnp.arange(0, HEAD_DIM, 2, dtype=F32) / HEAD_DIM)
    ang = jnp.arange(seq, dtype=F32)[:, None] * inv[None, :]
    cos, sin = jnp.cos(ang), jnp.sin(ang)
    cos = jnp.concatenate([cos, cos] * (LANES // HEAD_DIM), axis=1)
    sin = jnp.concatenate([-sin, sin] * (LANES // HEAD_DIM), axis=1)
    return cos, sin


def _ssm_weights(a_re, a_im, log_dt, b_re, b_im, c_re, c_im):
    dt = jnp.exp(log_dt)[:, None]
    decay = jnp.exp(a_re * dt)
    abar_re = decay * jnp.cos(a_im * dt)
    abar_im = decay * jnp.sin(a_im * dt)
    den = a_re * a_re + a_im * a_im
    num_re = abar_re - 1.0
    f_re = (num_re * a_re + abar_im * a_im) / den
    f_im = (abar_im * a_re - num_re * a_im) / den
    bbar_re = f_re[..., None] * b_re - f_im[..., None] * b_im
    bbar_im = f_re[..., None] * b_im + f_im[..., None] * b_re
    gh = SSM_GROUPS // 2
    eye = jnp.eye(gh, dtype=F32)

    def b_mat(t):
        t = t.reshape(2, gh, SSM_STATE, SSM_GROUP_CH)
        return jnp.einsum('cgph,gk->cghkp', t, eye).reshape(2, SSM_HALF, SSM_HALF_STATE).astype(BF16)

    def c_mat(t):
        t = t.reshape(2, gh, SSM_GROUP_CH, SSM_STATE)
        return jnp.einsum('cghp,gk->cgpkh', t, eye).reshape(2, SSM_HALF_STATE, SSM_HALF).astype(BF16)

    a_vec = lambda t: t.reshape(2, 1, SSM_HALF_STATE)
    return (b_mat(bbar_re), b_mat(bbar_im), c_mat(c_re), c_mat(-c_im), a_vec(abar_re), a_vec(abar_im))


def _route(logits, n):
    expert_id, rank, weights, cnt = _route_rank(logits)
    counts = cnt[0, :N_EXPERTS]
    padded = (counts + EXPERT_ROWS - 1) // EXPERT_ROWS * EXPERT_ROWS
    pend = jnp.cumsum(padded)
    pstart = pend - padded
    experts = jnp.arange(N_EXPERTS, dtype=jnp.int32)
    start_of = jnp.sum(jnp.where(expert_id[..., None] == experts, pstart, 0), axis=-1)
    dest = (start_of + rank).astype(jnp.int32)
    cap = n * EXPERT_TOPK + N_EXPERTS * EXPERT_ROWS
    nblk = cap // EXPERT_ROWS
    blk_e = jnp.minimum(jnp.searchsorted(pend, jnp.arange(nblk) * EXPERT_ROWS, side='right'),
                        N_EXPERTS - 1).astype(jnp.int32)
    nused = (pend[-1:] // EXPERT_ROWS).astype(jnp.int32)
    pos = dest.reshape(n // COMBINE_TILE, COMBINE_TILE, EXPERT_TOPK).transpose(0, 2, 1).reshape(-1)
    pad0 = jnp.concatenate([pstart + counts, pend[-1:]]).astype(jnp.int32)
    npad = jnp.concatenate([padded - counts, (cap - pend[-1:]) // SUBLANES]).astype(jnp.int32)
    return weights, pos, pad0, npad, blk_e, nused, cap


def kernel(x, norm1_g, w_in, ssm_a_re, ssm_a_im, ssm_log_dt, ssm_b_re, ssm_b_im, ssm_c_re, ssm_c_im,
           ssm_d, w_glu, q_norm_g, k_norm_g, w_attn, w_out, norm2_g, router_w_group, router_b_group,
           router_w_expert, router_b_expert, w_gate_up, w_down):
    batch, seq, _ = x.shape
    depth = w_in.shape[0]
    n = batch * seq
    assert batch == SUBLANES and seq % MOBA_BLOCK == 0 and n % TOKEN_TILE == 0
    assert seq // MOBA_BLOCK <= GATE_ROWS and seq % SSM_STEPS == 0

    xt = x.reshape(n, D_MODEL)
    cos, sin = _rope_tables(seq)
    time_major = lambda t: t.reshape(batch, seq, -1).transpose(1, 0, 2).reshape(n, -1)
    batch_major = lambda t: t.reshape(seq, batch, -1).transpose(1, 0, 2).reshape(n, -1)
    idx = jnp.arange(ATTN_WIDTH)
    gsum = (idx[:, None] // HEAD_DIM == idx[None, :] // HEAD_DIM).astype(BF16)
    row = lambda t: t.reshape(1, -1).astype(F32)

    for l in range(depth):
        qg = row(jnp.tile(q_norm_g[l], N_HEADS)) * (HEAD_DIM ** -0.5 * math.log2(math.e))
        kg = row(jnp.tile(k_norm_g[l], N_HEADS))
        u, q, k, v, g = _in_proj(xt, row(norm1_g[l]), w_in[l].astype(BF16), gsum, qg, kg, cos, sin)
        ssm_w = _ssm_weights(ssm_a_re[l], ssm_a_im[l], ssm_log_dt[l], ssm_b_re[l], ssm_b_im[l],
                             ssm_c_re[l], ssm_c_im[l])
        z = batch_major(_ssm(time_major(u), *ssm_w, row(ssm_d[l]), batch))
        attn = _moba(q, k, v, batch)

        w_r = jnp.concatenate([router_w_group[l], router_w_expert[l]], axis=1)
        w_r = jnp.pad(w_r, ((0, 0), (0, ROUTER_LANES - w_r.shape[1])))
        w_rh = w_r.astype(BF16)
        w_rl = (w_r - w_rh.astype(F32)).astype(BF16)
        b_r = jnp.concatenate([router_b_group[l], router_b_expert[l]])
        b_r = row(jnp.pad(b_r, (0, ROUTER_LANES - b_r.shape[0])))
        g2 = row(norm2_g[l])
        xt, logits = _mix(z, attn, g, xt, w_glu[l].astype(BF16), w_attn[l].astype(BF16),
                          w_out[l].astype(BF16), g2, w_rh, w_rl, b_r)

        weights, pos, pad0, npad, blk_e, nused, cap = _route(logits, n)
        xg = _dispatch(pos, pad0, npad, xt, g2, cap)
        y = _experts(blk_e, nused, xg, w_gate_up, w_down, l)
        xt = _combine(pos, xt, weights, y)

    return xt.reshape(batch, seq, D_MODEL)
```

```python
import functools
import math

import jax
import jax.numpy as jnp
from jax import lax
from jax.experimental import pallas as pl
from jax.experimental.pallas import tpu as pltpu

F32 = jnp.float32
BF16 = jnp.bfloat16

D_MODEL = 1024
SSM_GROUPS = 32
SSM_GROUP_CH = 16
SSM_WIDTH = SSM_GROUPS * SSM_GROUP_CH
SSM_STATE = 64
N_HEADS = 8
HEAD_DIM = 64
ATTN_WIDTH = N_HEADS * HEAD_DIM
MOBA_BLOCK = 256
MOBA_TOPK = 3
ROPE_THETA = 10000.0
D_IN = SSM_WIDTH + 3 * ATTN_WIDTH + 2 * D_MODEL
N_GROUPS = 4
EXPERTS_PER_GROUP = 8
N_EXPERTS = N_GROUPS * EXPERTS_PER_GROUP
EXPERT_TOPK = 2
D_EXPERT = 512
EXPERT_ROWS = 256
NORM_EPS = 1e-6

LANES = 128
SUBLANES = 8
VMEM_LIMIT = 56 * 1024 * 1024
MASK_NEG = -1e30

TOKEN_TILE = 512
SSM_STEPS = 64
SSM_HALF = SSM_WIDTH // 2
SSM_HALF_STATE = SSM_GROUPS * SSM_STATE // 2
GATE_ROWS = 16
V_ROWS = HEAD_DIM + 16
SOFTMAX_LAG = 3
VALUES_LAG = 6
ROUTER_LANES = 128
COMBINE_TILE = 256
ROW_TILES = D_MODEL // LANES


def _params(*sem, flags=None):
    return pltpu.CompilerParams(dimension_semantics=sem, vmem_limit_bytes=VMEM_LIMIT, flags=flags)


def _dot(a, b):
    return jnp.dot(a, b, preferred_element_type=F32)


def _dot_nt(a, b):
    return lax.dot_general(a, b, (((1,), (1,)), ((), ())), preferred_element_type=F32)


def _rms(x, gain):
    return x * lax.rsqrt(jnp.mean(x * x, axis=-1, keepdims=True) + NORM_EPS) * gain


def _in_proj_kernel(x_ref, g1_ref, w_ref, gs_ref, qg_ref, kg_ref, cos_ref, sin_ref,
                    u_ref, q_ref, k_ref, v_ref, g_ref):
    hb = _rms(x_ref[...], g1_ref[...]).astype(BF16)

    def seg(lo, hi):
        return _dot(hb, w_ref[:, lo:hi])

    o_q = SSM_WIDTH
    o_k = o_q + ATTN_WIDTH
    o_v = o_k + ATTN_WIDTH
    o_g = o_v + ATTN_WIDTH
    u_ref[...] = seg(0, o_q).astype(BF16)
    v_ref[...] = seg(o_v, o_g).astype(BF16)
    g_ref[...] = seg(o_g, D_IN).astype(BF16)

    reps = ATTN_WIDTH // LANES
    cos = jnp.concatenate([cos_ref[...]] * reps, axis=1)
    sin = jnp.concatenate([sin_ref[...]] * reps, axis=1)
    lane = lax.broadcasted_iota(jnp.int32, cos.shape, 1)
    first_half = (lane % HEAD_DIM) < (HEAD_DIM // 2)

    def norm_rope(t, gain):
        ss = _dot((t * t).astype(BF16), gs_ref[...])
        tn = t * lax.rsqrt(ss * (1.0 / HEAD_DIM) + NORM_EPS) * gain
        partner = jnp.where(first_half,
                            pltpu.roll(tn, ATTN_WIDTH - HEAD_DIM // 2, 1),
                            pltpu.roll(tn, HEAD_DIM // 2, 1))
        return tn * cos + partner * sin

    q_ref[...] = norm_rope(seg(o_q, o_k), qg_ref[...]).astype(BF16)
    k_ref[...] = norm_rope(seg(o_k, o_v), kg_ref[...]).astype(BF16)


def _in_proj(x, g1, w_in, gsum, qg, kg, cos, sin):
    n = x.shape[0]
    tm = TOKEN_TILE
    row = lambda i: (i, 0)
    fix = lambda i: (0, 0)
    tiles_per_seq = cos.shape[0] // tm
    pos = lambda i: (i % tiles_per_seq, 0)
    outs = [jax.ShapeDtypeStruct((n, w), BF16)
            for w in (SSM_WIDTH, ATTN_WIDTH, ATTN_WIDTH, ATTN_WIDTH, 2 * D_MODEL)]
    return pl.pallas_call(
        _in_proj_kernel,
        grid=(n // tm,),
        in_specs=[pl.BlockSpec((tm, D_MODEL), row),
                  pl.BlockSpec((1, D_MODEL), fix),
                  pl.BlockSpec((D_MODEL, D_IN), fix),
                  pl.BlockSpec((ATTN_WIDTH, ATTN_WIDTH), fix),
                  pl.BlockSpec((1, ATTN_WIDTH), fix),
                  pl.BlockSpec((1, ATTN_WIDTH), fix),
                  pl.BlockSpec((tm, LANES), pos),
                  pl.BlockSpec((tm, LANES), pos)],
        out_specs=[pl.BlockSpec((tm, o.shape[1]), row) for o in outs],
        out_shape=outs,
        compiler_params=_params("parallel"),
        name="in_proj",
    )(x, g1, w_in, gsum, qg, kg, cos, sin)


def _ssm_kernel(u_ref, bre_ref, bim_ref, cre_ref, cim_ref, are_ref, aim_ref, d_ref,
                z_ref, h_ref, sre_ref, sim_ref, *, batch):
    @pl.when(pl.program_id(0) == 0)
    def _():
        h_ref[...] = jnp.zeros_like(h_ref)

    steps = u_ref.shape[0] // batch
    ys = []
    for c in range(2):
        uc = u_ref[:, c * SSM_HALF:(c + 1) * SSM_HALF]
        sre_ref[...] = _dot(uc, bre_ref[c])
        sim_ref[...] = _dot(uc, bim_ref[c])
        a_re = jnp.broadcast_to(are_ref[c], (batch, SSM_HALF_STATE))
        a_im = jnp.broadcast_to(aim_ref[c], (batch, SSM_HALF_STATE))

        def step(t, carry):
            h_re, h_im = carry
            r0 = pl.multiple_of(t * batch, batch)
            n_re = a_re * h_re - a_im * h_im + sre_ref[pl.ds(r0, batch), :]
            n_im = a_re * h_im + a_im * h_re + sim_ref[pl.ds(r0, batch), :]
            sre_ref[pl.ds(r0, batch), :] = n_re
            sim_ref[pl.ds(r0, batch), :] = n_im
            return n_re, n_im

        h_re, h_im = lax.fori_loop(0, steps, step, (h_ref[c, 0], h_ref[c, 1]), unroll=4)
        h_ref[c, 0] = h_re
        h_ref[c, 1] = h_im
        ys.append(_dot(sre_ref[...].astype(BF16), cre_ref[c])
                  + _dot(sim_ref[...].astype(BF16), cim_ref[c]))
    y = jnp.concatenate(ys, axis=1) + d_ref[...] * u_ref[...].astype(F32)
    z_ref[...] = jax.nn.gelu(y).astype(BF16)


def _ssm(u, bre, bim, cre, cim, are, aim, d, batch):
    n = u.shape[0]
    rows = SSM_STEPS * batch
    fix3 = lambda i: (0, 0, 0)
    return pl.pallas_call(
        functools.partial(_ssm_kernel, batch=batch),
        grid=(n // rows,),
        in_specs=[pl.BlockSpec((rows, SSM_WIDTH), lambda i: (i, 0)),
                  pl.BlockSpec(bre.shape, fix3), pl.BlockSpec(bim.shape, fix3),
                  pl.BlockSpec(cre.shape, fix3), pl.BlockSpec(cim.shape, fix3),
                  pl.BlockSpec(are.shape, fix3), pl.BlockSpec(aim.shape, fix3),
                  pl.BlockSpec((1, SSM_WIDTH), lambda i: (0, 0))],
        out_specs=pl.BlockSpec((rows, SSM_WIDTH), lambda i: (i, 0)),
        out_shape=jax.ShapeDtypeStruct((n, SSM_WIDTH), BF16),
        scratch_shapes=[pltpu.VMEM((2, 2, batch, SSM_HALF_STATE), F32),
                        pltpu.VMEM((rows, SSM_HALF_STATE), F32),
                        pltpu.VMEM((rows, SSM_HALF_STATE), F32)],
        compiler_params=_params("arbitrary"),
        name="s5_scan",
    )(u, bre, bim, cre, cim, are, aim, d)


def _moba_kernel(q_ref, k_ref, v_ref, o_ref, kpad_ref, vt_ref, km_ref, qt_ref, m_ref, alpha_ref, acc_ref,
                 s_ref, p_ref):
    i = pl.program_id(1)
    blk = MOBA_BLOCK
    nblk = k_ref.shape[0] // blk
    n_sel = min(MOBA_TOPK, nblk - 1)

    @pl.when(i == 0)
    def _build():
        lane = lax.broadcasted_iota(jnp.int32, (blk, LANES), 1)
        is_head_lane = lane < HEAD_DIM
        km_ref[...] = jnp.zeros_like(km_ref)
        ones_row = (lax.broadcasted_iota(jnp.int32, (V_ROWS - HEAD_DIM, blk), 0) == 0)

        def per_block(j, _):
            r0 = pl.multiple_of(j * blk, blk)
            vt = v_ref[pl.ds(r0, blk), :].astype(F32).T
            for h in range(N_HEADS):
                p = h // 2
                kk = k_ref[pl.ds(r0, blk), p * LANES:(p + 1) * LANES].astype(F32)
                if h % 2:
                    kk = pltpu.roll(kk, HEAD_DIM, 1)
                kk = jnp.where(is_head_lane, kk, 0.0)
                km_ref[h, pl.ds(j, 1), :] = jnp.mean(kk, axis=0, keepdims=True)
                kk = jnp.where(lane == HEAD_DIM + j, 1.0, kk)
                kpad_ref[j, :, h * LANES:(h + 1) * LANES] = kk.astype(BF16)
                vt_ref[j, h, 0:HEAD_DIM, :] = vt[h * HEAD_DIM:(h + 1) * HEAD_DIM, :].astype(BF16)
                vt_ref[j, h, HEAD_DIM:V_ROWS, :] = ones_row.astype(BF16)
            return 0

        lax.fori_loop(0, nblk, per_block, 0)

    qt = q_ref[...].astype(F32).T
    blk_row = lax.broadcasted_iota(jnp.int32, (GATE_ROWS, blk), 0)
    past = blk_row < i
    blk_row_f = blk_row.astype(F32)
    zeros_q = jnp.zeros((LANES - HEAD_DIM, blk), F32)
    zeros_pad = jnp.zeros((LANES - HEAD_DIM - GATE_ROWS, blk), F32)
    for h in range(N_HEADS):
        qh = qt[h * HEAD_DIM:(h + 1) * HEAD_DIM, :]
        q_pad = jnp.concatenate([qh, zeros_q], axis=0).astype(BF16)
        km = km_ref[h]
        km_hi = km.astype(BF16)
        km_lo = (km - km_hi.astype(F32)).astype(BF16)
        gate = jnp.where(past, _dot(km_hi, q_pad) + _dot(km_lo, q_pad), -jnp.inf)
        sel = jnp.zeros(gate.shape, jnp.bool_)
        for _ in range(n_sel):
            best = jnp.max(gate, axis=0, keepdims=True)
            cand = (gate == best) & (best > -jnp.inf)
            first = jnp.min(jnp.where(cand, blk_row_f, float(GATE_ROWS)), axis=0, keepdims=True)
            pick = blk_row_f == first
            sel = sel | pick
            gate = jnp.where(pick, -jnp.inf, gate)
        bias = jnp.where(past & ~sel, MASK_NEG, 0.0)
        qt_ref[h] = jnp.concatenate([qh, bias, zeros_pad], axis=0).astype(BF16)
        m_ref[h] = jnp.full((1, blk), -jnp.inf, F32)
        acc_ref[h] = jnp.zeros((V_ROWS, blk), F32)

    n_s, n_p = s_ref.shape[0], p_ref.shape[0]

    def scores(j, h, u):
        s_ref[u % n_s] = _dot(kpad_ref[j, :, h * LANES:(h + 1) * LANES], qt_ref[h])

    def softmax(h, u, mask):
        st = s_ref[u % n_s]
        if mask is not None:
            st = jnp.where(mask, st, MASK_NEG)
        m_old = m_ref[h]
        m_new = jnp.maximum(m_old, jnp.max(st, axis=0, keepdims=True))
        alpha_ref[h] = jnp.exp2(m_old - m_new)
        p_ref[u % n_p] = jnp.exp2((st - m_new).astype(BF16))
        m_ref[h] = m_new

    def weighted_values(j, h, u):
        acc_ref[h] = alpha_ref[h] * acc_ref[h] + _dot(vt_ref[j, h], p_ref[u % n_p])

    def kv_blocks(js, mask):
        units = [(j, h) for j in js for h in range(N_HEADS)]
        for step in range(len(units) + VALUES_LAG):
            if step < len(units):
                scores(*units[step], step)
            u = step - SOFTMAX_LAG
            if 0 <= u < len(units):
                softmax(units[u][1], u, mask)
            u = step - VALUES_LAG
            if 0 <= u < len(units):
                weighted_values(*units[u], u)

    def quad(g, _):
        kv_blocks([g * 4 + d for d in range(4)], None)
        return 0

    n_quads = i // 4
    lax.fori_loop(0, n_quads, quad, 0)

    @pl.when(i % 4 >= 2)
    def _():
        kv_blocks([n_quads * 4, n_quads * 4 + 1], None)

    @pl.when(i % 2 == 1)
    def _():
        kv_blocks([i - 1], None)

    key_pos = lax.broadcasted_iota(jnp.int32, (blk, blk), 0)
    qry_pos = lax.broadcasted_iota(jnp.int32, (blk, blk), 1)
    kv_blocks([i], key_pos <= qry_pos)
    outs = []
    for h in range(N_HEADS):
        acc = acc_ref[h]
        outs.append(acc[0:HEAD_DIM, :] / acc[HEAD_DIM:HEAD_DIM + 1, :])
    o_ref[...] = jnp.concatenate(outs, axis=0).T.astype(BF16)


def _moba(q, k, v, batch):
    n = q.shape[0]
    seq = n // batch
    tq = MOBA_BLOCK
    nblk = seq // MOBA_BLOCK
    return pl.pallas_call(
        _moba_kernel,
        grid=(batch, nblk),
        in_specs=[pl.BlockSpec((tq, ATTN_WIDTH), lambda b, i: (b * nblk + i, 0)),
                  pl.BlockSpec((seq, ATTN_WIDTH), lambda b, i: (b, 0)),
                  pl.BlockSpec((seq, ATTN_WIDTH), lambda b, i: (b, 0))],
        out_specs=pl.BlockSpec((tq, ATTN_WIDTH), lambda b, i: (b * nblk + i, 0)),
        out_shape=jax.ShapeDtypeStruct((n, ATTN_WIDTH), BF16),
        scratch_shapes=[pltpu.VMEM((nblk, MOBA_BLOCK, N_HEADS * LANES), BF16),
                        pltpu.VMEM((nblk, N_HEADS, V_ROWS, MOBA_BLOCK), BF16),
                        pltpu.VMEM((N_HEADS, GATE_ROWS, LANES), F32),
                        pltpu.VMEM((N_HEADS, LANES, MOBA_BLOCK), BF16),
                        pltpu.VMEM((N_HEADS, 1, MOBA_BLOCK), F32),
                        pltpu.VMEM((N_HEADS, 1, MOBA_BLOCK), F32),
                        pltpu.VMEM((N_HEADS, V_ROWS, MOBA_BLOCK), F32),
                        pltpu.VMEM((VALUES_LAG, MOBA_BLOCK, MOBA_BLOCK), F32),
                        pltpu.VMEM((VALUES_LAG - SOFTMAX_LAG + 1, MOBA_BLOCK, MOBA_BLOCK), BF16)],
        compiler_params=_params("parallel", "arbitrary"),
        name="moba_attention",
    )(q, k, v)


def _mix_kernel(z_ref, a_ref, g_ref, x_ref, wglu_ref, wattn_ref, wout_ref, g2_ref,
                wrh_ref, wrl_ref, rb_ref, xo_ref, lg_ref):
    glu = _dot(z_ref[...], wglu_ref[...])
    y_ssm = glu[:, :D_MODEL] * jax.nn.sigmoid(glu[:, D_MODEL:])
    y_attn = _dot(a_ref[...], wattn_ref[...])
    g = g_ref[...].astype(F32)
    mixed = jax.nn.sigmoid(g[:, :D_MODEL]) * y_ssm + jax.nn.sigmoid(g[:, D_MODEL:]) * y_attn
    x = x_ref[...] + _dot(mixed.astype(BF16), wout_ref[...])
    xo_ref[...] = x
    h2 = _rms(x, g2_ref[...])
    hi = h2.astype(BF16)
    lo = (h2 - hi.astype(F32)).astype(BF16)
    lg_ref[...] = (_dot(hi, wrh_ref[...]) + _dot(lo, wrh_ref[...]) + _dot(hi, wrl_ref[...])
                   + rb_ref[...])


def _mix(z, a, g, x, wglu, wattn, wout, g2, wrh, wrl, rb):
    n = x.shape[0]
    tm = TOKEN_TILE
    row = lambda i: (i, 0)
    fix = lambda i: (0, 0)
    full = lambda t: pl.BlockSpec(t.shape, fix)
    return pl.pallas_call(
        _mix_kernel,
        grid=(n // tm,),
        in_specs=[pl.BlockSpec((tm, SSM_WIDTH), row), pl.BlockSpec((tm, ATTN_WIDTH), row),
                  pl.BlockSpec((tm, 2 * D_MODEL), row), pl.BlockSpec((tm, D_MODEL), row),
                  full(wglu), full(wattn), full(wout), full(g2), full(wrh), full(wrl), full(rb)],
        out_specs=[pl.BlockSpec((tm, D_MODEL), row), pl.BlockSpec((tm, ROUTER_LANES), row)],
        out_shape=[jax.ShapeDtypeStruct((n, D_MODEL), F32),
                   jax.ShapeDtypeStruct((n, ROUTER_LANES), F32)],
        compiler_params=_params("parallel"),
        name="mix_out_router",
    )(z, a, g, x, wglu, wattn, wout, g2, wrh, wrl, rb)


def _route_kernel(lg_ref, eid_ref, rank_ref, w_ref, cnt_ref, carry_ref):
    @pl.when(pl.program_id(0) == 0)
    def _():
        carry_ref[...] = jnp.zeros_like(carry_ref)

    lg = lg_ref[...]
    t = lg.shape[0]
    lane = lax.broadcasted_iota(jnp.int32, (t, LANES), 1).astype(F32)

    def first_lane(hit):
        return jnp.min(jnp.where(hit, lane, float(LANES)), axis=1, keepdims=True)

    g_log = jnp.where(lane < N_GROUPS, lg, -jnp.inf)
    g_max = jnp.max(g_log, axis=1, keepdims=True)
    g_top = first_lane(g_log == g_max)
    p_g = 1.0 / jnp.sum(jnp.exp(g_log - g_max), axis=1, keepdims=True)

    lo = N_GROUPS + g_top * EXPERTS_PER_GROUP
    in_group = (lane >= lo) & (lane < lo + EXPERTS_PER_GROUP)
    e_log = jnp.where(in_group, lg, -jnp.inf)
    e_exp = jnp.exp(e_log - jnp.max(e_log, axis=1, keepdims=True))
    prob = jnp.where(in_group, e_exp / jnp.sum(e_exp, axis=1, keepdims=True), -1.0)
    p1 = jnp.max(prob, axis=1, keepdims=True)
    l1 = first_lane(prob == p1)
    prob = jnp.where(lane == l1, -1.0, prob)
    p2 = jnp.max(prob, axis=1, keepdims=True)
    l2 = first_lane(prob == p2)
    w_ref[...] = jnp.concatenate([p_g * (p1 / (p1 + p2)), p_g * (p2 / (p1 + p2))], axis=1)
    eid = [l1 - N_GROUPS, l2 - N_GROUPS]
    eid_ref[...] = jnp.concatenate(eid, axis=1).astype(jnp.int32)

    earlier = (lax.broadcasted_iota(jnp.int32, (t, t), 1)
               < lax.broadcasted_iota(jnp.int32, (t, t), 0)).astype(BF16)
    carry = carry_ref[...]
    ranks = []
    for e in eid:
        hit = lane == e
        onehot = hit.astype(F32)
        before = _dot(earlier, onehot.astype(BF16)) + carry
        ranks.append(jnp.sum(jnp.where(hit, before, 0.0), axis=1, keepdims=True))
        carry = carry + jnp.sum(onehot, axis=0, keepdims=True)
    carry_ref[...] = carry
    rank_ref[...] = jnp.concatenate(ranks, axis=1).astype(jnp.int32)
    cnt_ref[...] = carry.astype(jnp.int32)


def _route_rank(logits):
    n = logits.shape[0]
    t = TOKEN_TILE
    pair = pl.BlockSpec((t, EXPERT_TOPK), lambda i: (i, 0))
    return pl.pallas_call(
        _route_kernel,
        grid=(n // t,),
        in_specs=[pl.BlockSpec((t, ROUTER_LANES), lambda i: (i, 0))],
        out_specs=[pair, pair, pair, pl.BlockSpec((1, LANES), lambda i: (0, 0))],
        out_shape=[jax.ShapeDtypeStruct((n, EXPERT_TOPK), jnp.int32),
                   jax.ShapeDtypeStruct((n, EXPERT_TOPK), jnp.int32),
                   jax.ShapeDtypeStruct((n, EXPERT_TOPK), F32),
                   jax.ShapeDtypeStruct((1, LANES), jnp.int32)],
        scratch_shapes=[pltpu.VMEM((1, LANES), F32)],
        compiler_params=_params("arbitrary"),
        name="route_rank",
    )(logits)


def _store_row_tiles(ref, value):
    rows = value.shape[0]
    for c in range(ROW_TILES):
        ref[pl.ds(c, rows, stride=ROW_TILES), :] = value[:, c * LANES:(c + 1) * LANES]


def _load_row_tiles(ref, start, rows):
    return jnp.concatenate(
        [ref[pl.ds(start * ROW_TILES + c, rows, stride=ROW_TILES), :] for c in range(ROW_TILES)], axis=1)


def _row_dma(src, src_row8, dst, dst_row8, sem, rows=1):
    n = rows * ROW_TILES
    return pltpu.make_async_copy(src.at[pl.ds(src_row8, n), :], dst.at[pl.ds(dst_row8, n), :], sem)


def _wait_rows(src, dst, sem, count, rows=1):
    def body(r, _):
        _row_dma(src, 0, dst, 0, sem, rows).wait()
        return 0
    lax.fori_loop(0, count, body, 0, unroll=8 if isinstance(count, int) else 1)


def _dispatch_kernel(dest_ref, pad0_ref, npad_ref, x_ref, g2_ref, xg_hbm, hbuf, zrow, sem, zsem):
    s = pl.program_id(0)
    last = pl.num_programs(0) - 1
    slot = s % 2
    tt = x_ref.shape[0]
    rows = EXPERT_TOPK * tt

    @pl.when(s == 0)
    def _():
        zrow[...] = jnp.zeros_like(zrow)

        def tile_of(row):
            return pl.multiple_of(row * ROW_TILES, ROW_TILES)

        def per_expert(e, total):
            def fill(r, _):
                _row_dma(zrow, 0, xg_hbm, tile_of(pad0_ref[e] + r), zsem).start()
                return 0
            lax.fori_loop(0, npad_ref[e], fill, 0)
            return total + npad_ref[e]
        total = lax.fori_loop(0, N_EXPERTS, per_expert, 0)
        _wait_rows(zrow, xg_hbm, zsem, total)

        def fill_unused(r, _):
            row = pad0_ref[N_EXPERTS] + r * SUBLANES
            _row_dma(zrow, 0, xg_hbm, tile_of(row), zsem, SUBLANES).start()
            return 0
        lax.fori_loop(0, npad_ref[N_EXPERTS], fill_unused, 0)
        _wait_rows(zrow, xg_hbm, zsem, npad_ref[N_EXPERTS], SUBLANES)

    src = hbuf.at[slot]
    _store_row_tiles(src, _rms(x_ref[...], g2_ref[...]))
    for r in range(rows):
        dst_row8 = pl.multiple_of(dest_ref[s * rows + r], ROW_TILES)
        _row_dma(src, (r % tt) * ROW_TILES, xg_hbm, dst_row8, sem.at[slot]).start(priority=r % 2)

    @pl.when(s > 0)
    def _():
        _wait_rows(hbuf.at[1 - slot], xg_hbm, sem.at[1 - slot], rows)

    @pl.when(s == last)
    def _():
        _wait_rows(src, xg_hbm, sem.at[slot], rows)


def _dispatch(dest, pad0, npad, x, g2, cap):
    n = x.shape[0]
    tt = COMBINE_TILE
    grid_spec = pltpu.PrefetchScalarGridSpec(
        num_scalar_prefetch=3,
        grid=(n // tt,),
        in_specs=[pl.BlockSpec((tt, D_MODEL), lambda s, d, p, c: (s, 0)),
                  pl.BlockSpec((1, D_MODEL), lambda s, d, p, c: (0, 0))],
        out_specs=pl.BlockSpec(memory_space=pl.ANY),
        scratch_shapes=[pltpu.VMEM((2, tt * ROW_TILES, LANES), F32),
                        pltpu.VMEM((SUBLANES * ROW_TILES, LANES), F32),
                        pltpu.SemaphoreType.DMA((2,)),
                        pltpu.SemaphoreType.DMA],
    )
    return pl.pallas_call(
        _dispatch_kernel,
        grid_spec=grid_spec,
        out_shape=jax.ShapeDtypeStruct((cap * ROW_TILES, LANES), F32),
        compiler_params=_params("arbitrary"),
        name="expert_dispatch",
    )(dest, pad0, npad, x, g2)


def _expert_kernel(blk_e_ref, nused_ref, xg_ref, wgu_ref, wd_ref, y_ref, wgu_bf, wd_bf):
    b = pl.program_id(0)
    prev = blk_e_ref[jnp.maximum(b, 1) - 1]

    @pl.when((b == 0) | (blk_e_ref[b] != prev))
    def _():
        wgu_bf[...] = wgu_ref[0, 0].astype(BF16)
        wd_bf[...] = wd_ref[0, 0].astype(BF16)

    @pl.when(b < nused_ref[0])
    def _():
        gu = _dot(_load_row_tiles(xg_ref, 0, EXPERT_ROWS).astype(BF16), wgu_bf[...])
        act = jax.nn.silu(gu[:, :D_EXPERT]) * gu[:, D_EXPERT:]
        _store_row_tiles(y_ref, _dot(act.astype(BF16), wd_bf[...]))

    @pl.when(b >= nused_ref[0])
    def _():
        y_ref[...] = jnp.zeros_like(y_ref)


def _experts(blk_e, nused, xg, w_gate_up, w_down, layer):
    rows = EXPERT_ROWS * ROW_TILES
    nblk = xg.shape[0] // rows
    used = lambda b, n: jnp.minimum(b, n[0] - 1)
    grid_spec = pltpu.PrefetchScalarGridSpec(
        num_scalar_prefetch=2,
        grid=(nblk,),
        in_specs=[pl.BlockSpec((rows, LANES), lambda b, e, n: (used(b, n), 0)),
                  pl.BlockSpec((1, 1, D_MODEL, 2 * D_EXPERT), lambda b, e, n: (layer, e[b], 0, 0)),
                  pl.BlockSpec((1, 1, D_EXPERT, D_MODEL), lambda b, e, n: (layer, e[b], 0, 0))],
        out_specs=pl.BlockSpec((rows, LANES), lambda b, e, n: (b, 0)),
        scratch_shapes=[pltpu.VMEM((D_MODEL, 2 * D_EXPERT), BF16),
                        pltpu.VMEM((D_EXPERT, D_MODEL), BF16)],
    )
    return pl.pallas_call(
        _expert_kernel,
        grid_spec=grid_spec,
        out_shape=jax.ShapeDtypeStruct((nblk * rows, LANES), F32),
        compiler_params=_params("arbitrary"),
        name="experts",
    )(blk_e, nused, xg, w_gate_up, w_down)


def _combine_kernel(pos_ref, x_ref, w_ref, y_hbm, o_ref, ybuf, sem):
    s = pl.program_id(0)
    nsteps = pl.num_programs(0)
    slot = s % 2
    tt = COMBINE_TILE
    rows = EXPERT_TOPK * tt

    def issue(step, slot):
        dst = ybuf.at[slot]
        for r in range(rows):
            src_row8 = pl.multiple_of(pos_ref[step * rows + r], ROW_TILES)
            _row_dma(y_hbm, src_row8, dst, r * ROW_TILES, sem.at[slot]).start(priority=r % 2)

    @pl.when(s == 0)
    def _():
        issue(0, 0)

    @pl.when(s + 1 < nsteps)
    def _():
        issue(s + 1, 1 - slot)

    _wait_rows(y_hbm, ybuf.at[slot], sem.at[slot], rows)
    w = w_ref[...]
    got = ybuf.at[slot]
    o_ref[...] = (x_ref[...] + w[:, 0:1] * _load_row_tiles(got, 0, tt)
                  + w[:, 1:2] * _load_row_tiles(got, tt, tt))


def _combine(pos, x, w, y):
    n = x.shape[0]
    tt = COMBINE_TILE
    grid_spec = pltpu.PrefetchScalarGridSpec(
        num_scalar_prefetch=1,
        grid=(n // tt,),
        in_specs=[pl.BlockSpec((tt, D_MODEL), lambda s, p: (s, 0)),
                  pl.BlockSpec((tt, EXPERT_TOPK), lambda s, p: (s, 0)),
                  pl.BlockSpec(memory_space=pl.ANY)],
        out_specs=pl.BlockSpec((tt, D_MODEL), lambda s, p: (s, 0)),
        scratch_shapes=[pltpu.VMEM((2, EXPERT_TOPK * tt * ROW_TILES, LANES), F32),
                        pltpu.SemaphoreType.DMA((2,))],
    )
    return pl.pallas_call(
        _combine_kernel,
        grid_spec=grid_spec,
        out_shape=jax.ShapeDtypeStruct((n, D_MODEL), F32),
        compiler_params=_params("arbitrary"),
        name="moe_combine",
    )(pos, x, w, y)


def _rope_tables(seq):
    inv = ROPE_THETA ** (-jnp.arange(0, HEAD_DIM, 2, dtype=F32) / HEAD_DIM)
    ang = jnp.arange(seq, dtype=F32)[:, None] * inv[None, :]
    cos, sin = jnp.cos(ang), jnp.sin(ang)
    cos = jnp.concatenate([cos, cos] * (LANES // HEAD_DIM), axis=1)
    sin = jnp.concatenate([-sin, sin] * (LANES // HEAD_DIM), axis=1)
    return cos, sin


def _ssm_weights(a_re, a_im, log_dt, b_re, b_im, c_re, c_im):
    dt = jnp.exp(log_dt)[:, None]
    decay = jnp.exp(a_re * dt)
    abar_re = decay * jnp.cos(a_im * dt)
    abar_im = decay * jnp.sin(a_im * dt)
    den = a_re * a_re + a_im * a_im
    num_re = abar_re - 1.0
    f_re = (num_re * a_re + abar_im * a_im) / den
    f_im = (abar_im * a_re - num_re * a_im) / den
    bbar_re = f_re[..., None] * b_re - f_im[..., None] * b_im
    bbar_im = f_re[..., None] * b_im + f_im[..., None] * b_re
    gh = SSM_GROUPS // 2
    eye = jnp.eye(gh, dtype=F32)

    def b_mat(t):
        t = t.reshape(2, gh, SSM_STATE, SSM_GROUP_CH)
        return jnp.einsum('cgph,gk->cghkp', t, eye).reshape(2, SSM_HALF, SSM_HALF_STATE).astype(BF16)

    def c_mat(t):
        t = t.reshape(2, gh, SSM_GROUP_CH, SSM_STATE)
        return jnp.einsum('cghp,gk->cgpkh', t, eye).reshape(2, SSM_HALF_STATE, SSM_HALF).astype(BF16)

    a_vec = lambda t: t.reshape(2, 1, SSM_HALF_STATE)
    return (b_mat(bbar_re), b_mat(bbar_im), c_mat(c_re), c_mat(-c_im), a_vec(abar_re), a_vec(abar_im))


def _route(logits, n):
    expert_id, rank, weights, cnt = _route_rank(logits)
    counts = cnt[0, :N_EXPERTS]
    padded = (counts + EXPERT_ROWS - 1) // EXPERT_ROWS * EXPERT_ROWS
    pend = jnp.cumsum(padded)
    pstart = pend - padded
    experts = jnp.arange(N_EXPERTS, dtype=jnp.int32)
    start_of = jnp.sum(jnp.where(expert_id[..., None] == experts, pstart, 0), axis=-1)
    dest = (start_of + rank).astype(jnp.int32)
    cap = n * EXPERT_TOPK + N_EXPERTS * EXPERT_ROWS
    nblk = cap // EXPERT_ROWS
    blk_row = jnp.arange(nblk, dtype=jnp.int32)[:, None] * EXPERT_ROWS
    blk_e = jnp.minimum(jnp.sum((pend[None, :] <= blk_row).astype(jnp.int32), axis=1), N_EXPERTS - 1)
    nused = (pend[-1:] // EXPERT_ROWS).astype(jnp.int32)
    pos = dest.reshape(n // COMBINE_TILE, COMBINE_TILE, EXPERT_TOPK).transpose(0, 2, 1).reshape(-1)
    pos = pos * ROW_TILES
    pad0 = jnp.concatenate([pstart + counts, pend[-1:]]).astype(jnp.int32)
    npad = jnp.concatenate([padded - counts, (cap - pend[-1:]) // SUBLANES]).astype(jnp.int32)
    return weights, pos, pad0, npad, blk_e, nused, cap


def kernel(x, norm1_g, w_in, ssm_a_re, ssm_a_im, ssm_log_dt, ssm_b_re, ssm_b_im, ssm_c_re, ssm_c_im,
           ssm_d, w_glu, q_norm_g, k_norm_g, w_attn, w_out, norm2_g, router_w_group, router_b_group,
           router_w_expert, router_b_expert, w_gate_up, w_down):
    batch, seq, _ = x.shape
    depth = w_in.shape[0]
    n = batch * seq
    assert batch == SUBLANES and seq % MOBA_BLOCK == 0 and n % TOKEN_TILE == 0
    assert seq // MOBA_BLOCK <= GATE_ROWS and seq % SSM_STEPS == 0

    xt = x.reshape(n, D_MODEL)
    cos, sin = _rope_tables(seq)
    time_major = lambda t: t.reshape(batch, seq, -1).transpose(1, 0, 2).reshape(n, -1)
    batch_major = lambda t: t.reshape(seq, batch, -1).transpose(1, 0, 2).reshape(n, -1)
    idx = jnp.arange(ATTN_WIDTH)
    gsum = (idx[:, None] // HEAD_DIM == idx[None, :] // HEAD_DIM).astype(BF16)
    row = lambda t: t.reshape(1, -1).astype(F32)

    for l in range(depth):
        qg = row(jnp.tile(q_norm_g[l], N_HEADS)) * (HEAD_DIM ** -0.5 * math.log2(math.e))
        kg = row(jnp.tile(k_norm_g[l], N_HEADS))
        u, q, k, v, g = _in_proj(xt, row(norm1_g[l]), w_in[l].astype(BF16), gsum, qg, kg, cos, sin)
        ssm_w = _ssm_weights(ssm_a_re[l], ssm_a_im[l], ssm_log_dt[l], ssm_b_re[l], ssm_b_im[l],
                             ssm_c_re[l], ssm_c_im[l])
        z = batch_major(_ssm(time_major(u), *ssm_w, row(ssm_d[l]), batch))
        attn = _moba(q, k, v, batch)

        w_r = jnp.concatenate([router_w_group[l], router_w_expert[l]], axis=1)
        w_r = jnp.pad(w_r, ((0, 0), (0, ROUTER_LANES - w_r.shape[1])))
        w_rh = w_r.astype(BF16)
        w_rl = (w_r - w_rh.astype(F32)).astype(BF16)
        b_r = jnp.concatenate([router_b_group[l], router_b_expert[l]])
        b_r = row(jnp.pad(b_r, (0, ROUTER_LANES - b_r.shape[0])))
        g2 = row(norm2_g[l])
        xt, logits = _mix(z, attn, g, xt, w_glu[l].astype(BF16), w_attn[l].astype(BF16),
                          w_out[l].astype(BF16), g2, w_rh, w_rl, b_r)

        weights, pos, pad0, npad, blk_e, nused, cap = _route(logits, n)
        xg = _dispatch(pos, pad0, npad, xt, g2, cap)
        y = _experts(blk_e, nused, xg, w_gate_up, w_down, l)
        xt = _combine(pos, xt, weights, y)

    return xt.reshape(batch, seq, D_MODEL)
```

```python
import functools
import math

import jax
import jax.numpy as jnp
from jax import lax
from jax.experimental import pallas as pl
from jax.experimental.pallas import tpu as pltpu

F32 = jnp.float32
BF16 = jnp.bfloat16

D_MODEL = 1024
SSM_GROUPS = 32
SSM_GROUP_CH = 16
SSM_WIDTH = SSM_GROUPS * SSM_GROUP_CH
SSM_STATE = 64
N_HEADS = 8
HEAD_DIM = 64
ATTN_WIDTH = N_HEADS * HEAD_DIM
MOBA_BLOCK = 256
MOBA_TOPK = 3
ROPE_THETA = 10000.0
D_IN = SSM_WIDTH + 3 * ATTN_WIDTH + 2 * D_MODEL
N_GROUPS = 4
EXPERTS_PER_GROUP = 8
N_EXPERTS = N_GROUPS * EXPERTS_PER_GROUP
EXPERT_TOPK = 2
D_EXPERT = 512
EXPERT_ROWS = 256
NORM_EPS = 1e-6

LANES = 128
SUBLANES = 8
VMEM_LIMIT = 56 * 1024 * 1024
MASK_NEG = -1e30

TOKEN_TILE = 512
SSM_STEPS = 64
SSM_HALF = SSM_WIDTH // 2
SSM_HALF_STATE = SSM_GROUPS * SSM_STATE // 2
GATE_ROWS = 16
V_ROWS = HEAD_DIM + 16
SOFTMAX_LAG = 3
VALUES_LAG = 6
ROUTER_LANES = 128
COMBINE_TILE = TOKEN_TILE
ROW_TILES = D_MODEL // LANES


def _params(*sem, flags=None):
    return pltpu.CompilerParams(dimension_semantics=sem, vmem_limit_bytes=VMEM_LIMIT, flags=flags)


def _dot(a, b):
    return jnp.dot(a, b, preferred_element_type=F32)


def _dot_nt(a, b):
    return lax.dot_general(a, b, (((1,), (1,)), ((), ())), preferred_element_type=F32)


def _rms(x, gain):
    return x * lax.rsqrt(jnp.mean(x * x, axis=-1, keepdims=True) + NORM_EPS) * gain


PROJ_SEGMENTS = 6


def _project(x, g1_ref, w_ref, gs_ref, qg_ref, kg_ref, cos_ref, sin_ref,
             u_ref, q_ref, k_ref, v_ref, g_ref, between=lambda k: None):
    hb = _rms(x, g1_ref[...]).astype(BF16)
    n_seg = iter(range(PROJ_SEGMENTS))

    def seg(lo, hi):
        between(next(n_seg))
        return _dot(hb, w_ref[:, lo:hi])

    o_q = SSM_WIDTH
    o_k = o_q + ATTN_WIDTH
    o_v = o_k + ATTN_WIDTH
    o_g = o_v + ATTN_WIDTH
    o_g2 = o_g + D_MODEL
    u_ref[...] = seg(0, o_q).astype(BF16)
    v_ref[...] = seg(o_v, o_g).astype(BF16)
    g_ref[:, :D_MODEL] = seg(o_g, o_g2).astype(BF16)
    g_ref[:, D_MODEL:] = seg(o_g2, D_IN).astype(BF16)

    reps = ATTN_WIDTH // LANES
    cos = jnp.concatenate([cos_ref[...]] * reps, axis=1)
    sin = jnp.concatenate([sin_ref[...]] * reps, axis=1)
    lane = lax.broadcasted_iota(jnp.int32, cos.shape, 1)
    first_half = (lane % HEAD_DIM) < (HEAD_DIM // 2)

    def norm_rope(t, gain):
        ss = _dot((t * t).astype(BF16), gs_ref[...])
        tn = t * lax.rsqrt(ss * (1.0 / HEAD_DIM) + NORM_EPS) * gain
        partner = jnp.where(first_half,
                            pltpu.roll(tn, ATTN_WIDTH - HEAD_DIM // 2, 1),
                            pltpu.roll(tn, HEAD_DIM // 2, 1))
        return tn * cos + partner * sin

    q_ref[...] = norm_rope(seg(o_q, o_k), qg_ref[...]).astype(BF16)
    k_ref[...] = norm_rope(seg(o_k, o_v), kg_ref[...]).astype(BF16)


def _in_proj_kernel(x_ref, *refs):
    _project(x_ref[...], *refs)


def _combine_in_proj_kernel(pos_ref, x_ref, w_ref, y_hbm, *refs):
    *proj_refs, xo_ref, ybuf, sem = refs
    s = pl.program_id(0)
    last = pl.num_programs(0) - 1
    slot = s % 2
    tt = x_ref.shape[0]
    rows = EXPERT_TOPK * tt

    def issue(step, slot, part=0, parts=1):
        dst = ybuf.at[slot]
        for r in range(part * rows // parts, (part + 1) * rows // parts):
            src_row8 = pl.multiple_of(pos_ref[step * rows + r], ROW_TILES)
            _row_dma(y_hbm, src_row8, dst, r * ROW_TILES, sem.at[slot]).start(priority=r % 2)

    @pl.when(s == 0)
    def _():
        issue(0, 0)

    _wait_rows(y_hbm, ybuf.at[slot], sem.at[slot], rows)
    w = w_ref[...]
    got = ybuf.at[slot]
    x = (x_ref[...] + w[:, 0:1] * _load_row_tiles(got, 0, tt)
         + w[:, 1:2] * _load_row_tiles(got, tt, tt))
    xo_ref[...] = x
    nxt = jnp.minimum(s + 1, last)
    _project(x, *proj_refs, between=lambda k: issue(nxt, 1 - slot, k, PROJ_SEGMENTS))

    @pl.when(s == last)
    def _():
        _wait_rows(y_hbm, ybuf.at[1 - slot], sem.at[1 - slot], rows)


def _in_proj(x, g1, w_in, gsum, qg, kg, cos, sin, moe=None):
    n = x.shape[0]
    tm = TOKEN_TILE
    row = lambda i, *_: (i, 0)
    fix = lambda i, *_: (0, 0)
    tiles_per_seq = cos.shape[0] // tm
    pos = lambda i, *_: (i % tiles_per_seq, 0)
    outs = [jax.ShapeDtypeStruct((n, w), BF16)
            for w in (SSM_WIDTH, ATTN_WIDTH, ATTN_WIDTH, ATTN_WIDTH, 2 * D_MODEL)]
    in_specs = [pl.BlockSpec((tm, D_MODEL), row),
                pl.BlockSpec((1, D_MODEL), fix),
                pl.BlockSpec((D_MODEL, D_IN), fix),
                pl.BlockSpec((ATTN_WIDTH, ATTN_WIDTH), fix),
                pl.BlockSpec((1, ATTN_WIDTH), fix),
                pl.BlockSpec((1, ATTN_WIDTH), fix),
                pl.BlockSpec((tm, LANES), pos),
                pl.BlockSpec((tm, LANES), pos)]
    out_specs = [pl.BlockSpec((tm, o.shape[1]), row) for o in outs]
    if moe is None:
        return pl.pallas_call(
            _in_proj_kernel,
            grid=(n // tm,),
            in_specs=in_specs,
            out_specs=out_specs,
            out_shape=outs,
            compiler_params=_params("parallel"),
            name="in_proj",
        )(x, g1, w_in, gsum, qg, kg, cos, sin)
    gather_pos, weights, y = moe
    grid_spec = pltpu.PrefetchScalarGridSpec(
        num_scalar_prefetch=1,
        grid=(n // tm,),
        in_specs=[in_specs[0], pl.BlockSpec((tm, EXPERT_TOPK), row), pl.BlockSpec(memory_space=pl.ANY)]
        + in_specs[1:],
        out_specs=out_specs + [pl.BlockSpec((tm, D_MODEL), row)],
        scratch_shapes=[pltpu.VMEM((2, EXPERT_TOPK * tm * ROW_TILES, LANES), F32),
                        pltpu.SemaphoreType.DMA((2,))],
    )
    return pl.pallas_call(
        _combine_in_proj_kernel,
        grid_spec=grid_spec,
        out_shape=outs + [jax.ShapeDtypeStruct((n, D_MODEL), F32)],
        compiler_params=_params("arbitrary"),
        name="combine_in_proj",
    )(gather_pos, x, weights, y, g1, w_in, gsum, qg, kg, cos, sin)


def _ssm_kernel(u_ref, bre_ref, bim_ref, cre_ref, cim_ref, are_ref, aim_ref, d_ref,
                z_ref, h_ref, sre_ref, sim_ref, *, batch):
    @pl.when(pl.program_id(0) == 0)
    def _():
        h_ref[...] = jnp.zeros_like(h_ref)

    steps = u_ref.shape[0] // batch
    ys = []
    for c in range(2):
        uc = u_ref[:, c * SSM_HALF:(c + 1) * SSM_HALF]
        sre_ref[...] = _dot(uc, bre_ref[c])
        sim_ref[...] = _dot(uc, bim_ref[c])
        a_re = jnp.broadcast_to(are_ref[c], (batch, SSM_HALF_STATE))
        a_im = jnp.broadcast_to(aim_ref[c], (batch, SSM_HALF_STATE))

        def step(t, carry):
            h_re, h_im = carry
            r0 = pl.multiple_of(t * batch, batch)
            n_re = a_re * h_re - a_im * h_im + sre_ref[pl.ds(r0, batch), :]
            n_im = a_re * h_im + a_im * h_re + sim_ref[pl.ds(r0, batch), :]
            sre_ref[pl.ds(r0, batch), :] = n_re
            sim_ref[pl.ds(r0, batch), :] = n_im
            return n_re, n_im

        h_re, h_im = lax.fori_loop(0, steps, step, (h_ref[c, 0], h_ref[c, 1]), unroll=True)
        h_ref[c, 0] = h_re
        h_ref[c, 1] = h_im
        ys.append(_dot(sre_ref[...].astype(BF16), cre_ref[c])
                  + _dot(sim_ref[...].astype(BF16), cim_ref[c]))
    y = jnp.concatenate(ys, axis=1) + d_ref[...] * u_ref[...].astype(F32)
    z_ref[...] = jax.nn.gelu(y).astype(BF16)


def _ssm(u, bre, bim, cre, cim, are, aim, d, batch):
    n = u.shape[0]
    rows = SSM_STEPS * batch
    fix3 = lambda i: (0, 0, 0)
    return pl.pallas_call(
        functools.partial(_ssm_kernel, batch=batch),
        grid=(n // rows,),
        in_specs=[pl.BlockSpec((rows, SSM_WIDTH), lambda i: (i, 0)),
                  pl.BlockSpec(bre.shape, fix3), pl.BlockSpec(bim.shape, fix3),
                  pl.BlockSpec(cre.shape, fix3), pl.BlockSpec(cim.shape, fix3),
                  pl.BlockSpec(are.shape, fix3), pl.BlockSpec(aim.shape, fix3),
                  pl.BlockSpec((1, SSM_WIDTH), lambda i: (0, 0))],
        out_specs=pl.BlockSpec((rows, SSM_WIDTH), lambda i: (i, 0)),
        out_shape=jax.ShapeDtypeStruct((n, SSM_WIDTH), BF16),
        scratch_shapes=[pltpu.VMEM((2, 2, batch, SSM_HALF_STATE), F32),
                        pltpu.VMEM((rows, SSM_HALF_STATE), F32),
                        pltpu.VMEM((rows, SSM_HALF_STATE), F32)],
        compiler_params=_params("arbitrary"),
        name="s5_scan",
    )(u, bre, bim, cre, cim, are, aim, d)


def _moba_kernel(q_ref, k_ref, v_ref, o_ref, kpad_ref, vt_ref, km_ref, qt_ref, m_ref, alpha_ref, acc_ref,
                 s_ref, p_ref):
    i = pl.program_id(1)
    blk = MOBA_BLOCK
    nblk = k_ref.shape[0] // blk
    n_sel = min(MOBA_TOPK, nblk - 1)

    @pl.when(i == 0)
    def _build():
        lane = lax.broadcasted_iota(jnp.int32, (blk, LANES), 1)
        is_head_lane = lane < HEAD_DIM
        km_ref[...] = jnp.zeros_like(km_ref)
        ones_row = (lax.broadcasted_iota(jnp.int32, (V_ROWS - HEAD_DIM, blk), 0) == 0)

        def per_block(j, _):
            r0 = pl.multiple_of(j * blk, blk)
            vt = v_ref[pl.ds(r0, blk), :].astype(F32).T
            for h in range(N_HEADS):
                p = h // 2
                kk = k_ref[pl.ds(r0, blk), p * LANES:(p + 1) * LANES].astype(F32)
                if h % 2:
                    kk = pltpu.roll(kk, HEAD_DIM, 1)
                kk = jnp.where(is_head_lane, kk, 0.0)
                km_ref[h, pl.ds(j, 1), :] = jnp.mean(kk, axis=0, keepdims=True)
                kk = jnp.where(lane == HEAD_DIM + j, 1.0, kk)
                kpad_ref[j, :, h * LANES:(h + 1) * LANES] = kk.astype(BF16)
                vt_ref[j, h, 0:HEAD_DIM, :] = vt[h * HEAD_DIM:(h + 1) * HEAD_DIM, :].astype(BF16)
                vt_ref[j, h, HEAD_DIM:V_ROWS, :] = ones_row.astype(BF16)
            return 0

        lax.fori_loop(0, nblk, per_block, 0)

    qt = q_ref[...].astype(F32).T
    blk_row = lax.broadcasted_iota(jnp.int32, (GATE_ROWS, blk), 0)
    past = blk_row < i
    blk_row_f = blk_row.astype(F32)
    zeros_q = jnp.zeros((LANES - HEAD_DIM, blk), F32)
    zeros_pad = jnp.zeros((LANES - HEAD_DIM - GATE_ROWS, blk), F32)
    for h in range(N_HEADS):
        qh = qt[h * HEAD_DIM:(h + 1) * HEAD_DIM, :]
        q_pad = jnp.concatenate([qh, zeros_q], axis=0).astype(BF16)
        km = km_ref[h]
        km_hi = km.astype(BF16)
        km_lo = (km - km_hi.astype(F32)).astype(BF16)
        gate = jnp.where(past, _dot(km_hi, q_pad) + _dot(km_lo, q_pad), -jnp.inf)
        sel = jnp.zeros(gate.shape, jnp.bool_)
        for _ in range(n_sel):
            best = jnp.max(gate, axis=0, keepdims=True)
            cand = (gate == best) & (best > -jnp.inf)
            first = jnp.min(jnp.where(cand, blk_row_f, float(GATE_ROWS)), axis=0, keepdims=True)
            pick = blk_row_f == first
            sel = sel | pick
            gate = jnp.where(pick, -jnp.inf, gate)
        bias = jnp.where(past & ~sel, MASK_NEG, 0.0)
        qt_ref[h] = jnp.concatenate([qh, bias, zeros_pad], axis=0).astype(BF16)
        m_ref[h] = jnp.full((1, blk), -jnp.inf, F32)
        acc_ref[h] = jnp.zeros((V_ROWS, blk), F32)

    n_s, n_p = s_ref.shape[0], p_ref.shape[0]

    def scores(j, h, u):
        s_ref[u % n_s] = _dot(kpad_ref[j, :, h * LANES:(h + 1) * LANES], qt_ref[h])

    def softmax(h, u, mask):
        st = s_ref[u % n_s]
        if mask is not None:
            st = jnp.where(mask, st, MASK_NEG)
        m_old = m_ref[h]
        m_new = jnp.maximum(m_old, jnp.max(st, axis=0, keepdims=True))
        alpha_ref[h] = jnp.exp2(m_old - m_new)
        p_ref[u % n_p] = jnp.exp2((st - m_new).astype(BF16))
        m_ref[h] = m_new

    def weighted_values(j, h, u):
        acc_ref[h] = alpha_ref[h] * acc_ref[h] + _dot(vt_ref[j, h], p_ref[u % n_p])

    def kv_blocks(js, mask):
        units = [(j, h) for j in js for h in range(N_HEADS)]
        for step in range(len(units) + VALUES_LAG):
            if step < len(units):
                scores(*units[step], step)
            u = step - SOFTMAX_LAG
            if 0 <= u < len(units):
                softmax(units[u][1], u, mask)
            u = step - VALUES_LAG
            if 0 <= u < len(units):
                weighted_values(*units[u], u)

    def quad(g, _):
        kv_blocks([g * 4 + d for d in range(4)], None)
        return 0

    n_quads = i // 4
    lax.fori_loop(0, n_quads, quad, 0)

    @pl.when(i % 4 >= 2)
    def _():
        kv_blocks([n_quads * 4, n_quads * 4 + 1], None)

    @pl.when(i % 2 == 1)
    def _():
        kv_blocks([i - 1], None)

    key_pos = lax.broadcasted_iota(jnp.int32, (blk, blk), 0)
    qry_pos = lax.broadcasted_iota(jnp.int32, (blk, blk), 1)
    kv_blocks([i], key_pos <= qry_pos)
    outs = []
    for h in range(N_HEADS):
        acc = acc_ref[h]
        outs.append(acc[0:HEAD_DIM, :] / acc[HEAD_DIM:HEAD_DIM + 1, :])
    o_ref[...] = jnp.concatenate(outs, axis=0).T.astype(BF16)


def _moba(q, k, v, batch):
    n = q.shape[0]
    seq = n // batch
    tq = MOBA_BLOCK
    nblk = seq // MOBA_BLOCK
    return pl.pallas_call(
        _moba_kernel,
        grid=(batch, nblk),
        in_specs=[pl.BlockSpec((tq, ATTN_WIDTH), lambda b, i: (b * nblk + i, 0)),
                  pl.BlockSpec((seq, ATTN_WIDTH), lambda b, i: (b, 0)),
                  pl.BlockSpec((seq, ATTN_WIDTH), lambda b, i: (b, 0))],
        out_specs=pl.BlockSpec((tq, ATTN_WIDTH), lambda b, i: (b * nblk + i, 0)),
        out_shape=jax.ShapeDtypeStruct((n, ATTN_WIDTH), BF16),
        scratch_shapes=[pltpu.VMEM((nblk, MOBA_BLOCK, N_HEADS * LANES), BF16),
                        pltpu.VMEM((nblk, N_HEADS, V_ROWS, MOBA_BLOCK), BF16),
                        pltpu.VMEM((N_HEADS, GATE_ROWS, LANES), F32),
                        pltpu.VMEM((N_HEADS, LANES, MOBA_BLOCK), BF16),
                        pltpu.VMEM((N_HEADS, 1, MOBA_BLOCK), F32),
                        pltpu.VMEM((N_HEADS, 1, MOBA_BLOCK), F32),
                        pltpu.VMEM((N_HEADS, V_ROWS, MOBA_BLOCK), F32),
                        pltpu.VMEM((VALUES_LAG, MOBA_BLOCK, MOBA_BLOCK), F32),
                        pltpu.VMEM((VALUES_LAG - SOFTMAX_LAG + 1, MOBA_BLOCK, MOBA_BLOCK), BF16)],
        compiler_params=_params("parallel", "arbitrary"),
        name="moba_attention",
    )(q, k, v)


def _mix_kernel(z_ref, a_ref, g_ref, x_ref, wglu_ref, wattn_ref, wout_ref, g2_ref,
                wrh_ref, wrl_ref, rb_ref, xo_ref, lg_ref):
    glu = _dot(z_ref[...], wglu_ref[...])
    y_ssm = glu[:, :D_MODEL] * jax.nn.sigmoid(glu[:, D_MODEL:])
    y_attn = _dot(a_ref[...], wattn_ref[...])
    g = g_ref[...].astype(F32)
    mixed = jax.nn.sigmoid(g[:, :D_MODEL]) * y_ssm + jax.nn.sigmoid(g[:, D_MODEL:]) * y_attn
    x = x_ref[...] + _dot(mixed.astype(BF16), wout_ref[...])
    xo_ref[...] = x
    h2 = _rms(x, g2_ref[...])
    hi = h2.astype(BF16)
    lo = (h2 - hi.astype(F32)).astype(BF16)
    lg_ref[...] = (_dot(hi, wrh_ref[...]) + _dot(lo, wrh_ref[...]) + _dot(hi, wrl_ref[...])
                   + rb_ref[...])


def _mix(z, a, g, x, wglu, wattn, wout, g2, wrh, wrl, rb):
    n = x.shape[0]
    tm = TOKEN_TILE
    row = lambda i: (i, 0)
    fix = lambda i: (0, 0)
    full = lambda t: pl.BlockSpec(t.shape, fix)
    return pl.pallas_call(
        _mix_kernel,
        grid=(n // tm,),
        in_specs=[pl.BlockSpec((tm, SSM_WIDTH), row), pl.BlockSpec((tm, ATTN_WIDTH), row),
                  pl.BlockSpec((tm, 2 * D_MODEL), row), pl.BlockSpec((tm, D_MODEL), row),
                  full(wglu), full(wattn), full(wout), full(g2), full(wrh), full(wrl), full(rb)],
        out_specs=[pl.BlockSpec((tm, D_MODEL), row), pl.BlockSpec((tm, ROUTER_LANES), row)],
        out_shape=[jax.ShapeDtypeStruct((n, D_MODEL), F32),
                   jax.ShapeDtypeStruct((n, ROUTER_LANES), F32)],
        compiler_params=_params("parallel"),
        name="mix_out_router",
    )(z, a, g, x, wglu, wattn, wout, g2, wrh, wrl, rb)


def _route_kernel(lg_ref, eid_ref, rank_ref, w_ref, cnt_ref, carry_ref):
    @pl.when(pl.program_id(0) == 0)
    def _():
        carry_ref[...] = jnp.zeros_like(carry_ref)

    lg = lg_ref[...]
    t = lg.shape[0]
    lane = lax.broadcasted_iota(jnp.int32, (t, LANES), 1).astype(F32)

    def first_lane(hit):
        return jnp.min(jnp.where(hit, lane, float(LANES)), axis=1, keepdims=True)

    g_log = jnp.where(lane < N_GROUPS, lg, -jnp.inf)
    g_max = jnp.max(g_log, axis=1, keepdims=True)
    g_top = first_lane(g_log == g_max)
    p_g = 1.0 / jnp.sum(jnp.exp(g_log - g_max), axis=1, keepdims=True)

    lo = N_GROUPS + g_top * EXPERTS_PER_GROUP
    in_group = (lane >= lo) & (lane < lo + EXPERTS_PER_GROUP)
    e_log = jnp.where(in_group, lg, -jnp.inf)
    e_exp = jnp.exp(e_log - jnp.max(e_log, axis=1, keepdims=True))
    prob = jnp.where(in_group, e_exp / jnp.sum(e_exp, axis=1, keepdims=True), -1.0)
    p1 = jnp.max(prob, axis=1, keepdims=True)
    l1 = first_lane(prob == p1)
    prob = jnp.where(lane == l1, -1.0, prob)
    p2 = jnp.max(prob, axis=1, keepdims=True)
    l2 = first_lane(prob == p2)
    w_ref[...] = jnp.concatenate([p_g * (p1 / (p1 + p2)), p_g * (p2 / (p1 + p2))], axis=1)
    eid = [l1 - N_GROUPS, l2 - N_GROUPS]
    eid_ref[...] = jnp.concatenate(eid, axis=1).astype(jnp.int32)

    earlier = (lax.broadcasted_iota(jnp.int32, (t, t), 1)
               < lax.broadcasted_iota(jnp.int32, (t, t), 0)).astype(BF16)
    carry = carry_ref[...]
    ranks = []
    for e in eid:
        hit = lane == e
        onehot = hit.astype(F32)
        before = _dot(earlier, onehot.astype(BF16)) + carry
        ranks.append(jnp.sum(jnp.where(hit, before, 0.0), axis=1, keepdims=True))
        carry = carry + jnp.sum(onehot, axis=0, keepdims=True)
    carry_ref[...] = carry
    rank_ref[...] = jnp.concatenate(ranks, axis=1).astype(jnp.int32)
    cnt_ref[...] = carry.astype(jnp.int32)


def _route_rank(logits):
    n = logits.shape[0]
    t = TOKEN_TILE
    pair = pl.BlockSpec((t, EXPERT_TOPK), lambda i: (i, 0))
    return pl.pallas_call(
        _route_kernel,
        grid=(n // t,),
        in_specs=[pl.BlockSpec((t, ROUTER_LANES), lambda i: (i, 0))],
        out_specs=[pair, pair, pair, pl.BlockSpec((1, LANES), lambda i: (0, 0))],
        out_shape=[jax.ShapeDtypeStruct((n, EXPERT_TOPK), jnp.int32),
                   jax.ShapeDtypeStruct((n, EXPERT_TOPK), jnp.int32),
                   jax.ShapeDtypeStruct((n, EXPERT_TOPK), F32),
                   jax.ShapeDtypeStruct((1, LANES), jnp.int32)],
        scratch_shapes=[pltpu.VMEM((1, LANES), F32)],
        compiler_params=_params("arbitrary"),
        name="route_rank",
    )(logits)


def _store_row_tiles(ref, value):
    rows = value.shape[0]
    for c in range(ROW_TILES):
        ref[pl.ds(c, rows, stride=ROW_TILES), :] = value[:, c * LANES:(c + 1) * LANES]


def _load_row_tiles(ref, start, rows):
    return jnp.concatenate(
        [ref[pl.ds(start * ROW_TILES + c, rows, stride=ROW_TILES), :] for c in range(ROW_TILES)], axis=1)


def _row_dma(src, src_row8, dst, dst_row8, sem, rows=1):
    n = rows * ROW_TILES
    return pltpu.make_async_copy(src.at[pl.ds(src_row8, n), :], dst.at[pl.ds(dst_row8, n), :], sem)


def _wait_rows(src, dst, sem, count, rows=1):
    def body(r, _):
        _row_dma(src, 0, dst, 0, sem, rows).wait()
        return 0
    lax.fori_loop(0, count, body, 0, unroll=8 if isinstance(count, int) else 1)


def _dispatch_kernel(dest_ref, pad0_ref, npad_ref, x_ref, g2_ref, xg_hbm, hbuf, zrow, sem, zsem):
    s = pl.program_id(0)
    last = pl.num_programs(0) - 1
    slot = s % 2
    tt = x_ref.shape[0]
    rows = EXPERT_TOPK * tt

    @pl.when(s == 0)
    def _():
        zrow[...] = jnp.zeros_like(zrow)

        def tile_of(row):
            return pl.multiple_of(row * ROW_TILES, ROW_TILES)

        def per_expert(e, total):
            def fill(r, _):
                _row_dma(zrow, 0, xg_hbm, tile_of(pad0_ref[e] + r), zsem).start()
                return 0
            lax.fori_loop(0, npad_ref[e], fill, 0)
            return total + npad_ref[e]
        total = lax.fori_loop(0, N_EXPERTS, per_expert, 0)
        _wait_rows(zrow, xg_hbm, zsem, total)

        def fill_unused(r, _):
            row = pad0_ref[N_EXPERTS] + r * SUBLANES
            _row_dma(zrow, 0, xg_hbm, tile_of(row), zsem, SUBLANES).start()
            return 0
        lax.fori_loop(0, npad_ref[N_EXPERTS], fill_unused, 0)
        _wait_rows(zrow, xg_hbm, zsem, npad_ref[N_EXPERTS], SUBLANES)

    src = hbuf.at[slot]
    _store_row_tiles(src, _rms(x_ref[...], g2_ref[...]))
    for r in range(rows):
        dst_row8 = pl.multiple_of(dest_ref[s * rows + r], ROW_TILES)
        _row_dma(src, (r % tt) * ROW_TILES, xg_hbm, dst_row8, sem.at[slot]).start(priority=r % 2)

    @pl.when(s > 0)
    def _():
        _wait_rows(hbuf.at[1 - slot], xg_hbm, sem.at[1 - slot], rows)

    @pl.when(s == last)
    def _():
        _wait_rows(src, xg_hbm, sem.at[slot], rows)


def _dispatch(dest, pad0, npad, x, g2, cap):
    n = x.shape[0]
    tt = COMBINE_TILE
    grid_spec = pltpu.PrefetchScalarGridSpec(
        num_scalar_prefetch=3,
        grid=(n // tt,),
        in_specs=[pl.BlockSpec((tt, D_MODEL), lambda s, d, p, c: (s, 0)),
                  pl.BlockSpec((1, D_MODEL), lambda s, d, p, c: (0, 0))],
        out_specs=pl.BlockSpec(memory_space=pl.ANY),
        scratch_shapes=[pltpu.VMEM((2, tt * ROW_TILES, LANES), F32),
                        pltpu.VMEM((SUBLANES * ROW_TILES, LANES), F32),
                        pltpu.SemaphoreType.DMA((2,)),
                        pltpu.SemaphoreType.DMA],
    )
    return pl.pallas_call(
        _dispatch_kernel,
        grid_spec=grid_spec,
        out_shape=jax.ShapeDtypeStruct((cap * ROW_TILES, LANES), F32),
        compiler_params=_params("arbitrary"),
        name="expert_dispatch",
    )(dest, pad0, npad, x, g2)


def _expert_kernel(blk_e_ref, nused_ref, xg_ref, wgu_ref, wd_ref, y_ref, wgu_bf, wd_bf):
    b = pl.program_id(0)
    prev = blk_e_ref[jnp.maximum(b, 1) - 1]

    @pl.when((b == 0) | (blk_e_ref[b] != prev))
    def _():
        wgu_bf[...] = wgu_ref[0, 0].astype(BF16)
        wd_bf[...] = wd_ref[0, 0].astype(BF16)

    @pl.when(b < nused_ref[0])
    def _():
        gu = _dot(_load_row_tiles(xg_ref, 0, EXPERT_ROWS).astype(BF16), wgu_bf[...])
        act = jax.nn.silu(gu[:, :D_EXPERT]) * gu[:, D_EXPERT:]
        _store_row_tiles(y_ref, _dot(act.astype(BF16), wd_bf[...]))

    @pl.when(b >= nused_ref[0])
    def _():
        y_ref[...] = jnp.zeros_like(y_ref)


def _experts(blk_e, nused, xg, w_gate_up, w_down, layer):
    rows = EXPERT_ROWS * ROW_TILES
    nblk = xg.shape[0] // rows
    used = lambda b, n: jnp.minimum(b, n[0] - 1)
    grid_spec = pltpu.PrefetchScalarGridSpec(
        num_scalar_prefetch=2,
        grid=(nblk,),
        in_specs=[pl.BlockSpec((rows, LANES), lambda b, e, n: (used(b, n), 0)),
                  pl.BlockSpec((1, 1, D_MODEL, 2 * D_EXPERT), lambda b, e, n: (layer, e[b], 0, 0)),
                  pl.BlockSpec((1, 1, D_EXPERT, D_MODEL), lambda b, e, n: (layer, e[b], 0, 0))],
        out_specs=pl.BlockSpec((rows, LANES), lambda b, e, n: (b, 0)),
        scratch_shapes=[pltpu.VMEM((D_MODEL, 2 * D_EXPERT), BF16),
                        pltpu.VMEM((D_EXPERT, D_MODEL), BF16)],
    )
    return pl.pallas_call(
        _expert_kernel,
        grid_spec=grid_spec,
        out_shape=jax.ShapeDtypeStruct((nblk * rows, LANES), F32),
        compiler_params=_params("arbitrary"),
        name="experts",
    )(blk_e, nused, xg, w_gate_up, w_down)


def _combine_kernel(pos_ref, x_ref, w_ref, y_hbm, o_ref, ybuf, sem):
    s = pl.program_id(0)
    nsteps = pl.num_programs(0)
    slot = s % 2
    tt = COMBINE_TILE
    rows = EXPERT_TOPK * tt

    def issue(step, slot):
        dst = ybuf.at[slot]
        for r in range(rows):
            src_row8 = pl.multiple_of(pos_ref[step * rows + r], ROW_TILES)
            _row_dma(y_hbm, src_row8, dst, r * ROW_TILES, sem.at[slot]).start(priority=r % 2)

    @pl.when(s == 0)
    def _():
        issue(0, 0)

    @pl.when(s + 1 < nsteps)
    def _():
        issue(s + 1, 1 - slot)

    _wait_rows(y_hbm, ybuf.at[slot], sem.at[slot], rows)
    w = w_ref[...]
    got = ybuf.at[slot]
    o_ref[...] = (x_ref[...] + w[:, 0:1] * _load_row_tiles(got, 0, tt)
                  + w[:, 1:2] * _load_row_tiles(got, tt, tt))


def _combine(pos, x, w, y):
    n = x.shape[0]
    tt = COMBINE_TILE
    grid_spec = pltpu.PrefetchScalarGridSpec(
        num_scalar_prefetch=1,
        grid=(n // tt,),
        in_specs=[pl.BlockSpec((tt, D_MODEL), lambda s, p: (s, 0)),
                  pl.BlockSpec((tt, EXPERT_TOPK), lambda s, p: (s, 0)),
                  pl.BlockSpec(memory_space=pl.ANY)],
        out_specs=pl.BlockSpec((tt, D_MODEL), lambda s, p: (s, 0)),
        scratch_shapes=[pltpu.VMEM((2, EXPERT_TOPK * tt * ROW_TILES, LANES), F32),
                        pltpu.SemaphoreType.DMA((2,))],
    )
    return pl.pallas_call(
        _combine_kernel,
        grid_spec=grid_spec,
        out_shape=jax.ShapeDtypeStruct((n, D_MODEL), F32),
        compiler_params=_params("arbitrary"),
        name="moe_combine",
    )(pos, x, w, y)


def _rope_tables(seq):
    inv = ROPE_THETA ** (-jnp.arange(0, HEAD_DIM, 2, dtype=F32) / HEAD_DIM)
    ang = jnp.arange(seq, dtype=F32)[:, None] * inv[None, :]
    cos, sin = jnp.cos(ang), jnp.sin(ang)
    cos = jnp.concatenate([cos, cos] * (LANES // HEAD_DIM), axis=1)
    sin = jnp.concatenate([-sin, sin] * (LANES // HEAD_DIM), axis=1)
    return cos, sin


def _ssm_weights(a_re, a_im, log_dt, b_re, b_im, c_re, c_im):
    dt = jnp.exp(log_dt)[:, None]
    decay = jnp.exp(a_re * dt)
    abar_re = decay * jnp.cos(a_im * dt)
    abar_im = decay * jnp.sin(a_im * dt)
    den = a_re * a_re + a_im * a_im
    num_re = abar_re - 1.0
    f_re = (num_re * a_re + abar_im * a_im) / den
    f_im = (abar_im * a_re - num_re * a_im) / den
    bbar_re = f_re[..., None] * b_re - f_im[..., None] * b_im
    bbar_im = f_re[..., None] * b_im + f_im[..., None] * b_re
    gh = SSM_GROUPS // 2
    eye = jnp.eye(gh, dtype=F32)

    def b_mat(t):
        t = t.reshape(2, gh, SSM_STATE, SSM_GROUP_CH)
        return jnp.einsum('cgph,gk->cghkp', t, eye).reshape(2, SSM_HALF, SSM_HALF_STATE).astype(BF16)

    def c_mat(t):
        t = t.reshape(2, gh, SSM_GROUP_CH, SSM_STATE)
        return jnp.einsum('cghp,gk->cgpkh', t, eye).reshape(2, SSM_HALF_STATE, SSM_HALF).astype(BF16)

    a_vec = lambda t: t.reshape(2, 1, SSM_HALF_STATE)
    return (b_mat(bbar_re), b_mat(bbar_im), c_mat(c_re), c_mat(-c_im), a_vec(abar_re), a_vec(abar_im))


def _route(logits, n):
    expert_id, rank, weights, cnt = _route_rank(logits)
    counts = cnt[0, :N_EXPERTS]
    padded = (counts + EXPERT_ROWS - 1) // EXPERT_ROWS * EXPERT_ROWS
    pend = jnp.cumsum(padded)
    pstart = pend - padded
    experts = jnp.arange(N_EXPERTS, dtype=jnp.int32)
    start_of = jnp.sum(jnp.where(expert_id[..., None] == experts, pstart, 0), axis=-1)
    dest = (start_of + rank).astype(jnp.int32)
    cap = n * EXPERT_TOPK + N_EXPERTS * EXPERT_ROWS
    nblk = cap // EXPERT_ROWS
    blk_row = jnp.arange(nblk, dtype=jnp.int32)[:, None] * EXPERT_ROWS
    blk_e = jnp.minimum(jnp.sum((pend[None, :] <= blk_row).astype(jnp.int32), axis=1), N_EXPERTS - 1)
    nused = (pend[-1:] // EXPERT_ROWS).astype(jnp.int32)
    pos = dest.reshape(n // COMBINE_TILE, COMBINE_TILE, EXPERT_TOPK).transpose(0, 2, 1).reshape(-1)
    pos = pos * ROW_TILES
    pad0 = jnp.concatenate([pstart + counts, pend[-1:]]).astype(jnp.int32)
    npad = jnp.concatenate([padded - counts, (cap - pend[-1:]) // SUBLANES]).astype(jnp.int32)
    return weights, pos, pad0, npad, blk_e, nused, cap


def kernel(x, norm1_g, w_in, ssm_a_re, ssm_a_im, ssm_log_dt, ssm_b_re, ssm_b_im, ssm_c_re, ssm_c_im,
           ssm_d, w_glu, q_norm_g, k_norm_g, w_attn, w_out, norm2_g, router_w_group, router_b_group,
           router_w_expert, router_b_expert, w_gate_up, w_down):
    batch, seq, _ = x.shape
    depth = w_in.shape[0]
    n = batch * seq
    assert batch == SUBLANES and seq % MOBA_BLOCK == 0 and n % TOKEN_TILE == 0
    assert seq // MOBA_BLOCK <= GATE_ROWS and seq % SSM_STEPS == 0

    xt = x.reshape(n, D_MODEL)
    cos, sin = _rope_tables(seq)
    time_major = lambda t: t.reshape(batch, seq, -1).transpose(1, 0, 2).reshape(n, -1)
    batch_major = lambda t: t.reshape(seq, batch, -1).transpose(1, 0, 2).reshape(n, -1)
    idx = jnp.arange(ATTN_WIDTH)
    gsum = (idx[:, None] // HEAD_DIM == idx[None, :] // HEAD_DIM).astype(BF16)
    row = lambda t: t.reshape(1, -1).astype(F32)

    moe = None
    for l in range(depth):
        qg = row(jnp.tile(q_norm_g[l], N_HEADS)) * (HEAD_DIM ** -0.5 * math.log2(math.e))
        kg = row(jnp.tile(k_norm_g[l], N_HEADS))
        proj = _in_proj(xt, row(norm1_g[l]), w_in[l].astype(BF16), gsum, qg, kg, cos, sin, moe)
        if moe is None:
            u, q, k, v, g = proj
        else:
            u, q, k, v, g, xt = proj
        ssm_w = _ssm_weights(ssm_a_re[l], ssm_a_im[l], ssm_log_dt[l], ssm_b_re[l], ssm_b_im[l],
                             ssm_c_re[l], ssm_c_im[l])
        z = batch_major(_ssm(time_major(u), *ssm_w, row(ssm_d[l]), batch))
        attn = _moba(q, k, v, batch)

        w_r = jnp.concatenate([router_w_group[l], router_w_expert[l]], axis=1)
        w_r = jnp.pad(w_r, ((0, 0), (0, ROUTER_LANES - w_r.shape[1])))
        w_rh = w_r.astype(BF16)
        w_rl = (w_r - w_rh.astype(F32)).astype(BF16)
        b_r = jnp.concatenate([router_b_group[l], router_b_expert[l]])
        b_r = row(jnp.pad(b_r, (0, ROUTER_LANES - b_r.shape[0])))
        g2 = row(norm2_g[l])
        xt, logits = _mix(z, attn, g, xt, w_glu[l].astype(BF16), w_attn[l].astype(BF16),
                          w_out[l].astype(BF16), g2, w_rh, w_rl, b_r)

        weights, pos, pad0, npad, blk_e, nused, cap = _route(logits, n)
        xg = _dispatch(pos, pad0, npad, xt, g2, cap)
        y = _experts(blk_e, nused, xg, w_gate_up, w_down, l)
        moe = (pos, weights, y)

    pos, weights, y = moe
    return _combine(pos, xt, weights, y).reshape(batch, seq, D_MODEL)
```

```python
import functools
import math

import jax
import jax.numpy as jnp
from jax import lax
from jax.experimental import pallas as pl
from jax.experimental.pallas import tpu as pltpu

F32 = jnp.float32
BF16 = jnp.bfloat16

D_MODEL = 1024
SSM_GROUPS = 32
SSM_GROUP_CH = 16
SSM_WIDTH = SSM_GROUPS * SSM_GROUP_CH
SSM_STATE = 64
N_HEADS = 8
HEAD_DIM = 64
ATTN_WIDTH = N_HEADS * HEAD_DIM
MOBA_BLOCK = 256
MOBA_TOPK = 3
ROPE_THETA = 10000.0
D_IN = SSM_WIDTH + 3 * ATTN_WIDTH + 2 * D_MODEL
N_GROUPS = 4
EXPERTS_PER_GROUP = 8
N_EXPERTS = N_GROUPS * EXPERTS_PER_GROUP
EXPERT_TOPK = 2
D_EXPERT = 512
EXPERT_ROWS = 256
NORM_EPS = 1e-6

LANES = 128
SUBLANES = 8
VMEM_LIMIT = 56 * 1024 * 1024
MASK_NEG = -1e30

TOKEN_TILE = 512
SSM_STEPS = 64
SSM_HALF = SSM_WIDTH // 2
SSM_HALF_STATE = SSM_GROUPS * SSM_STATE // 2
GATE_ROWS = 16
V_ROWS = HEAD_DIM + 16
SOFTMAX_LAG = 3
VALUES_LAG = 6
ROUTER_LANES = 128
COMBINE_TILE = 256
ROW_TILES = D_MODEL // LANES


def _params(*sem, flags=None):
    return pltpu.CompilerParams(dimension_semantics=sem, vmem_limit_bytes=VMEM_LIMIT, flags=flags)


def _dot(a, b):
    return jnp.dot(a, b, preferred_element_type=F32)


def _dot_nt(a, b):
    return lax.dot_general(a, b, (((1,), (1,)), ((), ())), preferred_element_type=F32)


def _rms(x, gain):
    return x * lax.rsqrt(jnp.mean(x * x, axis=-1, keepdims=True) + NORM_EPS) * gain


def _in_proj_kernel(x_ref, g1_ref, w_ref, gs_ref, qg_ref, kg_ref, cos_ref, sin_ref,
                    u_ref, q_ref, k_ref, v_ref, g_ref):
    hb = _rms(x_ref[...], g1_ref[...]).astype(BF16)

    def seg(lo, hi):
        return _dot(hb, w_ref[:, lo:hi])

    o_q = SSM_WIDTH
    o_k = o_q + ATTN_WIDTH
    o_v = o_k + ATTN_WIDTH
    o_g = o_v + ATTN_WIDTH
    u_ref[...] = seg(0, o_q).astype(BF16)
    v_ref[...] = seg(o_v, o_g).astype(BF16)
    g_ref[...] = seg(o_g, D_IN).astype(BF16)

    reps = ATTN_WIDTH // LANES
    cos = jnp.concatenate([cos_ref[...]] * reps, axis=1)
    sin = jnp.concatenate([sin_ref[...]] * reps, axis=1)
    lane = lax.broadcasted_iota(jnp.int32, cos.shape, 1)
    first_half = (lane % HEAD_DIM) < (HEAD_DIM // 2)

    def norm_rope(t, gain):
        ss = _dot((t * t).astype(BF16), gs_ref[...])
        tn = t * lax.rsqrt(ss * (1.0 / HEAD_DIM) + NORM_EPS) * gain
        partner = jnp.where(first_half,
                            pltpu.roll(tn, ATTN_WIDTH - HEAD_DIM // 2, 1),
                            pltpu.roll(tn, HEAD_DIM // 2, 1))
        return tn * cos + partner * sin

    q_ref[...] = norm_rope(seg(o_q, o_k), qg_ref[...]).astype(BF16)
    k_ref[...] = norm_rope(seg(o_k, o_v), kg_ref[...]).astype(BF16)


def _time_major_spec(tm, width, tiles_per_seq):
    return pl.BlockSpec((tm, width), lambda i: (i % tiles_per_seq, i // tiles_per_seq))


def _in_proj(x, g1, w_in, gsum, qg, kg, cos, sin):
    n = x.shape[0]
    tm = TOKEN_TILE
    row = lambda i: (i, 0)
    fix = lambda i: (0, 0)
    seq = cos.shape[0]
    tiles_per_seq = seq // tm
    pos = lambda i: (i % tiles_per_seq, 0)
    outs = [jax.ShapeDtypeStruct((seq, n // seq * SSM_WIDTH), BF16)]
    outs += [jax.ShapeDtypeStruct((n, w), BF16)
             for w in (ATTN_WIDTH, ATTN_WIDTH, ATTN_WIDTH, 2 * D_MODEL)]
    return pl.pallas_call(
        _in_proj_kernel,
        grid=(n // tm,),
        in_specs=[pl.BlockSpec((tm, D_MODEL), row),
                  pl.BlockSpec((1, D_MODEL), fix),
                  pl.BlockSpec((D_MODEL, D_IN), fix),
                  pl.BlockSpec((ATTN_WIDTH, ATTN_WIDTH), fix),
                  pl.BlockSpec((1, ATTN_WIDTH), fix),
                  pl.BlockSpec((1, ATTN_WIDTH), fix),
                  pl.BlockSpec((tm, LANES), pos),
                  pl.BlockSpec((tm, LANES), pos)],
        out_specs=[_time_major_spec(tm, SSM_WIDTH, tiles_per_seq)]
        + [pl.BlockSpec((tm, o.shape[1]), row) for o in outs[1:]],
        out_shape=outs,
        compiler_params=_params("parallel"),
        name="in_proj",
    )(x, g1, w_in, gsum, qg, kg, cos, sin)


def _ssm_kernel(u_ref, bre_ref, bim_ref, cre_ref, cim_ref, are_ref, aim_ref, d_ref,
                z_ref, h_ref, sre_ref, sim_ref, *, batch):
    @pl.when(pl.program_id(0) == 0)
    def _():
        h_ref[...] = jnp.zeros_like(h_ref)

    steps = u_ref.shape[0] // batch
    ys = []
    for c in range(2):
        uc = u_ref[:, c * SSM_HALF:(c + 1) * SSM_HALF]
        sre_ref[...] = _dot(uc, bre_ref[c])
        sim_ref[...] = _dot(uc, bim_ref[c])
        a_re = jnp.broadcast_to(are_ref[c], (batch, SSM_HALF_STATE))
        a_im = jnp.broadcast_to(aim_ref[c], (batch, SSM_HALF_STATE))

        def step(t, carry):
            h_re, h_im = carry
            r0 = pl.multiple_of(t * batch, batch)
            n_re = a_re * h_re - a_im * h_im + sre_ref[pl.ds(r0, batch), :]
            n_im = a_re * h_im + a_im * h_re + sim_ref[pl.ds(r0, batch), :]
            sre_ref[pl.ds(r0, batch), :] = n_re
            sim_ref[pl.ds(r0, batch), :] = n_im
            return n_re, n_im

        h_re, h_im = lax.fori_loop(0, steps, step, (h_ref[c, 0], h_ref[c, 1]), unroll=True)
        h_ref[c, 0] = h_re
        h_ref[c, 1] = h_im
        ys.append(_dot(sre_ref[...].astype(BF16), cre_ref[c])
                  + _dot(sim_ref[...].astype(BF16), cim_ref[c]))
    y = jnp.concatenate(ys, axis=1) + d_ref[...] * u_ref[...].astype(F32)
    z_ref[...] = jax.nn.gelu(y).astype(BF16)


def _ssm(u, bre, bim, cre, cim, are, aim, d, batch):
    n = u.shape[0]
    rows = SSM_STEPS * batch
    fix3 = lambda i: (0, 0, 0)
    return pl.pallas_call(
        functools.partial(_ssm_kernel, batch=batch),
        grid=(n // rows,),
        in_specs=[pl.BlockSpec((rows, SSM_WIDTH), lambda i: (i, 0)),
                  pl.BlockSpec(bre.shape, fix3), pl.BlockSpec(bim.shape, fix3),
                  pl.BlockSpec(cre.shape, fix3), pl.BlockSpec(cim.shape, fix3),
                  pl.BlockSpec(are.shape, fix3), pl.BlockSpec(aim.shape, fix3),
                  pl.BlockSpec((1, SSM_WIDTH), lambda i: (0, 0))],
        out_specs=pl.BlockSpec((rows, SSM_WIDTH), lambda i: (i, 0)),
        out_shape=jax.ShapeDtypeStruct((n, SSM_WIDTH), BF16),
        scratch_shapes=[pltpu.VMEM((2, 2, batch, SSM_HALF_STATE), F32),
                        pltpu.VMEM((rows, SSM_HALF_STATE), F32),
                        pltpu.VMEM((rows, SSM_HALF_STATE), F32)],
        compiler_params=_params("arbitrary"),
        name="s5_scan",
    )(u, bre, bim, cre, cim, are, aim, d)


def _moba_kernel(q_ref, k_ref, v_ref, o_ref, kpad_ref, vt_ref, km_ref, qt_ref, m_ref, alpha_ref, acc_ref,
                 s_ref, p_ref):
    i = pl.program_id(1)
    blk = MOBA_BLOCK
    nblk = k_ref.shape[0] // blk
    n_sel = min(MOBA_TOPK, nblk - 1)

    @pl.when(i == 0)
    def _build():
        lane = lax.broadcasted_iota(jnp.int32, (blk, LANES), 1)
        is_head_lane = lane < HEAD_DIM
        km_ref[...] = jnp.zeros_like(km_ref)
        ones_row = (lax.broadcasted_iota(jnp.int32, (V_ROWS - HEAD_DIM, blk), 0) == 0)

        def per_block(j, _):
            r0 = pl.multiple_of(j * blk, blk)
            vt = v_ref[pl.ds(r0, blk), :].astype(F32).T
            for h in range(N_HEADS):
                p = h // 2
                kk = k_ref[pl.ds(r0, blk), p * LANES:(p + 1) * LANES].astype(F32)
                if h % 2:
                    kk = pltpu.roll(kk, HEAD_DIM, 1)
                kk = jnp.where(is_head_lane, kk, 0.0)
                km_ref[h, pl.ds(j, 1), :] = jnp.mean(kk, axis=0, keepdims=True)
                kk = jnp.where(lane == HEAD_DIM + j, 1.0, kk)
                kpad_ref[j, :, h * LANES:(h + 1) * LANES] = kk.astype(BF16)
                vt_ref[j, h, 0:HEAD_DIM, :] = vt[h * HEAD_DIM:(h + 1) * HEAD_DIM, :].astype(BF16)
                vt_ref[j, h, HEAD_DIM:V_ROWS, :] = ones_row.astype(BF16)
            return 0

        lax.fori_loop(0, nblk, per_block, 0)

    qt = q_ref[...].astype(F32).T
    blk_row = lax.broadcasted_iota(jnp.int32, (GATE_ROWS, blk), 0)
    past = blk_row < i
    blk_row_f = blk_row.astype(F32)
    zeros_q = jnp.zeros((LANES - HEAD_DIM, blk), F32)
    zeros_pad = jnp.zeros((LANES - HEAD_DIM - GATE_ROWS, blk), F32)
    for h in range(N_HEADS):
        qh = qt[h * HEAD_DIM:(h + 1) * HEAD_DIM, :]
        q_pad = jnp.concatenate([qh, zeros_q], axis=0).astype(BF16)
        km = km_ref[h]
        km_hi = km.astype(BF16)
        km_lo = (km - km_hi.astype(F32)).astype(BF16)
        gate = jnp.where(past, _dot(km_hi, q_pad) + _dot(km_lo, q_pad), -jnp.inf)
        sel = jnp.zeros(gate.shape, jnp.bool_)
        for _ in range(n_sel):
            best = jnp.max(gate, axis=0, keepdims=True)
            cand = (gate == best) & (best > -jnp.inf)
            first = jnp.min(jnp.where(cand, blk_row_f, float(GATE_ROWS)), axis=0, keepdims=True)
            pick = blk_row_f == first
            sel = sel | pick
            gate = jnp.where(pick, -jnp.inf, gate)
        bias = jnp.where(past & ~sel, MASK_NEG, 0.0)
        qt_ref[h] = jnp.concatenate([qh, bias, zeros_pad], axis=0).astype(BF16)
        m_ref[h] = jnp.full((1, blk), -jnp.inf, F32)
        acc_ref[h] = jnp.zeros((V_ROWS, blk), F32)

    n_s, n_p = s_ref.shape[0], p_ref.shape[0]

    def scores(j, h, u):
        s_ref[u % n_s] = _dot(kpad_ref[j, :, h * LANES:(h + 1) * LANES], qt_ref[h])

    def softmax(h, u, mask):
        st = s_ref[u % n_s]
        if mask is not None:
            st = jnp.where(mask, st, MASK_NEG)
        m_old = m_ref[h]
        m_new = jnp.maximum(m_old, jnp.max(st, axis=0, keepdims=True))
        alpha_ref[h] = jnp.exp2(m_old - m_new)
        p_ref[u % n_p] = jnp.exp2((st - m_new).astype(BF16))
        m_ref[h] = m_new

    def weighted_values(j, h, u):
        acc_ref[h] = alpha_ref[h] * acc_ref[h] + _dot(vt_ref[j, h], p_ref[u % n_p])

    def kv_blocks(js, mask):
        units = [(j, h) for j in js for h in range(N_HEADS)]
        for step in range(len(units) + VALUES_LAG):
            if step < len(units):
                scores(*units[step], step)
            u = step - SOFTMAX_LAG
            if 0 <= u < len(units):
                softmax(units[u][1], u, mask)
            u = step - VALUES_LAG
            if 0 <= u < len(units):
                weighted_values(*units[u], u)

    def quad(g, _):
        kv_blocks([g * 4 + d for d in range(4)], None)
        return 0

    n_quads = i // 4
    lax.fori_loop(0, n_quads, quad, 0)

    @pl.when(i % 4 >= 2)
    def _():
        kv_blocks([n_quads * 4, n_quads * 4 + 1], None)

    @pl.when(i % 2 == 1)
    def _():
        kv_blocks([i - 1], None)

    key_pos = lax.broadcasted_iota(jnp.int32, (blk, blk), 0)
    qry_pos = lax.broadcasted_iota(jnp.int32, (blk, blk), 1)
    kv_blocks([i], key_pos <= qry_pos)
    outs = []
    for h in range(N_HEADS):
        acc = acc_ref[h]
        outs.append(acc[0:HEAD_DIM, :] / acc[HEAD_DIM:HEAD_DIM + 1, :])
    o_ref[...] = jnp.concatenate(outs, axis=0).T.astype(BF16)


def _moba(q, k, v, batch):
    n = q.shape[0]
    seq = n // batch
    tq = MOBA_BLOCK
    nblk = seq // MOBA_BLOCK
    return pl.pallas_call(
        _moba_kernel,
        grid=(batch, nblk),
        in_specs=[pl.BlockSpec((tq, ATTN_WIDTH), lambda b, i: (b * nblk + i, 0)),
                  pl.BlockSpec((seq, ATTN_WIDTH), lambda b, i: (b, 0)),
                  pl.BlockSpec((seq, ATTN_WIDTH), lambda b, i: (b, 0))],
        out_specs=pl.BlockSpec((tq, ATTN_WIDTH), lambda b, i: (b * nblk + i, 0)),
        out_shape=jax.ShapeDtypeStruct((n, ATTN_WIDTH), BF16),
        scratch_shapes=[pltpu.VMEM((nblk, MOBA_BLOCK, N_HEADS * LANES), BF16),
                        pltpu.VMEM((nblk, N_HEADS, V_ROWS, MOBA_BLOCK), BF16),
                        pltpu.VMEM((N_HEADS, GATE_ROWS, LANES), F32),
                        pltpu.VMEM((N_HEADS, LANES, MOBA_BLOCK), BF16),
                        pltpu.VMEM((N_HEADS, 1, MOBA_BLOCK), F32),
                        pltpu.VMEM((N_HEADS, 1, MOBA_BLOCK), F32),
                        pltpu.VMEM((N_HEADS, V_ROWS, MOBA_BLOCK), F32),
                        pltpu.VMEM((VALUES_LAG, MOBA_BLOCK, MOBA_BLOCK), F32),
                        pltpu.VMEM((VALUES_LAG - SOFTMAX_LAG + 1, MOBA_BLOCK, MOBA_BLOCK), BF16)],
        compiler_params=_params("parallel", "arbitrary"),
        name="moba_attention",
    )(q, k, v)


def _mix_kernel(z_ref, a_ref, g_ref, x_ref, wglu_ref, wattn_ref, wout_ref, g2_ref,
                wrh_ref, wrl_ref, rb_ref, xo_ref, lg_ref):
    glu = _dot(z_ref[...], wglu_ref[...])
    y_ssm = glu[:, :D_MODEL] * jax.nn.sigmoid(glu[:, D_MODEL:])
    y_attn = _dot(a_ref[...], wattn_ref[...])
    g = g_ref[...].astype(F32)
    mixed = jax.nn.sigmoid(g[:, :D_MODEL]) * y_ssm + jax.nn.sigmoid(g[:, D_MODEL:]) * y_attn
    x = x_ref[...] + _dot(mixed.astype(BF16), wout_ref[...])
    xo_ref[...] = x
    h2 = _rms(x, g2_ref[...])
    hi = h2.astype(BF16)
    lo = (h2 - hi.astype(F32)).astype(BF16)
    lg_ref[...] = (_dot(hi, wrh_ref[...]) + _dot(lo, wrh_ref[...]) + _dot(hi, wrl_ref[...])
                   + rb_ref[...])


def _mix(z, a, g, x, wglu, wattn, wout, g2, wrh, wrl, rb):
    n = x.shape[0]
    tm = TOKEN_TILE
    row = lambda i: (i, 0)
    fix = lambda i: (0, 0)
    full = lambda t: pl.BlockSpec(t.shape, fix)
    tiles_per_seq = z.shape[0] // tm
    return pl.pallas_call(
        _mix_kernel,
        grid=(n // tm,),
        in_specs=[_time_major_spec(tm, SSM_WIDTH, tiles_per_seq), pl.BlockSpec((tm, ATTN_WIDTH), row),
                  pl.BlockSpec((tm, 2 * D_MODEL), row), pl.BlockSpec((tm, D_MODEL), row),
                  full(wglu), full(wattn), full(wout), full(g2), full(wrh), full(wrl), full(rb)],
        out_specs=[pl.BlockSpec((tm, D_MODEL), row), pl.BlockSpec((tm, ROUTER_LANES), row)],
        out_shape=[jax.ShapeDtypeStruct((n, D_MODEL), F32),
                   jax.ShapeDtypeStruct((n, ROUTER_LANES), F32)],
        compiler_params=_params("parallel"),
        name="mix_out_router",
    )(z, a, g, x, wglu, wattn, wout, g2, wrh, wrl, rb)


def _route_kernel(lg_ref, eid_ref, rank_ref, w_ref, cnt_ref, carry_ref):
    @pl.when(pl.program_id(0) == 0)
    def _():
        carry_ref[...] = jnp.zeros_like(carry_ref)

    lg = lg_ref[...]
    t = lg.shape[0]
    lane = lax.broadcasted_iota(jnp.int32, (t, LANES), 1).astype(F32)

    def first_lane(hit):
        return jnp.min(jnp.where(hit, lane, float(LANES)), axis=1, keepdims=True)

    g_log = jnp.where(lane < N_GROUPS, lg, -jnp.inf)
    g_max = jnp.max(g_log, axis=1, keepdims=True)
    g_top = first_lane(g_log == g_max)
    p_g = 1.0 / jnp.sum(jnp.exp(g_log - g_max), axis=1, keepdims=True)

    lo = N_GROUPS + g_top * EXPERTS_PER_GROUP
    in_group = (lane >= lo) & (lane < lo + EXPERTS_PER_GROUP)
    e_log = jnp.where(in_group, lg, -jnp.inf)
    e_exp = jnp.exp(e_log - jnp.max(e_log, axis=1, keepdims=True))
    prob = jnp.where(in_group, e_exp / jnp.sum(e_exp, axis=1, keepdims=True), -1.0)
    p1 = jnp.max(prob, axis=1, keepdims=True)
    l1 = first_lane(prob == p1)
    prob = jnp.where(lane == l1, -1.0, prob)
    p2 = jnp.max(prob, axis=1, keepdims=True)
    l2 = first_lane(prob == p2)
    w_ref[...] = jnp.concatenate([p_g * (p1 / (p1 + p2)), p_g * (p2 / (p1 + p2))], axis=1)
    eid = [l1 - N_GROUPS, l2 - N_GROUPS]
    eid_ref[...] = jnp.concatenate(eid, axis=1).astype(jnp.int32)

    earlier = (lax.broadcasted_iota(jnp.int32, (t, t), 1)
               < lax.broadcasted_iota(jnp.int32, (t, t), 0)).astype(BF16)
    carry = carry_ref[...]
    ranks = []
    for e in eid:
        hit = lane == e
        onehot = hit.astype(F32)
        before = _dot(earlier, onehot.astype(BF16)) + carry
        ranks.append(jnp.sum(jnp.where(hit, before, 0.0), axis=1, keepdims=True))
        carry = carry + jnp.sum(onehot, axis=0, keepdims=True)
    carry_ref[...] = carry
    rank_ref[...] = jnp.concatenate(ranks, axis=1).astype(jnp.int32)
    cnt_ref[...] = carry.astype(jnp.int32)


def _route_rank(logits):
    n = logits.shape[0]
    t = TOKEN_TILE
    pair = pl.BlockSpec((t, EXPERT_TOPK), lambda i: (i, 0))
    return pl.pallas_call(
        _route_kernel,
        grid=(n // t,),
        in_specs=[pl.BlockSpec((t, ROUTER_LANES), lambda i: (i, 0))],
        out_specs=[pair, pair, pair, pl.BlockSpec((1, LANES), lambda i: (0, 0))],
        out_shape=[jax.ShapeDtypeStruct((n, EXPERT_TOPK), jnp.int32),
                   jax.ShapeDtypeStruct((n, EXPERT_TOPK), jnp.int32),
                   jax.ShapeDtypeStruct((n, EXPERT_TOPK), F32),
                   jax.ShapeDtypeStruct((1, LANES), jnp.int32)],
        scratch_shapes=[pltpu.VMEM((1, LANES), F32)],
        compiler_params=_params("arbitrary"),
        name="route_rank",
    )(logits)


def _store_row_tiles(ref, value):
    rows = value.shape[0]
    for c in range(ROW_TILES):
        ref[pl.ds(c, rows, stride=ROW_TILES), :] = value[:, c * LANES:(c + 1) * LANES]


def _load_row_tiles(ref, start, rows):
    return jnp.concatenate(
        [ref[pl.ds(start * ROW_TILES + c, rows, stride=ROW_TILES), :] for c in range(ROW_TILES)], axis=1)


def _row_dma(src, src_row8, dst, dst_row8, sem, rows=1):
    n = rows * ROW_TILES
    return pltpu.make_async_copy(src.at[pl.ds(src_row8, n), :], dst.at[pl.ds(dst_row8, n), :], sem)


def _wait_rows(src, dst, sem, count, rows=1):
    def body(r, _):
        _row_dma(src, 0, dst, 0, sem, rows).wait()
        return 0
    lax.fori_loop(0, count, body, 0, unroll=8 if isinstance(count, int) else 1)


def _dispatch_kernel(dest_ref, pad0_ref, npad_ref, x_ref, g2_ref, xg_hbm, hbuf, zrow, sem, zsem):
    s = pl.program_id(0)
    last = pl.num_programs(0) - 1
    slot = s % 2
    tt = x_ref.shape[0]
    rows = EXPERT_TOPK * tt

    @pl.when(s == 0)
    def _():
        zrow[...] = jnp.zeros_like(zrow)

        def tile_of(row):
            return pl.multiple_of(row * ROW_TILES, ROW_TILES)

        def per_expert(e, total):
            def fill(r, _):
                _row_dma(zrow, 0, xg_hbm, tile_of(pad0_ref[e] + r), zsem).start()
                return 0
            lax.fori_loop(0, npad_ref[e], fill, 0)
            return total + npad_ref[e]
        total = lax.fori_loop(0, N_EXPERTS, per_expert, 0)
        _wait_rows(zrow, xg_hbm, zsem, total)

        def fill_unused(r, _):
            row = pad0_ref[N_EXPERTS] + r * SUBLANES
            _row_dma(zrow, 0, xg_hbm, tile_of(row), zsem, SUBLANES).start()
            return 0
        lax.fori_loop(0, npad_ref[N_EXPERTS], fill_unused, 0)
        _wait_rows(zrow, xg_hbm, zsem, npad_ref[N_EXPERTS], SUBLANES)

    src = hbuf.at[slot]
    _store_row_tiles(src, _rms(x_ref[...], g2_ref[...]))
    for r in range(rows):
        dst_row8 = pl.multiple_of(dest_ref[s * rows + r], ROW_TILES)
        _row_dma(src, (r % tt) * ROW_TILES, xg_hbm, dst_row8, sem.at[slot]).start(priority=r % 2)

    @pl.when(s > 0)
    def _():
        _wait_rows(hbuf.at[1 - slot], xg_hbm, sem.at[1 - slot], rows)

    @pl.when(s == last)
    def _():
        _wait_rows(src, xg_hbm, sem.at[slot], rows)


def _dispatch(dest, pad0, npad, x, g2, cap):
    n = x.shape[0]
    tt = COMBINE_TILE
    grid_spec = pltpu.PrefetchScalarGridSpec(
        num_scalar_prefetch=3,
        grid=(n // tt,),
        in_specs=[pl.BlockSpec((tt, D_MODEL), lambda s, d, p, c: (s, 0)),
                  pl.BlockSpec((1, D_MODEL), lambda s, d, p, c: (0, 0))],
        out_specs=pl.BlockSpec(memory_space=pl.ANY),
        scratch_shapes=[pltpu.VMEM((2, tt * ROW_TILES, LANES), F32),
                        pltpu.VMEM((SUBLANES * ROW_TILES, LANES), F32),
                        pltpu.SemaphoreType.DMA((2,)),
                        pltpu.SemaphoreType.DMA],
    )
    return pl.pallas_call(
        _dispatch_kernel,
        grid_spec=grid_spec,
        out_shape=jax.ShapeDtypeStruct((cap * ROW_TILES, LANES), F32),
        compiler_params=_params("arbitrary"),
        name="expert_dispatch",
    )(dest, pad0, npad, x, g2)


def _expert_kernel(blk_e_ref, nused_ref, xg_ref, wgu_ref, wd_ref, y_ref, wgu_bf, wd_bf):
    b = pl.program_id(0)
    prev = blk_e_ref[jnp.maximum(b, 1) - 1]

    @pl.when((b == 0) | (blk_e_ref[b] != prev))
    def _():
        wgu_bf[...] = wgu_ref[0, 0].astype(BF16)
        wd_bf[...] = wd_ref[0, 0].astype(BF16)

    @pl.when(b < nused_ref[0])
    def _():
        gu = _dot(_load_row_tiles(xg_ref, 0, EXPERT_ROWS).astype(BF16), wgu_bf[...])
        act = jax.nn.silu(gu[:, :D_EXPERT]) * gu[:, D_EXPERT:]
        _store_row_tiles(y_ref, _dot(act.astype(BF16), wd_bf[...]))

    @pl.when(b >= nused_ref[0])
    def _():
        y_ref[...] = jnp.zeros_like(y_ref)


def _experts(blk_e, nused, xg, w_gate_up, w_down, layer):
    rows = EXPERT_ROWS * ROW_TILES
    nblk = xg.shape[0] // rows
    used = lambda b, n: jnp.minimum(b, n[0] - 1)
    grid_spec = pltpu.PrefetchScalarGridSpec(
        num_scalar_prefetch=2,
        grid=(nblk,),
        in_specs=[pl.BlockSpec((rows, LANES), lambda b, e, n: (used(b, n), 0)),
                  pl.BlockSpec((1, 1, D_MODEL, 2 * D_EXPERT), lambda b, e, n: (layer, e[b], 0, 0)),
                  pl.BlockSpec((1, 1, D_EXPERT, D_MODEL), lambda b, e, n: (layer, e[b], 0, 0))],
        out_specs=pl.BlockSpec((rows, LANES), lambda b, e, n: (b, 0)),
        scratch_shapes=[pltpu.VMEM((D_MODEL, 2 * D_EXPERT), BF16),
                        pltpu.VMEM((D_EXPERT, D_MODEL), BF16)],
    )
    return pl.pallas_call(
        _expert_kernel,
        grid_spec=grid_spec,
        out_shape=jax.ShapeDtypeStruct((nblk * rows, LANES), F32),
        compiler_params=_params("arbitrary"),
        name="experts",
    )(blk_e, nused, xg, w_gate_up, w_down)


def _combine_kernel(pos_ref, x_ref, w_ref, y_hbm, o_ref, ybuf, sem):
    s = pl.program_id(0)
    nsteps = pl.num_programs(0)
    slot = s % 2
    tt = COMBINE_TILE
    rows = EXPERT_TOPK * tt

    def issue(step, slot):
        dst = ybuf.at[slot]
        for r in range(rows):
            src_row8 = pl.multiple_of(pos_ref[step * rows + r], ROW_TILES)
            _row_dma(y_hbm, src_row8, dst, r * ROW_TILES, sem.at[slot]).start(priority=r % 2)

    @pl.when(s == 0)
    def _():
        issue(0, 0)

    @pl.when(s + 1 < nsteps)
    def _():
        issue(s + 1, 1 - slot)

    _wait_rows(y_hbm, ybuf.at[slot], sem.at[slot], rows)
    w = w_ref[...]
    got = ybuf.at[slot]
    o_ref[...] = (x_ref[...] + w[:, 0:1] * _load_row_tiles(got, 0, tt)
                  + w[:, 1:2] * _load_row_tiles(got, tt, tt))


def _combine(pos, x, w, y):
    n = x.shape[0]
    tt = COMBINE_TILE
    grid_spec = pltpu.PrefetchScalarGridSpec(
        num_scalar_prefetch=1,
        grid=(n // tt,),
        in_specs=[pl.BlockSpec((tt, D_MODEL), lambda s, p: (s, 0)),
                  pl.BlockSpec((tt, EXPERT_TOPK), lambda s, p: (s, 0)),
                  pl.BlockSpec(memory_space=pl.ANY)],
        out_specs=pl.BlockSpec((tt, D_MODEL), lambda s, p: (s, 0)),
        scratch_shapes=[pltpu.VMEM((2, EXPERT_TOPK * tt * ROW_TILES, LANES), F32),
                        pltpu.SemaphoreType.DMA((2,))],
    )
    return pl.pallas_call(
        _combine_kernel,
        grid_spec=grid_spec,
        out_shape=jax.ShapeDtypeStruct((n, D_MODEL), F32),
        compiler_params=_params("arbitrary"),
        name="moe_combine",
    )(pos, x, w, y)


def _rope_tables(seq):
    inv = ROPE_THETA ** (-jnp.arange(0, HEAD_DIM, 2, dtype=F32) / HEAD_DIM)
    ang = jnp.arange(seq, dtype=F32)[:, None] * inv[None, :]
    cos, sin = jnp.cos(ang), jnp.sin(ang)
    cos = jnp.concatenate([cos, cos] * (LANES // HEAD_DIM), axis=1)
    sin = jnp.concatenate([-sin, sin] * (LANES // HEAD_DIM), axis=1)
    return cos, sin


def _ssm_weights(a_re, a_im, log_dt, b_re, b_im, c_re, c_im):
    dt = jnp.exp(log_dt)[:, None]
    decay = jnp.exp(a_re * dt)
    abar_re = decay * jnp.cos(a_im * dt)
    abar_im = decay * jnp.sin(a_im * dt)
    den = a_re * a_re + a_im * a_im
    num_re = abar_re - 1.0
    f_re = (num_re * a_re + abar_im * a_im) / den
    f_im = (abar_im * a_re - num_re * a_im) / den
    bbar_re = f_re[..., None] * b_re - f_im[..., None] * b_im
    bbar_im = f_re[..., None] * b_im + f_im[..., None] * b_re
    gh = SSM_GROUPS // 2
    eye = jnp.eye(gh, dtype=F32)

    def b_mat(t):
        t = t.reshape(2, gh, SSM_STATE, SSM_GROUP_CH)
        return jnp.einsum('cgph,gk->cghkp', t, eye).reshape(2, SSM_HALF, SSM_HALF_STATE).astype(BF16)

    def c_mat(t):
        t = t.reshape(2, gh, SSM_GROUP_CH, SSM_STATE)
        return jnp.einsum('cghp,gk->cgpkh', t, eye).reshape(2, SSM_HALF_STATE, SSM_HALF).astype(BF16)

    a_vec = lambda t: t.reshape(2, 1, SSM_HALF_STATE)
    return (b_mat(bbar_re), b_mat(bbar_im), c_mat(c_re), c_mat(-c_im), a_vec(abar_re), a_vec(abar_im))


def _route(logits, n):
    expert_id, rank, weights, cnt = _route_rank(logits)
    counts = cnt[0, :N_EXPERTS]
    padded = (counts + EXPERT_ROWS - 1) // EXPERT_ROWS * EXPERT_ROWS
    pend = jnp.cumsum(padded)
    pstart = pend - padded
    experts = jnp.arange(N_EXPERTS, dtype=jnp.int32)
    start_of = jnp.sum(jnp.where(expert_id[..., None] == experts, pstart, 0), axis=-1)
    dest = (start_of + rank).astype(jnp.int32)
    cap = n * EXPERT_TOPK + N_EXPERTS * EXPERT_ROWS
    nblk = cap // EXPERT_ROWS
    blk_row = jnp.arange(nblk, dtype=jnp.int32)[:, None] * EXPERT_ROWS
    blk_e = jnp.minimum(jnp.sum((pend[None, :] <= blk_row).astype(jnp.int32), axis=1), N_EXPERTS - 1)
    nused = (pend[-1:] // EXPERT_ROWS).astype(jnp.int32)
    pos = dest.reshape(n // COMBINE_TILE, COMBINE_TILE, EXPERT_TOPK).transpose(0, 2, 1).reshape(-1)
    pos = pos * ROW_TILES
    pad0 = jnp.concatenate([pstart + counts, pend[-1:]]).astype(jnp.int32)
    npad = jnp.concatenate([padded - counts, (cap - pend[-1:]) // SUBLANES]).astype(jnp.int32)
    return weights, pos, pad0, npad, blk_e, nused, cap


def kernel(x, norm1_g, w_in, ssm_a_re, ssm_a_im, ssm_log_dt, ssm_b_re, ssm_b_im, ssm_c_re, ssm_c_im,
           ssm_d, w_glu, q_norm_g, k_norm_g, w_attn, w_out, norm2_g, router_w_group, router_b_group,
           router_w_expert, router_b_expert, w_gate_up, w_down):
    batch, seq, _ = x.shape
    depth = w_in.shape[0]
    n = batch * seq
    assert batch == SUBLANES and seq % MOBA_BLOCK == 0 and n % TOKEN_TILE == 0
    assert seq // MOBA_BLOCK <= GATE_ROWS and seq % SSM_STEPS == 0

    xt = x.reshape(n, D_MODEL)
    cos, sin = _rope_tables(seq)
    idx = jnp.arange(ATTN_WIDTH)
    gsum = (idx[:, None] // HEAD_DIM == idx[None, :] // HEAD_DIM).astype(BF16)
    row = lambda t: t.reshape(1, -1).astype(F32)

    for l in range(depth):
        qg = row(jnp.tile(q_norm_g[l], N_HEADS)) * (HEAD_DIM ** -0.5 * math.log2(math.e))
        kg = row(jnp.tile(k_norm_g[l], N_HEADS))
        u, q, k, v, g = _in_proj(xt, row(norm1_g[l]), w_in[l].astype(BF16), gsum, qg, kg, cos, sin)
        ssm_w = _ssm_weights(ssm_a_re[l], ssm_a_im[l], ssm_log_dt[l], ssm_b_re[l], ssm_b_im[l],
                             ssm_c_re[l], ssm_c_im[l])
        z = _ssm(u.reshape(n, SSM_WIDTH), *ssm_w, row(ssm_d[l]), batch).reshape(u.shape)
        attn = _moba(q, k, v, batch)

        w_r = jnp.concatenate([router_w_group[l], router_w_expert[l]], axis=1)
        w_r = jnp.pad(w_r, ((0, 0), (0, ROUTER_LANES - w_r.shape[1])))
        w_rh = w_r.astype(BF16)
        w_rl = (w_r - w_rh.astype(F32)).astype(BF16)
        b_r = jnp.concatenate([router_b_group[l], router_b_expert[l]])
        b_r = row(jnp.pad(b_r, (0, ROUTER_LANES - b_r.shape[0])))
        g2 = row(norm2_g[l])
        xt, logits = _mix(z, attn, g, xt, w_glu[l].astype(BF16), w_attn[l].astype(BF16),
                          w_out[l].astype(BF16), g2, w_rh, w_rl, b_r)

        weights, pos, pad0, npad, blk_e, nused, cap = _route(logits, n)
        xg = _dispatch(pos, pad0, npad, xt, g2, cap)
        y = _experts(blk_e, nused, xg, w_gate_up, w_down, l)
        xt = _combine(pos, xt, weights, y)

    return xt.reshape(batch, seq, D_MODEL)
```

```python
import functools
import math

import jax
import jax.numpy as jnp
from jax import lax
from jax.experimental import pallas as pl
from jax.experimental.pallas import tpu as pltpu

F32 = jnp.float32
BF16 = jnp.bfloat16
U32 = jnp.uint32

D_MODEL = 1024
SSM_GROUPS = 32
SSM_GROUP_CH = 16
SSM_WIDTH = SSM_GROUPS * SSM_GROUP_CH
SSM_STATE = 64
N_HEADS = 8
HEAD_DIM = 64
ATTN_WIDTH = N_HEADS * HEAD_DIM
MOBA_BLOCK = 256
MOBA_TOPK = 3
ROPE_THETA = 10000.0
D_IN = SSM_WIDTH + 3 * ATTN_WIDTH + 2 * D_MODEL
N_GROUPS = 4
EXPERTS_PER_GROUP = 8
N_EXPERTS = N_GROUPS * EXPERTS_PER_GROUP
EXPERT_TOPK = 2
D_EXPERT = 512
EXPERT_ROWS = 256
NORM_EPS = 1e-6

LANES = 128
SUBLANES = 8
VMEM_LIMIT = 56 * 1024 * 1024
MASK_NEG = -1e30

TOKEN_TILE = 512
SSM_STEPS = 64
SSM_HALF = SSM_WIDTH // 2
SSM_HALF_STATE = SSM_GROUPS * SSM_STATE // 2
GATE_ROWS = 16
V_ROWS = HEAD_DIM + 16
SOFTMAX_LAG = 3
VALUES_LAG = 6
ROUTER_LANES = 128
COMBINE_TILE = 256
ROW_TILES = D_MODEL // (2 * LANES)


def _params(*sem, flags=None):
    return pltpu.CompilerParams(dimension_semantics=sem, vmem_limit_bytes=VMEM_LIMIT, flags=flags)


def _dot(a, b):
    return jnp.dot(a, b, preferred_element_type=F32)


def _dot_nt(a, b):
    return lax.dot_general(a, b, (((1,), (1,)), ((), ())), preferred_element_type=F32)


def _rms(x, gain):
    return x * lax.rsqrt(jnp.mean(x * x, axis=-1, keepdims=True) + NORM_EPS) * gain


def _in_proj_kernel(x_ref, g1_ref, w_ref, gs_ref, qg_ref, kg_ref, cos_ref, sin_ref,
                    u_ref, q_ref, k_ref, v_ref, g_ref):
    hb = _rms(x_ref[...], g1_ref[...]).astype(BF16)

    def seg(lo, hi):
        return _dot(hb, w_ref[:, lo:hi])

    o_q = SSM_WIDTH
    o_k = o_q + ATTN_WIDTH
    o_v = o_k + ATTN_WIDTH
    o_g = o_v + ATTN_WIDTH
    u_ref[...] = seg(0, o_q).astype(BF16)
    v_ref[...] = seg(o_v, o_g).astype(BF16)
    g_ref[...] = seg(o_g, D_IN).astype(BF16)

    reps = ATTN_WIDTH // LANES
    cos = jnp.concatenate([cos_ref[...]] * reps, axis=1)
    sin = jnp.concatenate([sin_ref[...]] * reps, axis=1)
    lane = lax.broadcasted_iota(jnp.int32, cos.shape, 1)
    first_half = (lane % HEAD_DIM) < (HEAD_DIM // 2)

    def norm_rope(t, gain):
        ss = _dot((t * t).astype(BF16), gs_ref[...])
        tn = t * lax.rsqrt(ss * (1.0 / HEAD_DIM) + NORM_EPS) * gain
        partner = jnp.where(first_half,
                            pltpu.roll(tn, ATTN_WIDTH - HEAD_DIM // 2, 1),
                            pltpu.roll(tn, HEAD_DIM // 2, 1))
        return tn * cos + partner * sin

    q_ref[...] = norm_rope(seg(o_q, o_k), qg_ref[...]).astype(BF16)
    k_ref[...] = norm_rope(seg(o_k, o_v), kg_ref[...]).astype(BF16)


def _in_proj(x, g1, w_in, gsum, qg, kg, cos, sin):
    n = x.shape[0]
    tm = TOKEN_TILE
    row = lambda i: (i, 0)
    fix = lambda i: (0, 0)
    seq = cos.shape[0]
    tiles_per_seq = seq // tm
    pos = lambda i: (i % tiles_per_seq, 0)
    outs = [jax.ShapeDtypeStruct((n, w), BF16)
            for w in (SSM_WIDTH, ATTN_WIDTH, ATTN_WIDTH, ATTN_WIDTH, 2 * D_MODEL)]
    return pl.pallas_call(
        _in_proj_kernel,
        grid=(n // tm,),
        in_specs=[pl.BlockSpec((tm, D_MODEL), row),
                  pl.BlockSpec((1, D_MODEL), fix),
                  pl.BlockSpec((D_MODEL, D_IN), fix),
                  pl.BlockSpec((ATTN_WIDTH, ATTN_WIDTH), fix),
                  pl.BlockSpec((1, ATTN_WIDTH), fix),
                  pl.BlockSpec((1, ATTN_WIDTH), fix),
                  pl.BlockSpec((tm, LANES), pos),
                  pl.BlockSpec((tm, LANES), pos)],
        out_specs=[pl.BlockSpec((tm, o.shape[1]), row) for o in outs],
        out_shape=outs,
        compiler_params=_params("parallel"),
        name="in_proj",
    )(x, g1, w_in, gsum, qg, kg, cos, sin)


def _ssm_kernel(u_ref, bre_ref, bim_ref, cre_ref, cim_ref, are_ref, aim_ref, d_ref,
                z_ref, h_ref, sre_ref, sim_ref, *, batch):
    @pl.when(pl.program_id(0) == 0)
    def _():
        h_ref[...] = jnp.zeros_like(h_ref)

    steps = u_ref.shape[0] // batch
    ys = []
    for c in range(2):
        uc = u_ref[:, c * SSM_HALF:(c + 1) * SSM_HALF]
        sre_ref[...] = _dot(uc, bre_ref[c])
        sim_ref[...] = _dot(uc, bim_ref[c])
        a_re = jnp.broadcast_to(are_ref[c], (batch, SSM_HALF_STATE))
        a_im = jnp.broadcast_to(aim_ref[c], (batch, SSM_HALF_STATE))

        def step(t, carry):
            h_re, h_im = carry
            r0 = pl.multiple_of(t * batch, batch)
            n_re = a_re * h_re - a_im * h_im + sre_ref[pl.ds(r0, batch), :]
            n_im = a_re * h_im + a_im * h_re + sim_ref[pl.ds(r0, batch), :]
            sre_ref[pl.ds(r0, batch), :] = n_re
            sim_ref[pl.ds(r0, batch), :] = n_im
            return n_re, n_im

        h_re, h_im = lax.fori_loop(0, steps, step, (h_ref[c, 0], h_ref[c, 1]), unroll=True)
        h_ref[c, 0] = h_re
        h_ref[c, 1] = h_im
        ys.append(_dot(sre_ref[...].astype(BF16), cre_ref[c])
                  + _dot(sim_ref[...].astype(BF16), cim_ref[c]))
    y = jnp.concatenate(ys, axis=1) + d_ref[...] * u_ref[...].astype(F32)
    z_ref[...] = jax.nn.gelu(y).astype(BF16)


def _ssm(u, bre, bim, cre, cim, are, aim, d, batch):
    n = u.shape[0]
    rows = SSM_STEPS * batch
    fix3 = lambda i: (0, 0, 0)
    return pl.pallas_call(
        functools.partial(_ssm_kernel, batch=batch),
        grid=(n // rows,),
        in_specs=[pl.BlockSpec((rows, SSM_WIDTH), lambda i: (i, 0)),
                  pl.BlockSpec(bre.shape, fix3), pl.BlockSpec(bim.shape, fix3),
                  pl.BlockSpec(cre.shape, fix3), pl.BlockSpec(cim.shape, fix3),
                  pl.BlockSpec(are.shape, fix3), pl.BlockSpec(aim.shape, fix3),
                  pl.BlockSpec((1, SSM_WIDTH), lambda i: (0, 0))],
        out_specs=pl.BlockSpec((rows, SSM_WIDTH), lambda i: (i, 0)),
        out_shape=jax.ShapeDtypeStruct((n, SSM_WIDTH), BF16),
        scratch_shapes=[pltpu.VMEM((2, 2, batch, SSM_HALF_STATE), F32),
                        pltpu.VMEM((rows, SSM_HALF_STATE), F32),
                        pltpu.VMEM((rows, SSM_HALF_STATE), F32)],
        compiler_params=_params("arbitrary"),
        name="s5_scan",
    )(u, bre, bim, cre, cim, are, aim, d)


def _moba_kernel(q_ref, k_ref, v_ref, o_ref, kpad_ref, vt_ref, km_ref, qt_ref, m_ref, alpha_ref, acc_ref,
                 s_ref, p_ref):
    i = pl.program_id(1)
    blk = MOBA_BLOCK
    nblk = k_ref.shape[0] // blk
    n_sel = min(MOBA_TOPK, nblk - 1)

    @pl.when(i == 0)
    def _build():
        lane = lax.broadcasted_iota(jnp.int32, (blk, LANES), 1)
        is_head_lane = lane < HEAD_DIM
        km_ref[...] = jnp.zeros_like(km_ref)
        ones_row = (lax.broadcasted_iota(jnp.int32, (V_ROWS - HEAD_DIM, blk), 0) == 0)

        def per_block(j, _):
            r0 = pl.multiple_of(j * blk, blk)
            vt = v_ref[pl.ds(r0, blk), :].astype(F32).T
            for h in range(N_HEADS):
                p = h // 2
                kk = k_ref[pl.ds(r0, blk), p * LANES:(p + 1) * LANES].astype(F32)
                if h % 2:
                    kk = pltpu.roll(kk, HEAD_DIM, 1)
                kk = jnp.where(is_head_lane, kk, 0.0)
                km_ref[h, pl.ds(j, 1), :] = jnp.mean(kk, axis=0, keepdims=True)
                kk = jnp.where(lane == HEAD_DIM + j, 1.0, kk)
                kpad_ref[j, :, h * LANES:(h + 1) * LANES] = kk.astype(BF16)
                vt_ref[j, h, 0:HEAD_DIM, :] = vt[h * HEAD_DIM:(h + 1) * HEAD_DIM, :].astype(BF16)
                vt_ref[j, h, HEAD_DIM:V_ROWS, :] = ones_row.astype(BF16)
            return 0

        lax.fori_loop(0, nblk, per_block, 0)

    qt = q_ref[...].astype(F32).T
    blk_row = lax.broadcasted_iota(jnp.int32, (GATE_ROWS, blk), 0)
    past = blk_row < i
    blk_row_f = blk_row.astype(F32)
    zeros_q = jnp.zeros((LANES - HEAD_DIM, blk), F32)
    zeros_pad = jnp.zeros((LANES - HEAD_DIM - GATE_ROWS, blk), F32)
    for h in range(N_HEADS):
        qh = qt[h * HEAD_DIM:(h + 1) * HEAD_DIM, :]
        q_pad = jnp.concatenate([qh, zeros_q], axis=0).astype(BF16)
        km = km_ref[h]
        km_hi = km.astype(BF16)
        km_lo = (km - km_hi.astype(F32)).astype(BF16)
        gate = jnp.where(past, _dot(km_hi, q_pad) + _dot(km_lo, q_pad), -jnp.inf)
        sel = jnp.zeros(gate.shape, jnp.bool_)
        for _ in range(n_sel):
            best = jnp.max(gate, axis=0, keepdims=True)
            cand = (gate == best) & (best > -jnp.inf)
            first = jnp.min(jnp.where(cand, blk_row_f, float(GATE_ROWS)), axis=0, keepdims=True)
            pick = blk_row_f == first
            sel = sel | pick
            gate = jnp.where(pick, -jnp.inf, gate)
        bias = jnp.where(past & ~sel, MASK_NEG, 0.0)
        qt_ref[h] = jnp.concatenate([qh, bias, zeros_pad], axis=0).astype(BF16)
        m_ref[h] = jnp.full((1, blk), -jnp.inf, F32)
        acc_ref[h] = jnp.zeros((V_ROWS, blk), F32)

    n_s, n_p = s_ref.shape[0], p_ref.shape[0]

    def scores(j, h, u):
        s_ref[u % n_s] = _dot(kpad_ref[j, :, h * LANES:(h + 1) * LANES], qt_ref[h])

    def softmax(h, u, mask):
        st = s_ref[u % n_s]
        if mask is not None:
            st = jnp.where(mask, st, MASK_NEG)
        m_old = m_ref[h]
        m_new = jnp.maximum(m_old, jnp.max(st, axis=0, keepdims=True))
        alpha_ref[h] = jnp.exp2(m_old - m_new)
        p_ref[u % n_p] = jnp.exp2((st - m_new).astype(BF16))
        m_ref[h] = m_new

    def weighted_values(j, h, u):
        acc_ref[h] = alpha_ref[h] * acc_ref[h] + _dot(vt_ref[j, h], p_ref[u % n_p])

    def kv_blocks(js, mask):
        units = [(j, h) for j in js for h in range(N_HEADS)]
        for step in range(len(units) + VALUES_LAG):
            if step < len(units):
                scores(*units[step], step)
            u = step - SOFTMAX_LAG
            if 0 <= u < len(units):
                softmax(units[u][1], u, mask)
            u = step - VALUES_LAG
            if 0 <= u < len(units):
                weighted_values(*units[u], u)

    def quad(g, _):
        kv_blocks([g * 4 + d for d in range(4)], None)
        return 0

    n_quads = i // 4
    lax.fori_loop(0, n_quads, quad, 0)

    @pl.when(i % 4 >= 2)
    def _():
        kv_blocks([n_quads * 4, n_quads * 4 + 1], None)

    @pl.when(i % 2 == 1)
    def _():
        kv_blocks([i - 1], None)

    key_pos = lax.broadcasted_iota(jnp.int32, (blk, blk), 0)
    qry_pos = lax.broadcasted_iota(jnp.int32, (blk, blk), 1)
    kv_blocks([i], key_pos <= qry_pos)
    outs = []
    for h in range(N_HEADS):
        acc = acc_ref[h]
        outs.append(acc[0:HEAD_DIM, :] / acc[HEAD_DIM:HEAD_DIM + 1, :])
    o_ref[...] = jnp.concatenate(outs, axis=0).T.astype(BF16)


def _moba(q, k, v, batch):
    n = q.shape[0]
    seq = n // batch
    tq = MOBA_BLOCK
    nblk = seq // MOBA_BLOCK
    return pl.pallas_call(
        _moba_kernel,
        grid=(batch, nblk),
        in_specs=[pl.BlockSpec((tq, ATTN_WIDTH), lambda b, i: (b * nblk + i, 0)),
                  pl.BlockSpec((seq, ATTN_WIDTH), lambda b, i: (b, 0)),
                  pl.BlockSpec((seq, ATTN_WIDTH), lambda b, i: (b, 0))],
        out_specs=pl.BlockSpec((tq, ATTN_WIDTH), lambda b, i: (b * nblk + i, 0)),
        out_shape=jax.ShapeDtypeStruct((n, ATTN_WIDTH), BF16),
        scratch_shapes=[pltpu.VMEM((nblk, MOBA_BLOCK, N_HEADS * LANES), BF16),
                        pltpu.VMEM((nblk, N_HEADS, V_ROWS, MOBA_BLOCK), BF16),
                        pltpu.VMEM((N_HEADS, GATE_ROWS, LANES), F32),
                        pltpu.VMEM((N_HEADS, LANES, MOBA_BLOCK), BF16),
                        pltpu.VMEM((N_HEADS, 1, MOBA_BLOCK), F32),
                        pltpu.VMEM((N_HEADS, 1, MOBA_BLOCK), F32),
                        pltpu.VMEM((N_HEADS, V_ROWS, MOBA_BLOCK), F32),
                        pltpu.VMEM((VALUES_LAG, MOBA_BLOCK, MOBA_BLOCK), F32),
                        pltpu.VMEM((VALUES_LAG - SOFTMAX_LAG + 1, MOBA_BLOCK, MOBA_BLOCK), BF16)],
        compiler_params=_params("parallel", "arbitrary"),
        name="moba_attention",
    )(q, k, v)


def _mix_kernel(z_ref, a_ref, g_ref, x_ref, wglu_ref, wattn_ref, wout_ref, g2_ref,
                wrh_ref, wrl_ref, rb_ref, xo_ref, lg_ref):
    glu = _dot(z_ref[...], wglu_ref[...])
    y_ssm = glu[:, :D_MODEL] * jax.nn.sigmoid(glu[:, D_MODEL:])
    y_attn = _dot(a_ref[...], wattn_ref[...])
    g = g_ref[...].astype(F32)
    mixed = jax.nn.sigmoid(g[:, :D_MODEL]) * y_ssm + jax.nn.sigmoid(g[:, D_MODEL:]) * y_attn
    x = x_ref[...] + _dot(mixed.astype(BF16), wout_ref[...])
    xo_ref[...] = x
    h2 = _rms(x, g2_ref[...])
    hi = h2.astype(BF16)
    lo = (h2 - hi.astype(F32)).astype(BF16)
    lg_ref[...] = (_dot(hi, wrh_ref[...]) + _dot(lo, wrh_ref[...]) + _dot(hi, wrl_ref[...])
                   + rb_ref[...])


def _mix(z, a, g, x, wglu, wattn, wout, g2, wrh, wrl, rb):
    n = x.shape[0]
    tm = TOKEN_TILE
    row = lambda i: (i, 0)
    fix = lambda i: (0, 0)
    full = lambda t: pl.BlockSpec(t.shape, fix)
    return pl.pallas_call(
        _mix_kernel,
        grid=(n // tm,),
        in_specs=[pl.BlockSpec((tm, SSM_WIDTH), row), pl.BlockSpec((tm, ATTN_WIDTH), row),
                  pl.BlockSpec((tm, 2 * D_MODEL), row), pl.BlockSpec((tm, D_MODEL), row),
                  full(wglu), full(wattn), full(wout), full(g2), full(wrh), full(wrl), full(rb)],
        out_specs=[pl.BlockSpec((tm, D_MODEL), row), pl.BlockSpec((tm, ROUTER_LANES), row)],
        out_shape=[jax.ShapeDtypeStruct((n, D_MODEL), F32),
                   jax.ShapeDtypeStruct((n, ROUTER_LANES), F32)],
        compiler_params=_params("parallel"),
        name="mix_out_router",
    )(z, a, g, x, wglu, wattn, wout, g2, wrh, wrl, rb)


def _route_kernel(lg_ref, eid_ref, rank_ref, w_ref, cnt_ref, carry_ref):
    @pl.when(pl.program_id(0) == 0)
    def _():
        carry_ref[...] = jnp.zeros_like(carry_ref)

    lg = lg_ref[...]
    t = lg.shape[0]
    lane = lax.broadcasted_iota(jnp.int32, (t, LANES), 1).astype(F32)

    def first_lane(hit):
        return jnp.min(jnp.where(hit, lane, float(LANES)), axis=1, keepdims=True)

    g_log = jnp.where(lane < N_GROUPS, lg, -jnp.inf)
    g_max = jnp.max(g_log, axis=1, keepdims=True)
    g_top = first_lane(g_log == g_max)
    p_g = 1.0 / jnp.sum(jnp.exp(g_log - g_max), axis=1, keepdims=True)

    lo = N_GROUPS + g_top * EXPERTS_PER_GROUP
    in_group = (lane >= lo) & (lane < lo + EXPERTS_PER_GROUP)
    e_log = jnp.where(in_group, lg, -jnp.inf)
    e_exp = jnp.exp(e_log - jnp.max(e_log, axis=1, keepdims=True))
    prob = jnp.where(in_group, e_exp / jnp.sum(e_exp, axis=1, keepdims=True), -1.0)
    p1 = jnp.max(prob, axis=1, keepdims=True)
    l1 = first_lane(prob == p1)
    prob = jnp.where(lane == l1, -1.0, prob)
    p2 = jnp.max(prob, axis=1, keepdims=True)
    l2 = first_lane(prob == p2)
    w_ref[...] = jnp.concatenate([p_g * (p1 / (p1 + p2)), p_g * (p2 / (p1 + p2))], axis=1)
    eid = [l1 - N_GROUPS, l2 - N_GROUPS]
    eid_ref[...] = jnp.concatenate(eid, axis=1).astype(jnp.int32)

    earlier = (lax.broadcasted_iota(jnp.int32, (t, t), 1)
               < lax.broadcasted_iota(jnp.int32, (t, t), 0)).astype(BF16)
    carry = carry_ref[...]
    ranks = []
    for e in eid:
        hit = lane == e
        onehot = hit.astype(F32)
        before = _dot(earlier, onehot.astype(BF16)) + carry
        ranks.append(jnp.sum(jnp.where(hit, before, 0.0), axis=1, keepdims=True))
        carry = carry + jnp.sum(onehot, axis=0, keepdims=True)
    carry_ref[...] = carry
    rank_ref[...] = jnp.concatenate(ranks, axis=1).astype(jnp.int32)
    cnt_ref[...] = carry.astype(jnp.int32)


def _route_rank(logits):
    n = logits.shape[0]
    t = TOKEN_TILE
    pair = pl.BlockSpec((t, EXPERT_TOPK), lambda i: (i, 0))
    return pl.pallas_call(
        _route_kernel,
        grid=(n // t,),
        in_specs=[pl.BlockSpec((t, ROUTER_LANES), lambda i: (i, 0))],
        out_specs=[pair, pair, pair, pl.BlockSpec((1, LANES), lambda i: (0, 0))],
        out_shape=[jax.ShapeDtypeStruct((n, EXPERT_TOPK), jnp.int32),
                   jax.ShapeDtypeStruct((n, EXPERT_TOPK), jnp.int32),
                   jax.ShapeDtypeStruct((n, EXPERT_TOPK), F32),
                   jax.ShapeDtypeStruct((1, LANES), jnp.int32)],
        scratch_shapes=[pltpu.VMEM((1, LANES), F32)],
        compiler_params=_params("arbitrary"),
        name="route_rank",
    )(logits)


def _store_row_tiles(ref, value):
    rows, half = value.shape[0], value.shape[1] // 2
    as_bits = lambda v: lax.bitcast_convert_type(v.astype(BF16).astype(F32), U32)
    words = as_bits(value[:, half:]) | (as_bits(value[:, :half]) >> 16)
    for c in range(ROW_TILES):
        ref[pl.ds(c, rows, stride=ROW_TILES), :] = words[:, c * LANES:(c + 1) * LANES]


def _load_row_tiles(ref, start, rows):
    words = jnp.concatenate(
        [ref[pl.ds(start * ROW_TILES + c, rows, stride=ROW_TILES), :] for c in range(ROW_TILES)], axis=1)
    low = lax.bitcast_convert_type(words << 16, F32)
    high = lax.bitcast_convert_type(words & jnp.uint32(0xFFFF0000), F32)
    return jnp.concatenate([low, high], axis=1)


def _row_dma(src, src_row8, dst, dst_row8, sem, rows=1):
    n = rows * ROW_TILES
    return pltpu.make_async_copy(src.at[pl.ds(src_row8, n), :], dst.at[pl.ds(dst_row8, n), :], sem)


def _wait_rows(src, dst, sem, count, rows=1):
    def body(r, _):
        _row_dma(src, 0, dst, 0, sem, rows).wait()
        return 0
    lax.fori_loop(0, count, body, 0, unroll=8 if isinstance(count, int) else 1)


def _dispatch_kernel(dest_ref, pad0_ref, npad_ref, x_ref, g2_ref, xg_hbm, hbuf, zrow, sem, zsem):
    s = pl.program_id(0)
    last = pl.num_programs(0) - 1
    slot = s % 2
    tt = x_ref.shape[0]
    rows = EXPERT_TOPK * tt

    @pl.when(s == 0)
    def _():
        zrow[...] = jnp.zeros_like(zrow)

        def tile_of(row):
            return pl.multiple_of(row * ROW_TILES, ROW_TILES)

        def per_expert(e, total):
            def fill(r, _):
                _row_dma(zrow, 0, xg_hbm, tile_of(pad0_ref[e] + r), zsem).start()
                return 0
            lax.fori_loop(0, npad_ref[e], fill, 0)
            return total + npad_ref[e]
        total = lax.fori_loop(0, N_EXPERTS, per_expert, 0)
        _wait_rows(zrow, xg_hbm, zsem, total)

        def fill_unused(r, _):
            row = pad0_ref[N_EXPERTS] + r * SUBLANES
            _row_dma(zrow, 0, xg_hbm, tile_of(row), zsem, SUBLANES).start()
            return 0
        lax.fori_loop(0, npad_ref[N_EXPERTS], fill_unused, 0)
        _wait_rows(zrow, xg_hbm, zsem, npad_ref[N_EXPERTS], SUBLANES)

    src = hbuf.at[slot]
    _store_row_tiles(src, _rms(x_ref[...], g2_ref[...]))
    for r in range(rows):
        dst_row8 = pl.multiple_of(dest_ref[s * rows + r], ROW_TILES)
        _row_dma(src, (r % tt) * ROW_TILES, xg_hbm, dst_row8, sem.at[slot]).start(priority=r % 2)

    @pl.when(s > 0)
    def _():
        _wait_rows(hbuf.at[1 - slot], xg_hbm, sem.at[1 - slot], rows)

    @pl.when(s == last)
    def _():
        _wait_rows(src, xg_hbm, sem.at[slot], rows)


def _dispatch(dest, pad0, npad, x, g2, cap):
    n = x.shape[0]
    tt = COMBINE_TILE
    grid_spec = pltpu.PrefetchScalarGridSpec(
        num_scalar_prefetch=3,
        grid=(n // tt,),
        in_specs=[pl.BlockSpec((tt, D_MODEL), lambda s, d, p, c: (s, 0)),
                  pl.BlockSpec((1, D_MODEL), lambda s, d, p, c: (0, 0))],
        out_specs=pl.BlockSpec(memory_space=pl.ANY),
        scratch_shapes=[pltpu.VMEM((2, tt * ROW_TILES, LANES), U32),
                        pltpu.VMEM((SUBLANES * ROW_TILES, LANES), U32),
                        pltpu.SemaphoreType.DMA((2,)),
                        pltpu.SemaphoreType.DMA],
    )
    return pl.pallas_call(
        _dispatch_kernel,
        grid_spec=grid_spec,
        out_shape=jax.ShapeDtypeStruct((cap * ROW_TILES, LANES), U32),
        compiler_params=_params("arbitrary"),
        name="expert_dispatch",
    )(dest, pad0, npad, x, g2)


def _expert_kernel(blk_e_ref, nused_ref, xg_ref, wgu_ref, wd_ref, y_ref, wgu_bf, wd_bf):
    b = pl.program_id(0)
    prev = blk_e_ref[jnp.maximum(b, 1) - 1]

    @pl.when((b == 0) | (blk_e_ref[b] != prev))
    def _():
        wgu_bf[...] = wgu_ref[0, 0].astype(BF16)
        wd_bf[...] = wd_ref[0, 0].astype(BF16)

    @pl.when(b < nused_ref[0])
    def _():
        gu = _dot(_load_row_tiles(xg_ref, 0, EXPERT_ROWS).astype(BF16), wgu_bf[...])
        act = jax.nn.silu(gu[:, :D_EXPERT]) * gu[:, D_EXPERT:]
        _store_row_tiles(y_ref, _dot(act.astype(BF16), wd_bf[...]))

    @pl.when(b >= nused_ref[0])
    def _():
        y_ref[...] = jnp.zeros_like(y_ref)


def _experts(blk_e, nused, xg, w_gate_up, w_down, layer):
    rows = EXPERT_ROWS * ROW_TILES
    nblk = xg.shape[0] // rows
    used = lambda b, n: jnp.minimum(b, n[0] - 1)
    grid_spec = pltpu.PrefetchScalarGridSpec(
        num_scalar_prefetch=2,
        grid=(nblk,),
        in_specs=[pl.BlockSpec((rows, LANES), lambda b, e, n: (used(b, n), 0)),
                  pl.BlockSpec((1, 1, D_MODEL, 2 * D_EXPERT), lambda b, e, n: (layer, e[b], 0, 0)),
                  pl.BlockSpec((1, 1, D_EXPERT, D_MODEL), lambda b, e, n: (layer, e[b], 0, 0))],
        out_specs=pl.BlockSpec((rows, LANES), lambda b, e, n: (b, 0)),
        scratch_shapes=[pltpu.VMEM((D_MODEL, 2 * D_EXPERT), BF16),
                        pltpu.VMEM((D_EXPERT, D_MODEL), BF16)],
    )
    return pl.pallas_call(
        _expert_kernel,
        grid_spec=grid_spec,
        out_shape=jax.ShapeDtypeStruct((nblk * rows, LANES), U32),
        compiler_params=_params("arbitrary"),
        name="experts",
    )(blk_e, nused, xg, w_gate_up, w_down)


def _combine_kernel(pos_ref, x_ref, w_ref, y_hbm, o_ref, ybuf, sem):
    s = pl.program_id(0)
    nsteps = pl.num_programs(0)
    slot = s % 2
    tt = COMBINE_TILE
    rows = EXPERT_TOPK * tt

    def issue(step, slot):
        dst = ybuf.at[slot]
        for r in range(rows):
            src_row8 = pl.multiple_of(pos_ref[step * rows + r], ROW_TILES)
            _row_dma(y_hbm, src_row8, dst, r * ROW_TILES, sem.at[slot]).start(priority=r % 2)

    @pl.when(s == 0)
    def _():
        issue(0, 0)

    @pl.when(s + 1 < nsteps)
    def _():
        issue(s + 1, 1 - slot)

    _wait_rows(y_hbm, ybuf.at[slot], sem.at[slot], rows)
    w = w_ref[...]
    got = ybuf.at[slot]
    o_ref[...] = (x_ref[...] + w[:, 0:1] * _load_row_tiles(got, 0, tt)
                  + w[:, 1:2] * _load_row_tiles(got, tt, tt))


def _combine(pos, x, w, y):
    n = x.shape[0]
    tt = COMBINE_TILE
    grid_spec = pltpu.PrefetchScalarGridSpec(
        num_scalar_prefetch=1,
        grid=(n // tt,),
        in_specs=[pl.BlockSpec((tt, D_MODEL), lambda s, p: (s, 0)),
                  pl.BlockSpec((tt, EXPERT_TOPK), lambda s, p: (s, 0)),
                  pl.BlockSpec(memory_space=pl.ANY)],
        out_specs=pl.BlockSpec((tt, D_MODEL), lambda s, p: (s, 0)),
        scratch_shapes=[pltpu.VMEM((2, EXPERT_TOPK * tt * ROW_TILES, LANES), U32),
                        pltpu.SemaphoreType.DMA((2,))],
    )
    return pl.pallas_call(
        _combine_kernel,
        grid_spec=grid_spec,
        out_shape=jax.ShapeDtypeStruct((n, D_MODEL), F32),
        compiler_params=_params("arbitrary"),
        name="moe_combine",
    )(pos, x, w, y)


def _rope_tables(seq):
    inv = ROPE_THETA ** (-jnp.arange(0, HEAD_DIM, 2, dtype=F32) / HEAD_DIM)
    ang = jnp.arange(seq, dtype=F32)[:, None] * inv[None, :]
    cos, sin = jnp.cos(ang), jnp.sin(ang)
    cos = jnp.concatenate([cos, cos] * (LANES // HEAD_DIM), axis=1)
    sin = jnp.concatenate([-sin, sin] * (LANES // HEAD_DIM), axis=1)
    return cos, sin


def _ssm_weights(a_re, a_im, log_dt, b_re, b_im, c_re, c_im):
    dt = jnp.exp(log_dt)[:, None]
    decay = jnp.exp(a_re * dt)
    abar_re = decay * jnp.cos(a_im * dt)
    abar_im = decay * jnp.sin(a_im * dt)
    den = a_re * a_re + a_im * a_im
    num_re = abar_re - 1.0
    f_re = (num_re * a_re + abar_im * a_im) / den
    f_im = (abar_im * a_re - num_re * a_im) / den
    bbar_re = f_re[..., None] * b_re - f_im[..., None] * b_im
    bbar_im = f_re[..., None] * b_im + f_im[..., None] * b_re
    gh = SSM_GROUPS // 2
    eye = jnp.eye(gh, dtype=F32)

    def b_mat(t):
        t = t.reshape(2, gh, SSM_STATE, SSM_GROUP_CH)
        return jnp.einsum('cgph,gk->cghkp', t, eye).reshape(2, SSM_HALF, SSM_HALF_STATE).astype(BF16)

    def c_mat(t):
        t = t.reshape(2, gh, SSM_GROUP_CH, SSM_STATE)
        return jnp.einsum('cghp,gk->cgpkh', t, eye).reshape(2, SSM_HALF_STATE, SSM_HALF).astype(BF16)

    a_vec = lambda t: t.reshape(2, 1, SSM_HALF_STATE)
    return (b_mat(bbar_re), b_mat(bbar_im), c_mat(c_re), c_mat(-c_im), a_vec(abar_re), a_vec(abar_im))


def _route(logits, n):
    expert_id, rank, weights, cnt = _route_rank(logits)
    counts = cnt[0, :N_EXPERTS]
    padded = (counts + EXPERT_ROWS - 1) // EXPERT_ROWS * EXPERT_ROWS
    pend = jnp.cumsum(padded)
    pstart = pend - padded
    experts = jnp.arange(N_EXPERTS, dtype=jnp.int32)
    start_of = jnp.sum(jnp.where(expert_id[..., None] == experts, pstart, 0), axis=-1)
    dest = (start_of + rank).astype(jnp.int32)
    cap = n * EXPERT_TOPK + N_EXPERTS * EXPERT_ROWS
    nblk = cap // EXPERT_ROWS
    blk_row = jnp.arange(nblk, dtype=jnp.int32)[:, None] * EXPERT_ROWS
    blk_e = jnp.minimum(jnp.sum((pend[None, :] <= blk_row).astype(jnp.int32), axis=1), N_EXPERTS - 1)
    nused = (pend[-1:] // EXPERT_ROWS).astype(jnp.int32)
    pos = dest.reshape(n // COMBINE_TILE, COMBINE_TILE, EXPERT_TOPK).transpose(0, 2, 1).reshape(-1)
    pos = pos * ROW_TILES
    pad0 = jnp.concatenate([pstart + counts, pend[-1:]]).astype(jnp.int32)
    npad = jnp.concatenate([padded - counts, (cap - pend[-1:]) // SUBLANES]).astype(jnp.int32)
    return weights, pos, pad0, npad, blk_e, nused, cap


def kernel(x, norm1_g, w_in, ssm_a_re, ssm_a_im, ssm_log_dt, ssm_b_re, ssm_b_im, ssm_c_re, ssm_c_im,
           ssm_d, w_glu, q_norm_g, k_norm_g, w_attn, w_out, norm2_g, router_w_group, router_b_group,
           router_w_expert, router_b_expert, w_gate_up, w_down):
    batch, seq, _ = x.shape
    depth = w_in.shape[0]
    n = batch * seq
    assert batch == SUBLANES and seq % MOBA_BLOCK == 0 and n % TOKEN_TILE == 0
    assert seq // MOBA_BLOCK <= GATE_ROWS and seq % SSM_STEPS == 0

    xt = x.reshape(n, D_MODEL)
    cos, sin = _rope_tables(seq)
    time_major = lambda t: t.reshape(batch, seq, -1).transpose(1, 0, 2).reshape(n, -1)
    batch_major = lambda t: t.reshape(seq, batch, -1).transpose(1, 0, 2).reshape(n, -1)
    idx = jnp.arange(ATTN_WIDTH)
    gsum =(idx[:, None] // HEAD_DIM == idx[None, :] // HEAD_DIM).astype(BF16)
    row = lambda t: t.reshape(1, -1).astype(F32)

    for l in range(depth):
        qg = row(jnp.tile(q_norm_g[l], N_HEADS)) * (HEAD_DIM ** -0.5 * math.log2(math.e))
        kg = row(jnp.tile(k_norm_g[l], N_HEADS))
        u, q, k, v, g = _in_proj(xt, row(norm1_g[l]), w_in[l].astype(BF16), gsum, qg, kg, cos, sin)
        ssm_w = _ssm_weights(ssm_a_re[l], ssm_a_im[l], ssm_log_dt[l], ssm_b_re[l], ssm_b_im[l],
                             ssm_c_re[l], ssm_c_im[l])
        z = batch_major(_ssm(time_major(u), *ssm_w, row(ssm_d[l]), batch))
        attn = _moba(q, k, v, batch)

        w_r = jnp.concatenate([router_w_group[l], router_w_expert[l]], axis=1)
        w_r = jnp.pad(w_r, ((0, 0), (0, ROUTER_LANES - w_r.shape[1])))
        w_rh = w_r.astype(BF16)
        w_rl = (w_r - w_rh.astype(F32)).astype(BF16)
        b_r = jnp.concatenate([router_b_group[l], router_b_expert[l]])
        b_r = row(jnp.pad(b_r, (0, ROUTER_LANES - b_r.shape[0])))
        g2 = row(norm2_g[l])
        xt, logits = _mix(z, attn, g, xt, w_glu[l].astype(BF16), w_attn[l].astype(BF16),
                          w_out[l].astype(BF16), g2, w_rh, w_rl, b_r)

        weights, pos, pad0, npad, blk_e, nused, cap = _route(logits, n)
        xg = _dispatch(pos, pad0, npad, xt, g2, cap)
        y = _experts(blk_e, nused, xg, w_gate_up, w_down, l)
        xt = _combine(pos, xt, weights, y)

    return xt.reshape(batch, seq, D_MODEL)
```

```python
import functools
import math

import jax
import jax.numpy as jnp
from jax import lax
from jax.experimental import pallas as pl
from jax.experimental.pallas import tpu as pltpu

F32 = jnp.float32
BF16 = jnp.bfloat16
U32 = jnp.uint32

D_MODEL = 1024
SSM_GROUPS = 32
SSM_GROUP_CH = 16
SSM_WIDTH = SSM_GROUPS * SSM_GROUP_CH
SSM_STATE = 64
N_HEADS = 8
HEAD_DIM = 64
ATTN_WIDTH = N_HEADS * HEAD_DIM
MOBA_BLOCK = 256
MOBA_TOPK = 3
ROPE_THETA = 10000.0
D_IN = SSM_WIDTH + 3 * ATTN_WIDTH + 2 * D_MODEL
N_GROUPS = 4
EXPERTS_PER_GROUP = 8
N_EXPERTS = N_GROUPS * EXPERTS_PER_GROUP
EXPERT_TOPK = 2
D_EXPERT = 512
EXPERT_ROWS = 256
NORM_EPS = 1e-6

LANES = 128
SUBLANES = 8
VMEM_LIMIT = 56 * 1024 * 1024
MASK_NEG = -1e30

TOKEN_TILE = 512
SSM_STEPS = 64
SSM_HALF = SSM_WIDTH // 2
SSM_HALF_STATE = SSM_GROUPS * SSM_STATE // 2
GATE_ROWS = 16
V_ROWS = HEAD_DIM + 16
SOFTMAX_LAG = 3
VALUES_LAG = 6
ROUTER_LANES = 128
COMBINE_TILE = 256
ROW_TILES = D_MODEL // (2 * LANES)


def _params(*sem, flags=None):
    return pltpu.CompilerParams(dimension_semantics=sem, vmem_limit_bytes=VMEM_LIMIT, flags=flags)


def _dot(a, b):
    return jnp.dot(a, b, preferred_element_type=F32)


def _dot_nt(a, b):
    return lax.dot_general(a, b, (((1,), (1,)), ((), ())), preferred_element_type=F32)


def _rms(x, gain):
    return x * lax.rsqrt(jnp.mean(x * x, axis=-1, keepdims=True) + NORM_EPS) * gain


def _in_proj_kernel(x_ref, g1_ref, w_ref, gs_ref, qg_ref, kg_ref, cos_ref, sin_ref,
                    u_ref, q_ref, k_ref, v_ref, g_ref):
    hb = _rms(x_ref[...], g1_ref[...]).astype(BF16)

    def seg(lo, hi):
        return _dot(hb, w_ref[:, lo:hi])

    o_q = SSM_WIDTH
    o_k = o_q + ATTN_WIDTH
    o_v = o_k + ATTN_WIDTH
    o_g = o_v + ATTN_WIDTH
    u_ref[...] = seg(0, o_q).astype(BF16)
    v_ref[...] = seg(o_v, o_g).astype(BF16)
    g_ref[...] = seg(o_g, D_IN).astype(BF16)

    reps = ATTN_WIDTH // LANES
    cos = jnp.concatenate([cos_ref[...]] * reps, axis=1)
    sin = jnp.concatenate([sin_ref[...]] * reps, axis=1)
    lane = lax.broadcasted_iota(jnp.int32, cos.shape, 1)
    first_half = (lane % HEAD_DIM) < (HEAD_DIM // 2)

    def norm_rope(t, gain):
        ss = _dot((t * t).astype(BF16), gs_ref[...])
        tn = t * lax.rsqrt(ss * (1.0 / HEAD_DIM) + NORM_EPS) * gain
        partner = jnp.where(first_half,
                            pltpu.roll(tn, ATTN_WIDTH - HEAD_DIM // 2, 1),
                            pltpu.roll(tn, HEAD_DIM // 2, 1))
        return tn * cos + partner * sin

    q_ref[...] = norm_rope(seg(o_q, o_k), qg_ref[...]).astype(BF16)
    k_ref[...] = norm_rope(seg(o_k, o_v), kg_ref[...]).astype(BF16)


def _in_proj(x, g1, w_in, gsum, qg, kg, cos, sin):
    n = x.shape[0]
    tm = TOKEN_TILE
    row = lambda i: (i, 0)
    fix = lambda i: (0, 0)
    seq = cos.shape[0]
    tiles_per_seq = seq // tm
    pos = lambda i: (i % tiles_per_seq, 0)
    outs = [jax.ShapeDtypeStruct((n, w), BF16)
            for w in (SSM_WIDTH, ATTN_WIDTH, ATTN_WIDTH, ATTN_WIDTH, 2 * D_MODEL)]
    return pl.pallas_call(
        _in_proj_kernel,
        grid=(n // tm,),
        in_specs=[pl.BlockSpec((tm, D_MODEL), row),
                  pl.BlockSpec((1, D_MODEL), fix),
                  pl.BlockSpec((D_MODEL, D_IN), fix),
                  pl.BlockSpec((ATTN_WIDTH, ATTN_WIDTH), fix),
                  pl.BlockSpec((1, ATTN_WIDTH), fix),
                  pl.BlockSpec((1, ATTN_WIDTH), fix),
                  pl.BlockSpec((tm, LANES), pos),
                  pl.BlockSpec((tm, LANES), pos)],
        out_specs=[pl.BlockSpec((tm, o.shape[1]), row) for o in outs],
        out_shape=outs,
        compiler_params=_params("parallel"),
        name="in_proj",
    )(x, g1, w_in, gsum, qg, kg, cos, sin)


def _ssm_kernel(u_ref, bre_ref, bim_ref, cre_ref, cim_ref, are_ref, aim_ref, d_ref,
                z_ref, h_ref, sre_ref, sim_ref, *, batch):
    @pl.when(pl.program_id(0) == 0)
    def _():
        h_ref[...] = jnp.zeros_like(h_ref)

    steps = u_ref.shape[0] // batch
    ys = []
    for c in range(2):
        uc = u_ref[:, c * SSM_HALF:(c + 1) * SSM_HALF]
        sre_ref[...] = _dot(uc, bre_ref[c])
        sim_ref[...] = _dot(uc, bim_ref[c])
        a_re = jnp.broadcast_to(are_ref[c], (batch, SSM_HALF_STATE))
        a_im = jnp.broadcast_to(aim_ref[c], (batch, SSM_HALF_STATE))

        def step(t, carry):
            h_re, h_im = carry
            r0 = pl.multiple_of(t * batch, batch)
            n_re = a_re * h_re - a_im * h_im + sre_ref[pl.ds(r0, batch), :]
            n_im = a_re * h_im + a_im * h_re + sim_ref[pl.ds(r0, batch), :]
            sre_ref[pl.ds(r0, batch), :] = n_re
            sim_ref[pl.ds(r0, batch), :] = n_im
            return n_re, n_im

        h_re, h_im = lax.fori_loop(0, steps, step, (h_ref[c, 0], h_ref[c, 1]), unroll=True)
        h_ref[c, 0] = h_re
        h_ref[c, 1] = h_im
        ys.append(_dot(sre_ref[...].astype(BF16), cre_ref[c])
                  + _dot(sim_ref[...].astype(BF16), cim_ref[c]))
    y = jnp.concatenate(ys, axis=1) + d_ref[...] * u_ref[...].astype(F32)
    z_ref[...] = jax.nn.gelu(y).astype(BF16)


def _ssm(u, bre, bim, cre, cim, are, aim, d, batch):
    n = u.shape[0]
    rows = SSM_STEPS * batch
    fix3 = lambda i: (0, 0, 0)
    return pl.pallas_call(
        functools.partial(_ssm_kernel, batch=batch),
        grid=(n // rows,),
        in_specs=[pl.BlockSpec((rows, SSM_WIDTH), lambda i: (i, 0)),
                  pl.BlockSpec(bre.shape, fix3), pl.BlockSpec(bim.shape, fix3),
                  pl.BlockSpec(cre.shape, fix3), pl.BlockSpec(cim.shape, fix3),
                  pl.BlockSpec(are.shape, fix3), pl.BlockSpec(aim.shape, fix3),
                  pl.BlockSpec((1, SSM_WIDTH), lambda i: (0, 0))],
        out_specs=pl.BlockSpec((rows, SSM_WIDTH), lambda i: (i, 0)),
        out_shape=jax.ShapeDtypeStruct((n, SSM_WIDTH), BF16),
        scratch_shapes=[pltpu.VMEM((2, 2, batch, SSM_HALF_STATE), F32),
                        pltpu.VMEM((rows, SSM_HALF_STATE), F32),
                        pltpu.VMEM((rows, SSM_HALF_STATE), F32)],
        compiler_params=_params("arbitrary"),
        name="s5_scan",
    )(u, bre, bim, cre, cim, are, aim, d)


def _moba_kernel(q_ref, k_ref, v_ref, o_ref, kpad_ref, vt_ref, km_ref, qt_ref, m_ref, alpha_ref, acc_ref,
                 s_ref, p_ref):
    i = pl.program_id(1)
    blk = MOBA_BLOCK
    nblk = k_ref.shape[0] // blk
    n_sel = min(MOBA_TOPK, nblk - 1)

    @pl.when(i == 0)
    def _build():
        lane = lax.broadcasted_iota(jnp.int32, (blk, LANES), 1)
        is_head_lane = lane < HEAD_DIM
        km_ref[...] = jnp.zeros_like(km_ref)
        ones_row = (lax.broadcasted_iota(jnp.int32, (V_ROWS - HEAD_DIM, blk), 0) == 0)

        def per_block(j, _):
            r0 = pl.multiple_of(j * blk, blk)
            vt = v_ref[pl.ds(r0, blk), :].astype(F32).T
            for h in range(N_HEADS):
                p = h // 2
                kk = k_ref[pl.ds(r0, blk), p * LANES:(p + 1) * LANES].astype(F32)
                if h % 2:
                    kk = pltpu.roll(kk, HEAD_DIM, 1)
                kk = jnp.where(is_head_lane, kk, 0.0)
                km_ref[h, pl.ds(j, 1), :] = jnp.mean(kk, axis=0, keepdims=True)
                kk = jnp.where(lane == HEAD_DIM + j, 1.0, kk)
                kpad_ref[j, :, h * LANES:(h + 1) * LANES] = kk.astype(BF16)
                vt_ref[j, h, 0:HEAD_DIM, :] = vt[h * HEAD_DIM:(h + 1) * HEAD_DIM, :].astype(BF16)
                vt_ref[j, h, HEAD_DIM:V_ROWS, :] = ones_row.astype(BF16)
            return 0

        lax.fori_loop(0, nblk, per_block, 0)

    qt = q_ref[...].astype(F32).T
    blk_row = lax.broadcasted_iota(jnp.int32, (GATE_ROWS, blk), 0)
    past = blk_row < i
    blk_row_f = blk_row.astype(F32)
    zeros_q = jnp.zeros((LANES - HEAD_DIM, blk), F32)
    zeros_pad = jnp.zeros((LANES - HEAD_DIM - GATE_ROWS, blk), F32)
    for h in range(N_HEADS):
        qh = qt[h * HEAD_DIM:(h + 1) * HEAD_DIM, :]
        q_pad = jnp.concatenate([qh, zeros_q], axis=0).astype(BF16)
        km = km_ref[h]
        km_hi = km.astype(BF16)
        km_lo = (km - km_hi.astype(F32)).astype(BF16)
        gate = jnp.where(past, _dot(km_hi, q_pad) + _dot(km_lo, q_pad), -jnp.inf)
        sel = jnp.zeros(gate.shape, jnp.bool_)
        for _ in range(n_sel):
            best = jnp.max(gate, axis=0, keepdims=True)
            cand = (gate == best) & (best > -jnp.inf)
            first = jnp.min(jnp.where(cand, blk_row_f, float(GATE_ROWS)), axis=0, keepdims=True)
            pick = blk_row_f == first
            sel = sel | pick
            gate = jnp.where(pick, -jnp.inf, gate)
        bias = jnp.where(past & ~sel, MASK_NEG, 0.0)
        qt_ref[h] = jnp.concatenate([qh, bias, zeros_pad], axis=0).astype(BF16)
        m_ref[h] = jnp.full((1, blk), -jnp.inf, F32)
        acc_ref[h] = jnp.zeros((V_ROWS, blk), F32)

    n_s, n_p = s_ref.shape[0], p_ref.shape[0]

    def scores(j, h, u):
        s_ref[u % n_s] = _dot(kpad_ref[j, :, h * LANES:(h + 1) * LANES], qt_ref[h])

    def softmax(h, u, mask):
        st = s_ref[u % n_s]
        if mask is not None:
            st = jnp.where(mask, st, MASK_NEG)
        m_old = m_ref[h]
        m_new = jnp.maximum(m_old, jnp.max(st, axis=0, keepdims=True))
        alpha_ref[h] = jnp.exp2(m_old - m_new)
        p_ref[u % n_p] = jnp.exp2((st - m_new).astype(BF16))
        m_ref[h] = m_new

    def weighted_values(j, h, u):
        acc_ref[h] = alpha_ref[h] * acc_ref[h] + _dot(vt_ref[j, h], p_ref[u % n_p])

    def kv_blocks(js, mask):
        units = [(j, h) for j in js for h in range(N_HEADS)]
        for step in range(len(units) + VALUES_LAG):
            if step < len(units):
                scores(*units[step], step)
            u = step - SOFTMAX_LAG
            if 0 <= u < len(units):
                softmax(units[u][1], u, mask)
            u = step - VALUES_LAG
            if 0 <= u < len(units):
                weighted_values(*units[u], u)

    def octet(g, _):
        kv_blocks([g * 8 + d for d in range(8)], None)
        return 0

    def quad(g, _):
        kv_blocks([g * 4 + d for d in range(4)], None)
        return 0

    n_octets = i // 8
    n_quads = i // 4
    lax.fori_loop(0, n_octets, octet, 0)
    lax.fori_loop(n_octets * 2, n_quads, quad, 0)

    @pl.when(i % 4 >= 2)
    def _():
        kv_blocks([n_quads * 4, n_quads * 4 + 1], None)

    @pl.when(i % 2 == 1)
    def _():
        kv_blocks([i - 1], None)

    key_pos = lax.broadcasted_iota(jnp.int32, (blk, blk), 0)
    qry_pos = lax.broadcasted_iota(jnp.int32, (blk, blk), 1)
    kv_blocks([i], key_pos <= qry_pos)
    outs = []
    for h in range(N_HEADS):
        acc = acc_ref[h]
        outs.append(acc[0:HEAD_DIM, :] / acc[HEAD_DIM:HEAD_DIM + 1, :])
    o_ref[...] = jnp.concatenate(outs, axis=0).T.astype(BF16)


def _moba(q, k, v, batch):
    n = q.shape[0]
    seq = n // batch
    tq = MOBA_BLOCK
    nblk = seq // MOBA_BLOCK
    return pl.pallas_call(
        _moba_kernel,
        grid=(batch, nblk),
        in_specs=[pl.BlockSpec((tq, ATTN_WIDTH), lambda b, i: (b * nblk + i, 0)),
                  pl.BlockSpec((seq, ATTN_WIDTH), lambda b, i: (b, 0)),
                  pl.BlockSpec((seq, ATTN_WIDTH), lambda b, i: (b, 0))],
        out_specs=pl.BlockSpec((tq, ATTN_WIDTH), lambda b, i: (b * nblk + i, 0)),
        out_shape=jax.ShapeDtypeStruct((n, ATTN_WIDTH), BF16),
        scratch_shapes=[pltpu.VMEM((nblk, MOBA_BLOCK, N_HEADS * LANES), BF16),
                        pltpu.VMEM((nblk, N_HEADS, V_ROWS, MOBA_BLOCK), BF16),
                        pltpu.VMEM((N_HEADS, GATE_ROWS, LANES), F32),
                        pltpu.VMEM((N_HEADS, LANES, MOBA_BLOCK), BF16),
                        pltpu.VMEM((N_HEADS, 1, MOBA_BLOCK), F32),
                        pltpu.VMEM((N_HEADS, 1, MOBA_BLOCK), F32),
                        pltpu.VMEM((N_HEADS, V_ROWS, MOBA_BLOCK), F32),
                        pltpu.VMEM((VALUES_LAG, MOBA_BLOCK, MOBA_BLOCK), F32),
                        pltpu.VMEM((VALUES_LAG - SOFTMAX_LAG + 1, MOBA_BLOCK, MOBA_BLOCK), BF16)],
        compiler_params=_params("parallel", "arbitrary"),
        name="moba_attention",
    )(q, k, v)


def _mix_kernel(z_ref, a_ref, g_ref, x_ref, wglu_ref, wattn_ref, wout_ref, g2_ref,
                wr_ref, rb_ref, xo_ref, lg_ref):
    glu = _dot(z_ref[...], wglu_ref[...])
    y_ssm = glu[:, :D_MODEL] * jax.nn.sigmoid(glu[:, D_MODEL:])
    y_attn = _dot(a_ref[...], wattn_ref[...])
    g = g_ref[...].astype(F32)
    mixed = jax.nn.sigmoid(g[:, :D_MODEL]) * y_ssm + jax.nn.sigmoid(g[:, D_MODEL:]) * y_attn
    x = x_ref[...] + _dot(mixed.astype(BF16), wout_ref[...])
    xo_ref[...] = x
    h2 = _rms(x, g2_ref[...])
    hi = h2.astype(BF16)
    lo = (h2 - hi.astype(F32)).astype(BF16)
    parts = _dot(hi, wr_ref[...]) + _dot(lo, wr_ref[...])
    lg_ref[...] = parts[:, :ROUTER_LANES] + parts[:, ROUTER_LANES:] + rb_ref[...]


def _mix(z, a, g, x, wglu, wattn, wout, g2, wr, rb):
    n = x.shape[0]
    tm = TOKEN_TILE
    row = lambda i: (i, 0)
    fix = lambda i: (0, 0)
    full = lambda t: pl.BlockSpec(t.shape, fix)
    return pl.pallas_call(
        _mix_kernel,
        grid=(n // tm,),
        in_specs=[pl.BlockSpec((tm, SSM_WIDTH), row), pl.BlockSpec((tm, ATTN_WIDTH), row),
                  pl.BlockSpec((tm, 2 * D_MODEL), row), pl.BlockSpec((tm, D_MODEL), row),
                  full(wglu), full(wattn), full(wout), full(g2), full(wr), full(rb)],
        out_specs=[pl.BlockSpec((tm, D_MODEL), row), pl.BlockSpec((tm, ROUTER_LANES), row)],
        out_shape=[jax.ShapeDtypeStruct((n, D_MODEL), F32),
                   jax.ShapeDtypeStruct((n, ROUTER_LANES), F32)],
        compiler_params=_params("parallel"),
        name="mix_out_router",
    )(z, a, g, x, wglu, wattn, wout, g2, wr, rb)


def _route_kernel(lg_ref, eid_ref, rank_ref, w_ref, cnt_ref, carry_ref):
    @pl.when(pl.program_id(0) == 0)
    def _():
        carry_ref[...] = jnp.zeros_like(carry_ref)

    lg = lg_ref[...]
    t = lg.shape[0]
    lane = lax.broadcasted_iota(jnp.int32, (t, LANES), 1).astype(F32)

    def first_lane(hit):
        return jnp.min(jnp.where(hit, lane, float(LANES)), axis=1, keepdims=True)

    g_log = jnp.where(lane < N_GROUPS, lg, -jnp.inf)
    g_max = jnp.max(g_log, axis=1, keepdims=True)
    g_top = first_lane(g_log == g_max)
    p_g = 1.0 / jnp.sum(jnp.exp(g_log - g_max), axis=1, keepdims=True)

    lo = N_GROUPS + g_top * EXPERTS_PER_GROUP
    in_group = (lane >= lo) & (lane < lo + EXPERTS_PER_GROUP)
    e_log = jnp.where(in_group, lg, -jnp.inf)
    e_exp = jnp.exp(e_log - jnp.max(e_log, axis=1, keepdims=True))
    prob = jnp.where(in_group, e_exp / jnp.sum(e_exp, axis=1, keepdims=True), -1.0)
    p1 = jnp.max(prob, axis=1, keepdims=True)
    l1 = first_lane(prob == p1)
    prob = jnp.where(lane == l1, -1.0, prob)
    p2 = jnp.max(prob, axis=1, keepdims=True)
    l2 = first_lane(prob == p2)
    w_ref[...] = jnp.concatenate([p_g * (p1 / (p1 + p2)), p_g * (p2 / (p1 + p2))], axis=1)
    eid = [l1 - N_GROUPS, l2 - N_GROUPS]
    eid_ref[...] = jnp.concatenate(eid, axis=1).astype(jnp.int32)

    earlier = (lax.broadcasted_iota(jnp.int32, (t, t), 1)
               < lax.broadcasted_iota(jnp.int32, (t, t), 0)).astype(BF16)
    carry = carry_ref[...]
    ranks = []
    for e in eid:
        hit = lane == e
        onehot = hit.astype(F32)
        before = _dot(earlier, onehot.astype(BF16)) + carry
        ranks.append(jnp.sum(jnp.where(hit, before, 0.0), axis=1, keepdims=True))
        carry = carry + jnp.sum(onehot, axis=0, keepdims=True)
    carry_ref[...] = carry
    rank_ref[...] = jnp.concatenate(ranks, axis=1).astype(jnp.int32)
    cnt_ref[...] = carry.astype(jnp.int32)


def _route_rank(logits):
    n = logits.shape[0]
    t = TOKEN_TILE
    pair = pl.BlockSpec((t, EXPERT_TOPK), lambda i: (i, 0))
    return pl.pallas_call(
        _route_kernel,
        grid=(n // t,),
        in_specs=[pl.BlockSpec((t, ROUTER_LANES), lambda i: (i, 0))],
        out_specs=[pair, pair, pair, pl.BlockSpec((1, LANES), lambda i: (0, 0))],
        out_shape=[jax.ShapeDtypeStruct((n, EXPERT_TOPK), jnp.int32),
                   jax.ShapeDtypeStruct((n, EXPERT_TOPK), jnp.int32),
                   jax.ShapeDtypeStruct((n, EXPERT_TOPK), F32),
                   jax.ShapeDtypeStruct((1, LANES), jnp.int32)],
        scratch_shapes=[pltpu.VMEM((1, LANES), F32)],
        compiler_params=_params("arbitrary"),
        name="route_rank",
    )(logits)


def _store_row_tiles(ref, value):
    rows, half = value.shape[0], value.shape[1] // 2
    as_bits = lambda v: lax.bitcast_convert_type(v.astype(BF16).astype(F32), U32)
    words = as_bits(value[:, half:]) | (as_bits(value[:, :half]) >> 16)
    for c in range(ROW_TILES):
        ref[pl.ds(c, rows, stride=ROW_TILES), :] = words[:, c * LANES:(c + 1) * LANES]


def _load_row_tiles(ref, start, rows):
    words = jnp.concatenate(
        [ref[pl.ds(start * ROW_TILES + c, rows, stride=ROW_TILES), :] for c in range(ROW_TILES)], axis=1)
    low = lax.bitcast_convert_type(words << 16, F32)
    high = lax.bitcast_convert_type(words & jnp.uint32(0xFFFF0000), F32)
    return jnp.concatenate([low, high], axis=1)


def _row_dma(src, src_row8, dst, dst_row8, sem, rows=1):
    n = rows * ROW_TILES
    return pltpu.make_async_copy(src.at[pl.ds(src_row8, n), :], dst.at[pl.ds(dst_row8, n), :], sem)


def _wait_rows(src, dst, sem, count, rows=1):
    def body(r, _):
        _row_dma(src, 0, dst, 0, sem, rows).wait()
        return 0
    lax.fori_loop(0, count, body, 0, unroll=8 if isinstance(count, int) else 1)


def _dispatch_kernel(dest_ref, pad0_ref, npad_ref, x_ref, g2_ref, xg_hbm, hbuf, zrow, sem, zsem):
    s = pl.program_id(0)
    last = pl.num_programs(0) - 1
    slot = s % 2
    tt = x_ref.shape[0]
    rows = EXPERT_TOPK * tt

    @pl.when(s == 0)
    def _():
        zrow[...] = jnp.zeros_like(zrow)

        def tile_of(row):
            return pl.multiple_of(row * ROW_TILES, ROW_TILES)

        def per_expert(e, total):
            def fill(r, _):
                _row_dma(zrow, 0, xg_hbm, tile_of(pad0_ref[e] + r), zsem).start()
                return 0
            lax.fori_loop(0, npad_ref[e], fill, 0)
            return total + npad_ref[e]
        total = lax.fori_loop(0, N_EXPERTS, per_expert, 0)
        _wait_rows(zrow, xg_hbm, zsem, total)

        def fill_unused(r, _):
            row = pad0_ref[N_EXPERTS] + r * SUBLANES
            _row_dma(zrow, 0, xg_hbm, tile_of(row), zsem, SUBLANES).start()
            return 0
        lax.fori_loop(0, npad_ref[N_EXPERTS], fill_unused, 0)
        _wait_rows(zrow, xg_hbm, zsem, npad_ref[N_EXPERTS], SUBLANES)

    src = hbuf.at[slot]
    _store_row_tiles(src, _rms(x_ref[...], g2_ref[...]))
    for r in range(rows):
        dst_row8 = pl.multiple_of(dest_ref[s * rows + r], ROW_TILES)
        _row_dma(src, (r % tt) * ROW_TILES, xg_hbm, dst_row8, sem.at[slot]).start(priority=r % 2)

    @pl.when(s > 0)
    def _():
        _wait_rows(hbuf.at[1 - slot], xg_hbm, sem.at[1 - slot], rows)

    @pl.when(s == last)
    def _():
        _wait_rows(src, xg_hbm, sem.at[slot], rows)


def _dispatch(dest, pad0, npad, x, g2, cap):
    n = x.shape[0]
    tt = COMBINE_TILE
    grid_spec = pltpu.PrefetchScalarGridSpec(
        num_scalar_prefetch=3,
        grid=(n // tt,),
        in_specs=[pl.BlockSpec((tt, D_MODEL), lambda s, d, p, c: (s, 0)),
                  pl.BlockSpec((1, D_MODEL), lambda s, d, p, c: (0, 0))],
        out_specs=pl.BlockSpec(memory_space=pl.ANY),
        scratch_shapes=[pltpu.VMEM((2, tt * ROW_TILES, LANES), U32),
                        pltpu.VMEM((SUBLANES * ROW_TILES, LANES), U32),
                        pltpu.SemaphoreType.DMA((2,)),
                        pltpu.SemaphoreType.DMA],
    )
    return pl.pallas_call(
        _dispatch_kernel,
        grid_spec=grid_spec,
        out_shape=jax.ShapeDtypeStruct((cap * ROW_TILES, LANES), U32),
        compiler_params=_params("arbitrary"),
        name="expert_dispatch",
    )(dest, pad0, npad, x, g2)


def _expert_kernel(blk_e_ref, nused_ref, xg_ref, wgu_ref, wd_ref, y_ref, wgu_bf, wd_bf):
    b = pl.program_id(0)
    prev = blk_e_ref[jnp.maximum(b, 1) - 1]

    @pl.when((b == 0) | (blk_e_ref[b] != prev))
    def _():
        wgu_bf[...] = wgu_ref[0, 0].astype(BF16)
        wd_bf[...] = wd_ref[0, 0].astype(BF16)

    @pl.when(b < nused_ref[0])
    def _():
        gu = _dot(_load_row_tiles(xg_ref, 0, EXPERT_ROWS).astype(BF16), wgu_bf[...])
        act = jax.nn.silu(gu[:, :D_EXPERT]) * gu[:, D_EXPERT:]
        _store_row_tiles(y_ref, _dot(act.astype(BF16), wd_bf[...]))

    @pl.when(b >= nused_ref[0])
    def _():
        y_ref[...] = jnp.zeros_like(y_ref)


def _experts(blk_e, nused, xg, w_gate_up, w_down, layer):
    rows = EXPERT_ROWS * ROW_TILES
    nblk = xg.shape[0] // rows
    used = lambda b, n: jnp.minimum(b, n[0] - 1)
    grid_spec = pltpu.PrefetchScalarGridSpec(
        num_scalar_prefetch=2,
        grid=(nblk,),
        in_specs=[pl.BlockSpec((rows, LANES), lambda b, e, n: (used(b, n), 0)),
                  pl.BlockSpec((1, 1, D_MODEL, 2 * D_EXPERT), lambda b, e, n: (layer, e[b], 0, 0)),
                  pl.BlockSpec((1, 1, D_EXPERT, D_MODEL), lambda b, e, n: (layer, e[b], 0, 0))],
        out_specs=pl.BlockSpec((rows, LANES), lambda b, e, n: (b, 0)),
        scratch_shapes=[pltpu.VMEM((D_MODEL, 2 * D_EXPERT), BF16),
                        pltpu.VMEM((D_EXPERT, D_MODEL), BF16)],
    )
    return pl.pallas_call(
        _expert_kernel,
        grid_spec=grid_spec,
        out_shape=jax.ShapeDtypeStruct((nblk * rows, LANES), U32),
        compiler_params=_params("arbitrary"),
        name="experts",
    )(blk_e, nused, xg, w_gate_up, w_down)


def _combine_kernel(pos_ref, x_ref, w_ref, y_hbm, o_ref, ybuf, sem):
    s = pl.program_id(0)
    nsteps = pl.num_programs(0)
    slot = s % 2
    tt = COMBINE_TILE
    rows = EXPERT_TOPK * tt

    def issue(step, slot):
        dst = ybuf.at[slot]
        for r in range(rows):
            src_row8 = pl.multiple_of(pos_ref[step * rows + r], ROW_TILES)
            _row_dma(y_hbm, src_row8, dst, r * ROW_TILES, sem.at[slot]).start(priority=r % 2)

    @pl.when(s == 0)
    def _():
        issue(0, 0)

    @pl.when(s + 1 < nsteps)
    def _():
        issue(s + 1, 1 - slot)

    _wait_rows(y_hbm, ybuf.at[slot], sem.at[slot], rows)
    w = w_ref[...]
    got = ybuf.at[slot]
    o_ref[...] = (x_ref[...] + w[:, 0:1] * _load_row_tiles(got, 0, tt)
                  + w[:, 1:2] * _load_row_tiles(got, tt, tt))


def _combine(pos, x, w, y):
    n = x.shape[0]
    tt = COMBINE_TILE
    grid_spec = pltpu.PrefetchScalarGridSpec(
        num_scalar_prefetch=1,
        grid=(n // tt,),
        in_specs=[pl.BlockSpec((tt, D_MODEL), lambda s, p: (s, 0)),
                  pl.BlockSpec((tt, EXPERT_TOPK), lambda s, p: (s, 0)),
                  pl.BlockSpec(memory_space=pl.ANY)],
        out_specs=pl.BlockSpec((tt, D_MODEL), lambda s, p: (s, 0)),
        scratch_shapes=[pltpu.VMEM((2, EXPERT_TOPK * tt * ROW_TILES, LANES), U32),
                        pltpu.SemaphoreType.DMA((2,))],
    )
    return pl.pallas_call(
        _combine_kernel,
        grid_spec=grid_spec,
        out_shape=jax.ShapeDtypeStruct((n, D_MODEL), F32),
        compiler_params=_params("arbitrary"),
        name="moe_combine",
    )(pos, x, w, y)


def _rope_tables(seq):
    inv = ROPE_THETA ** (-jnp.arange(0, HEAD_DIM, 2, dtype=F32) / HEAD_DIM)
    ang = jnp.arange(seq, dtype=F32)[:, None] * inv[None, :]
    cos, sin = jnp.cos(ang), jnp.sin(ang)
    cos = jnp.concatenate([cos, cos] * (LANES // HEAD_DIM), axis=1)
    sin = jnp.concatenate([-sin, sin] * (LANES // HEAD_DIM), axis=1)
    return cos, sin


def _ssm_weights(a_re, a_im, log_dt, b_re, b_im, c_re, c_im):
    dt = jnp.exp(log_dt)[:, None]
    decay = jnp.exp(a_re * dt)
    abar_re = decay * jnp.cos(a_im * dt)
    abar_im = decay * jnp.sin(a_im * dt)
    den = a_re * a_re + a_im * a_im
    num_re = abar_re - 1.0
    f_re = (num_re * a_re + abar_im * a_im) / den
    f_im = (abar_im * a_re - num_re * a_im) / den
    bbar_re = f_re[..., None] * b_re - f_im[..., None] * b_im
    bbar_im = f_re[..., None] * b_im + f_im[..., None] * b_re
    gh = SSM_GROUPS // 2
    eye = jnp.eye(gh, dtype=F32)

    def b_mat(t):
        t = t.reshape(2, gh, SSM_STATE, SSM_GROUP_CH)
        return jnp.einsum('cgph,gk->cghkp', t, eye).reshape(2, SSM_HALF, SSM_HALF_STATE).astype(BF16)

    def c_mat(t):
        t = t.reshape(2, gh, SSM_GROUP_CH, SSM_STATE)
        return jnp.einsum('cghp,gk->cgpkh', t, eye).reshape(2, SSM_HALF_STATE, SSM_HALF).astype(BF16)

    a_vec = lambda t: t.reshape(2, 1, SSM_HALF_STATE)
    return (b_mat(bbar_re), b_mat(bbar_im), c_mat(c_re), c_mat(-c_im), a_vec(abar_re), a_vec(abar_im))


def _route(logits, n):
    expert_id, rank, weights, cnt = _route_rank(logits)
    counts = cnt[0, :N_EXPERTS]
    padded = (counts + EXPERT_ROWS - 1) // EXPERT_ROWS * EXPERT_ROWS
    pend = jnp.cumsum(padded)
    pstart = pend - padded
    experts = jnp.arange(N_EXPERTS, dtype=jnp.int32)
    start_of = jnp.sum(jnp.where(expert_id[..., None] == experts, pstart, 0), axis=-1)
    dest = (start_of + rank).astype(jnp.int32)
    cap = n * EXPERT_TOPK + N_EXPERTS * EXPERT_ROWS
    nblk = cap // EXPERT_ROWS
    blk_row = jnp.arange(nblk, dtype=jnp.int32)[:, None] * EXPERT_ROWS
    blk_e = jnp.minimum(jnp.sum((pend[None, :] <= blk_row).astype(jnp.int32), axis=1), N_EXPERTS - 1)
    nused = (pend[-1:] // EXPERT_ROWS).astype(jnp.int32)
    pos = dest.reshape(n // COMBINE_TILE, COMBINE_TILE, EXPERT_TOPK).transpose(0, 2, 1).reshape(-1)
    pos = pos * ROW_TILES
    pad0 = jnp.concatenate([pstart + counts, pend[-1:]]).astype(jnp.int32)
    npad = jnp.concatenate([padded - counts, (cap - pend[-1:]) // SUBLANES]).astype(jnp.int32)
    return weights, pos, pad0, npad, blk_e, nused, cap


def kernel(x, norm1_g, w_in, ssm_a_re, ssm_a_im, ssm_log_dt, ssm_b_re, ssm_b_im, ssm_c_re, ssm_c_im,
           ssm_d, w_glu, q_norm_g, k_norm_g, w_attn, w_out, norm2_g, router_w_group, router_b_group,
           router_w_expert, router_b_expert, w_gate_up, w_down):
    batch, seq, _ = x.shape
    depth = w_in.shape[0]
    n = batch * seq
    assert batch == SUBLANES and seq % MOBA_BLOCK == 0 and n % TOKEN_TILE == 0
    assert seq // MOBA_BLOCK <= GATE_ROWS and seq % SSM_STEPS == 0

    xt = x.reshape(n, D_MODEL)
    cos, sin = _rope_tables(seq)
    time_major = lambda t: t.reshape(batch, seq, -1).transpose(1, 0, 2).reshape(n, -1)
    batch_major = lambda t: t.reshape(seq, batch, -1).transpose(1, 0, 2).reshape(n, -1)
    idx = jnp.arange(ATTN_WIDTH)
    gsum =(idx[:, None] // HEAD_DIM == idx[None, :] // HEAD_DIM).astype(BF16)
    row = lambda t: t.reshape(1, -1).astype(F32)

    for l in range(depth):
        qg = row(jnp.tile(q_norm_g[l], N_HEADS)) * (HEAD_DIM ** -0.5 * math.log2(math.e))
        kg = row(jnp.tile(k_norm_g[l], N_HEADS))
        u, q, k, v, g = _in_proj(xt, row(norm1_g[l]), w_in[l].astype(BF16), gsum, qg, kg, cos, sin)
        ssm_w = _ssm_weights(ssm_a_re[l], ssm_a_im[l], ssm_log_dt[l], ssm_b_re[l], ssm_b_im[l],
                             ssm_c_re[l], ssm_c_im[l])
        z = batch_major(_ssm(time_major(u), *ssm_w, row(ssm_d[l]), batch))
        attn = _moba(q, k, v, batch)

        w_r = jnp.concatenate([router_w_group[l], router_w_expert[l]], axis=1)
        w_r = jnp.pad(w_r, ((0, 0), (0, ROUTER_LANES - w_r.shape[1])))
        w_rh = w_r.astype(BF16)
        w_r = jnp.concatenate([w_rh, (w_r - w_rh.astype(F32)).astype(BF16)], axis=1)
        b_r = jnp.concatenate([router_b_group[l], router_b_expert[l]])
        b_r = row(jnp.pad(b_r, (0, ROUTER_LANES - b_r.shape[0])))
        g2 = row(norm2_g[l])
        xt, logits = _mix(z, attn, g, xt, w_glu[l].astype(BF16), w_attn[l].astype(BF16),
                          w_out[l].astype(BF16), g2, w_r, b_r)

        weights, pos, pad0, npad, blk_e, nused, cap = _route(logits, n)
        xg = _dispatch(pos, pad0, npad, xt, g2, cap)
        y = _experts(blk_e, nused, xg, w_gate_up, w_down, l)
        xt = _combine(pos, xt, weights, y)

    return xt.reshape(batch, seq, D_MODEL)
```

```python
import functools
import math

import jax
import jax.numpy as jnp
from jax import lax
from jax.experimental import pallas as pl
from jax.experimental.pallas import tpu as pltpu

F32 = jnp.float32
BF16 = jnp.bfloat16
U32 = jnp.uint32

D_MODEL = 1024
SSM_GROUPS = 32
SSM_GROUP_CH = 16
SSM_WIDTH = SSM_GROUPS * SSM_GROUP_CH
SSM_STATE = 64
N_HEADS = 8
HEAD_DIM = 64
ATTN_WIDTH = N_HEADS * HEAD_DIM
MOBA_BLOCK = 256
MOBA_TOPK = 3
ROPE_THETA = 10000.0
D_IN = SSM_WIDTH + 3 * ATTN_WIDTH + 2 * D_MODEL
N_GROUPS = 4
EXPERTS_PER_GROUP = 8
N_EXPERTS = N_GROUPS * EXPERTS_PER_GROUP
EXPERT_TOPK = 2
D_EXPERT = 512
EXPERT_ROWS = 256
NORM_EPS = 1e-6

LANES = 128
SUBLANES = 8
VMEM_LIMIT = 56 * 1024 * 1024
MASK_NEG = -1e30

TOKEN_TILE = 512
SSM_STEPS = 64
SSM_HALF = SSM_WIDTH // 2
SSM_HALF_STATE = SSM_GROUPS * SSM_STATE // 2
GATE_ROWS = 16
V_ROWS = HEAD_DIM + 16
SOFTMAX_LAG = 3
VALUES_LAG = 6
ROUTER_LANES = 128
COMBINE_TILE = 256
ROW_TILES = D_MODEL // (2 * LANES)


def _params(*sem):
    return pltpu.CompilerParams(dimension_semantics=sem, vmem_limit_bytes=VMEM_LIMIT)


def _dot(a, b):
    return jnp.dot(a, b, preferred_element_type=F32)


def _rms(x, gain):
    return x * lax.rsqrt(jnp.mean(x * x, axis=-1, keepdims=True) + NORM_EPS) * gain


def _in_proj_kernel(x_ref, g1_ref, w_ref, gs_ref, qg_ref, kg_ref, cos_ref, sin_ref,
                    u_ref, q_ref, k_ref, v_ref, g_ref):
    hb = _rms(x_ref[...], g1_ref[...]).astype(BF16)

    def seg(lo, hi):
        return _dot(hb, w_ref[:, lo:hi])

    o_q = SSM_WIDTH
    o_k = o_q + ATTN_WIDTH
    o_v = o_k + ATTN_WIDTH
    o_g = o_v + ATTN_WIDTH
    u_ref[...] = seg(0, o_q).astype(BF16)
    v_ref[...] = seg(o_v, o_g).astype(BF16)
    g_ref[...] = seg(o_g, D_IN).astype(BF16)

    reps = ATTN_WIDTH // LANES
    cos = jnp.concatenate([cos_ref[...]] * reps, axis=1)
    sin = jnp.concatenate([sin_ref[...]] * reps, axis=1)
    lane = lax.broadcasted_iota(jnp.int32, cos.shape, 1)
    first_half = (lane % HEAD_DIM) < (HEAD_DIM // 2)

    def norm_rope(t, gain):
        ss = _dot((t * t).astype(BF16), gs_ref[...])
        tn = t * lax.rsqrt(ss * (1.0 / HEAD_DIM) + NORM_EPS) * gain
        partner = jnp.where(first_half,
                            pltpu.roll(tn, ATTN_WIDTH - HEAD_DIM // 2, 1),
                            pltpu.roll(tn, HEAD_DIM // 2, 1))
        return tn * cos + partner * sin

    q_ref[...] = norm_rope(seg(o_q, o_k), qg_ref[...]).astype(BF16)
    k_ref[...] = norm_rope(seg(o_k, o_v), kg_ref[...]).astype(BF16)


def _in_proj(x, g1, w_in, gsum, qg, kg, cos, sin):
    n = x.shape[0]
    tm = TOKEN_TILE
    row = lambda i: (i, 0)
    fix = lambda i: (0, 0)
    seq = cos.shape[0]
    tiles_per_seq = seq // tm
    pos = lambda i: (i % tiles_per_seq, 0)
    outs = [jax.ShapeDtypeStruct((n, w), BF16)
            for w in (SSM_WIDTH, ATTN_WIDTH, ATTN_WIDTH, ATTN_WIDTH, 2 * D_MODEL)]
    return pl.pallas_call(
        _in_proj_kernel,
        grid=(n // tm,),
        in_specs=[pl.BlockSpec((tm, D_MODEL), row),
                  pl.BlockSpec((1, D_MODEL), fix),
                  pl.BlockSpec((D_MODEL, D_IN), fix),
                  pl.BlockSpec((ATTN_WIDTH, ATTN_WIDTH), fix),
                  pl.BlockSpec((1, ATTN_WIDTH), fix),
                  pl.BlockSpec((1, ATTN_WIDTH), fix),
                  pl.BlockSpec((tm, LANES), pos),
                  pl.BlockSpec((tm, LANES), pos)],
        out_specs=[pl.BlockSpec((tm, o.shape[1]), row) for o in outs],
        out_shape=outs,
        compiler_params=_params("parallel"),
        name="in_proj",
    )(x, g1, w_in, gsum, qg, kg, cos, sin)


def _ssm_kernel(u_ref, bre_ref, bim_ref, cre_ref, cim_ref, are_ref, aim_ref, d_ref,
                z_ref, h_ref, sre_ref, sim_ref, *, batch):
    @pl.when(pl.program_id(0) == 0)
    def _():
        h_ref[...] = jnp.zeros_like(h_ref)

    steps = u_ref.shape[0] // batch
    ys = []
    for c in range(2):
        uc = u_ref[:, c * SSM_HALF:(c + 1) * SSM_HALF]
        sre_ref[...] = _dot(uc, bre_ref[c])
        sim_ref[...] = _dot(uc, bim_ref[c])
        a_re = jnp.broadcast_to(are_ref[c], (batch, SSM_HALF_STATE))
        a_im = jnp.broadcast_to(aim_ref[c], (batch, SSM_HALF_STATE))

        def step(t, carry):
            h_re, h_im = carry
            r0 = pl.multiple_of(t * batch, batch)
            n_re = a_re * h_re - a_im * h_im + sre_ref[pl.ds(r0, batch), :]
            n_im = a_re * h_im + a_im * h_re + sim_ref[pl.ds(r0, batch), :]
            sre_ref[pl.ds(r0, batch), :] = n_re
            sim_ref[pl.ds(r0, batch), :] = n_im
            return n_re, n_im

        h_re, h_im = lax.fori_loop(0, steps, step, (h_ref[c, 0], h_ref[c, 1]), unroll=True)
        h_ref[c, 0] = h_re
        h_ref[c, 1] = h_im
        ys.append(_dot(sre_ref[...].astype(BF16), cre_ref[c])
                  + _dot(sim_ref[...].astype(BF16), cim_ref[c]))
    y = jnp.concatenate(ys, axis=1) + d_ref[...] * u_ref[...].astype(F32)
    z_ref[...] = jax.nn.gelu(y).astype(BF16)


def _ssm(u, bre, bim, cre, cim, are, aim, d, batch):
    n = u.shape[0]
    rows = SSM_STEPS * batch
    fix3 = lambda i: (0, 0, 0)
    return pl.pallas_call(
        functools.partial(_ssm_kernel, batch=batch),
        grid=(n // rows,),
        in_specs=[pl.BlockSpec((rows, SSM_WIDTH), lambda i: (i, 0)),
                  pl.BlockSpec(bre.shape, fix3), pl.BlockSpec(bim.shape, fix3),
                  pl.BlockSpec(cre.shape, fix3), pl.BlockSpec(cim.shape, fix3),
                  pl.BlockSpec(are.shape, fix3), pl.BlockSpec(aim.shape, fix3),
                  pl.BlockSpec((1, SSM_WIDTH), lambda i: (0, 0))],
        out_specs=pl.BlockSpec((rows, SSM_WIDTH), lambda i: (i, 0)),
        out_shape=jax.ShapeDtypeStruct((n, SSM_WIDTH), BF16),
        scratch_shapes=[pltpu.VMEM((2, 2, batch, SSM_HALF_STATE), F32),
                        pltpu.VMEM((rows, SSM_HALF_STATE), F32),
                        pltpu.VMEM((rows, SSM_HALF_STATE), F32)],
        compiler_params=_params("arbitrary"),
        name="s5_scan",
    )(u, bre, bim, cre, cim, are, aim, d)


def _moba_kernel(q_ref, k_ref, v_ref, o_ref, kpad_ref, vt_ref, km_ref, qt_ref, m_ref, alpha_ref, acc_ref,
                 s_ref, p_ref):
    i = pl.program_id(1)
    blk = MOBA_BLOCK
    nblk = k_ref.shape[0] // blk
    n_sel = min(MOBA_TOPK, nblk - 1)

    @pl.when(i == 0)
    def _build():
        lane = lax.broadcasted_iota(jnp.int32, (blk, LANES), 1)
        is_head_lane = lane < HEAD_DIM
        km_ref[...] = jnp.zeros_like(km_ref)
        ones_row = (lax.broadcasted_iota(jnp.int32, (V_ROWS - HEAD_DIM, blk), 0) == 0)

        def per_block(j, _):
            r0 = pl.multiple_of(j * blk, blk)
            vt = v_ref[pl.ds(r0, blk), :].astype(F32).T
            for h in range(N_HEADS):
                p = h // 2
                kk = k_ref[pl.ds(r0, blk), p * LANES:(p + 1) * LANES].astype(F32)
                if h % 2:
                    kk = pltpu.roll(kk, HEAD_DIM, 1)
                kk = jnp.where(is_head_lane, kk, 0.0)
                km_ref[h, pl.ds(j, 1), :] = jnp.mean(kk, axis=0, keepdims=True)
                kk = jnp.where(lane == HEAD_DIM + j, 1.0, kk)
                kpad_ref[j, :, h * LANES:(h + 1) * LANES] = kk.astype(BF16)
                vt_ref[j, h, 0:HEAD_DIM, :] = vt[h * HEAD_DIM:(h + 1) * HEAD_DIM, :].astype(BF16)
                vt_ref[j, h, HEAD_DIM:V_ROWS, :] = ones_row.astype(BF16)
            return 0

        lax.fori_loop(0, nblk, per_block, 0)

    qt = q_ref[...].astype(F32).T
    blk_row = lax.broadcasted_iota(jnp.int32, (GATE_ROWS, blk), 0)
    past = blk_row < i
    blk_row_f = blk_row.astype(F32)
    zeros_q = jnp.zeros((LANES - HEAD_DIM, blk), F32)
    zeros_pad = jnp.zeros((LANES - HEAD_DIM - GATE_ROWS, blk), F32)
    for h in range(N_HEADS):
        qh = qt[h * HEAD_DIM:(h + 1) * HEAD_DIM, :]
        q_pad = jnp.concatenate([qh, zeros_q], axis=0).astype(BF16)
        km = km_ref[h]
        km_hi = km.astype(BF16)
        km_lo = (km - km_hi.astype(F32)).astype(BF16)
        gate = jnp.where(past, _dot(km_hi, q_pad) + _dot(km_lo, q_pad), -jnp.inf)
        sel = jnp.zeros(gate.shape, jnp.bool_)
        for _ in range(n_sel):
            best = jnp.max(gate, axis=0, keepdims=True)
            cand = (gate == best) & (best > -jnp.inf)
            first = jnp.min(jnp.where(cand, blk_row_f, float(GATE_ROWS)), axis=0, keepdims=True)
            pick = blk_row_f == first
            sel = sel | pick
            gate = jnp.where(pick, -jnp.inf, gate)
        bias = jnp.where(past & ~sel, MASK_NEG, 0.0)
        qt_ref[h] = jnp.concatenate([qh, bias, zeros_pad], axis=0).astype(BF16)
        m_ref[h] = jnp.full((1, blk), -jnp.inf, F32)
        acc_ref[h] = jnp.zeros((V_ROWS, blk), F32)

    n_s, n_p = s_ref.shape[0], p_ref.shape[0]

    def scores(j, h, u):
        s_ref[u % n_s] = _dot(kpad_ref[j, :, h * LANES:(h + 1) * LANES], qt_ref[h])

    def softmax(h, u, mask):
        st = s_ref[u % n_s]
        if mask is not None:
            st = jnp.where(mask, st, MASK_NEG)
        m_old = m_ref[h]
        m_new = jnp.maximum(m_old, jnp.max(st, axis=0, keepdims=True))
        alpha_ref[h] = jnp.exp2(m_old - m_new)
        p_ref[u % n_p] = jnp.exp2((st - m_new).astype(BF16))
        m_ref[h] = m_new

    def weighted_values(j, h, u):
        acc_ref[h] = alpha_ref[h] * acc_ref[h] + _dot(vt_ref[j, h], p_ref[u % n_p])

    def kv_blocks(js, mask):
        units = [(j, h) for j in js for h in range(N_HEADS)]
        for step in range(len(units) + VALUES_LAG):
            if step < len(units):
                scores(*units[step], step)
            u = step - SOFTMAX_LAG
            if 0 <= u < len(units):
                softmax(units[u][1], u, mask)
            u = step - VALUES_LAG
            if 0 <= u < len(units):
                weighted_values(*units[u], u)

    def octet(g, _):
        kv_blocks([g * 8 + d for d in range(8)], None)
        return 0

    def quad(g, _):
        kv_blocks([g * 4 + d for d in range(4)], None)
        return 0

    n_octets = i // 8
    n_quads = i // 4
    lax.fori_loop(0, n_octets, octet, 0)
    lax.fori_loop(n_octets * 2, n_quads, quad, 0)

    @pl.when(i % 4 >= 2)
    def _():
        kv_blocks([n_quads * 4, n_quads * 4 + 1], None)

    @pl.when(i % 2 == 1)
    def _():
        kv_blocks([i - 1], None)

    key_pos = lax.broadcasted_iota(jnp.int32, (blk, blk), 0)
    qry_pos = lax.broadcasted_iota(jnp.int32, (blk, blk), 1)
    kv_blocks([i], key_pos <= qry_pos)
    outs = []
    for h in range(N_HEADS):
        acc = acc_ref[h]
        outs.append(acc[0:HEAD_DIM, :] / acc[HEAD_DIM:HEAD_DIM + 1, :])
    o_ref[...] = jnp.concatenate(outs, axis=0).T.astype(BF16)


def _moba(q, k, v, batch):
    n = q.shape[0]
    seq = n // batch
    tq = MOBA_BLOCK
    nblk = seq // MOBA_BLOCK
    return pl.pallas_call(
        _moba_kernel,
        grid=(batch, nblk),
        in_specs=[pl.BlockSpec((tq, ATTN_WIDTH), lambda b, i: (b * nblk + i, 0)),
                  pl.BlockSpec((seq, ATTN_WIDTH), lambda b, i: (b, 0)),
                  pl.BlockSpec((seq, ATTN_WIDTH), lambda b, i: (b, 0))],
        out_specs=pl.BlockSpec((tq, ATTN_WIDTH), lambda b, i: (b * nblk + i, 0)),
        out_shape=jax.ShapeDtypeStruct((n, ATTN_WIDTH), BF16),
        scratch_shapes=[pltpu.VMEM((nblk, MOBA_BLOCK, N_HEADS * LANES), BF16),
                        pltpu.VMEM((nblk, N_HEADS, V_ROWS, MOBA_BLOCK), BF16),
                        pltpu.VMEM((N_HEADS, GATE_ROWS, LANES), F32),
                        pltpu.VMEM((N_HEADS, LANES, MOBA_BLOCK), BF16),
                        pltpu.VMEM((N_HEADS, 1, MOBA_BLOCK), F32),
                        pltpu.VMEM((N_HEADS, 1, MOBA_BLOCK), F32),
                        pltpu.VMEM((N_HEADS, V_ROWS, MOBA_BLOCK), F32),
                        pltpu.VMEM((VALUES_LAG, MOBA_BLOCK, MOBA_BLOCK), F32),
                        pltpu.VMEM((VALUES_LAG - SOFTMAX_LAG + 1, MOBA_BLOCK, MOBA_BLOCK), BF16)],
        compiler_params=_params("parallel", "arbitrary"),
        name="moba_attention",
    )(q, k, v)


def _mix_kernel(z_ref, a_ref, g_ref, x_ref, wglu_ref, wattn_ref, wout_ref, g2_ref,
                wrh_ref, wrl_ref, rb_ref, xo_ref, lg_ref):
    glu = _dot(z_ref[...], wglu_ref[...])
    y_ssm = glu[:, :D_MODEL] * jax.nn.sigmoid(glu[:, D_MODEL:])
    y_attn = _dot(a_ref[...], wattn_ref[...])
    g = g_ref[...].astype(F32)
    mixed = jax.nn.sigmoid(g[:, :D_MODEL]) * y_ssm + jax.nn.sigmoid(g[:, D_MODEL:]) * y_attn
    x = x_ref[...] + _dot(mixed.astype(BF16), wout_ref[...])
    xo_ref[...] = x
    h2 = _rms(x, g2_ref[...])
    hi = h2.astype(BF16)
    lo = (h2 - hi.astype(F32)).astype(BF16)
    lg_ref[...] = (_dot(hi, wrh_ref[...]) + _dot(lo, wrh_ref[...]) + _dot(hi, wrl_ref[...])
                   + rb_ref[...])


def _mix(z, a, g, x, wglu, wattn, wout, g2, wrh, wrl, rb):
    n = x.shape[0]
    tm = TOKEN_TILE
    row = lambda i: (i, 0)
    fix = lambda i: (0, 0)
    full = lambda t: pl.BlockSpec(t.shape, fix)
    return pl.pallas_call(
        _mix_kernel,
        grid=(n // tm,),
        in_specs=[pl.BlockSpec((tm, SSM_WIDTH), row), pl.BlockSpec((tm, ATTN_WIDTH), row),
                  pl.BlockSpec((tm, 2 * D_MODEL), row), pl.BlockSpec((tm, D_MODEL), row),
                  full(wglu), full(wattn), full(wout), full(g2), full(wrh), full(wrl), full(rb)],
        out_specs=[pl.BlockSpec((tm, D_MODEL), row), pl.BlockSpec((tm, ROUTER_LANES), row)],
        out_shape=[jax.ShapeDtypeStruct((n, D_MODEL), F32),
                   jax.ShapeDtypeStruct((n, ROUTER_LANES), F32)],
        compiler_params=_params("parallel"),
        name="mix_out_router",
    )(z, a, g, x, wglu, wattn, wout, g2, wrh, wrl, rb)


def _route_kernel(lg_ref, eid_ref, rank_ref, w_ref, cnt_ref, carry_ref):
    @pl.when(pl.program_id(0) == 0)
    def _():
        carry_ref[...] = jnp.zeros_like(carry_ref)

    lg = lg_ref[...]
    t = lg.shape[0]
    lane = lax.broadcasted_iota(jnp.int32, (t, LANES), 1).astype(F32)

    def first_lane(hit):
        return jnp.min(jnp.where(hit, lane, float(LANES)), axis=1, keepdims=True)

    g_log = jnp.where(lane < N_GROUPS, lg, -jnp.inf)
    g_max = jnp.max(g_log, axis=1, keepdims=True)
    g_top = first_lane(g_log == g_max)
    p_g = 1.0 / jnp.sum(jnp.exp(g_log - g_max), axis=1, keepdims=True)

    lo = N_GROUPS + g_top * EXPERTS_PER_GROUP
    in_group = (lane >= lo) & (lane < lo + EXPERTS_PER_GROUP)
    e_log = jnp.where(in_group, lg, -jnp.inf)
    e_exp = jnp.exp(e_log - jnp.max(e_log, axis=1, keepdims=True))
    prob = jnp.where(in_group, e_exp / jnp.sum(e_exp, axis=1, keepdims=True), -1.0)
    p1 = jnp.max(prob, axis=1, keepdims=True)
    l1 = first_lane(prob == p1)
    prob = jnp.where(lane == l1, -1.0, prob)
    p2 = jnp.max(prob, axis=1, keepdims=True)
    l2 = first_lane(prob == p2)
    w_ref[...] = jnp.concatenate([p_g * (p1 / (p1 + p2)), p_g * (p2 / (p1 + p2))], axis=1)
    eid = [l1 - N_GROUPS, l2 - N_GROUPS]
    eid_ref[...] = jnp.concatenate(eid, axis=1).astype(jnp.int32)

    earlier = (lax.broadcasted_iota(jnp.int32, (t, t), 1)
               < lax.broadcasted_iota(jnp.int32, (t, t), 0)).astype(BF16)
    carry = carry_ref[...]
    ranks = []
    for e in eid:
        hit = lane == e
        onehot = hit.astype(F32)
        before = _dot(earlier, onehot.astype(BF16)) + carry
        ranks.append(jnp.sum(jnp.where(hit, before, 0.0), axis=1, keepdims=True))
        carry = carry + jnp.sum(onehot, axis=0, keepdims=True)
    carry_ref[...] = carry
    rank_ref[...] = jnp.concatenate(ranks, axis=1).astype(jnp.int32)
    cnt_ref[...] = carry.astype(jnp.int32)


def _route_rank(logits):
    n = logits.shape[0]
    t = TOKEN_TILE
    pair = pl.BlockSpec((t, EXPERT_TOPK), lambda i: (i, 0))
    return pl.pallas_call(
        _route_kernel,
        grid=(n // t,),
        in_specs=[pl.BlockSpec((t, ROUTER_LANES), lambda i: (i, 0))],
        out_specs=[pair, pair, pair, pl.BlockSpec((1, LANES), lambda i: (0, 0))],
        out_shape=[jax.ShapeDtypeStruct((n, EXPERT_TOPK), jnp.int32),
                   jax.ShapeDtypeStruct((n, EXPERT_TOPK), jnp.int32),
                   jax.ShapeDtypeStruct((n, EXPERT_TOPK), F32),
                   jax.ShapeDtypeStruct((1, LANES), jnp.int32)],
        scratch_shapes=[pltpu.VMEM((1, LANES), F32)],
        compiler_params=_params("arbitrary"),
        name="route_rank",
    )(logits)


def _store_row_tiles(ref, value):
    rows, half = value.shape[0], value.shape[1] // 2
    as_bits = lambda v: lax.bitcast_convert_type(v.astype(BF16).astype(F32), U32)
    words = as_bits(value[:, half:]) | (as_bits(value[:, :half]) >> 16)
    for c in range(ROW_TILES):
        ref[pl.ds(c, rows, stride=ROW_TILES), :] = words[:, c * LANES:(c + 1) * LANES]


def _load_row_tiles(ref, start, rows):
    words = jnp.concatenate(
        [ref[pl.ds(start * ROW_TILES + c, rows, stride=ROW_TILES), :] for c in range(ROW_TILES)], axis=1)
    low = lax.bitcast_convert_type(words << 16, F32)
    high = lax.bitcast_convert_type(words & jnp.uint32(0xFFFF0000), F32)
    return jnp.concatenate([low, high], axis=1)


def _row_dma(src, src_slab, dst, dst_slab, sem, rows=1):
    n = rows * ROW_TILES
    return pltpu.make_async_copy(src.at[pl.ds(src_slab, n), :], dst.at[pl.ds(dst_slab, n), :], sem)


def _wait_rows(src, dst, sem, count, rows=1):
    def body(r, _):
        _row_dma(src, 0, dst, 0, sem, rows).wait()
        return 0
    lax.fori_loop(0, count, body, 0, unroll=8 if isinstance(count, int) else 1)


def _dispatch_kernel(dest_ref, pad0_ref, npad_ref, x_ref, g2_ref, xg_hbm, hbuf, zrow, sem, zsem):
    s = pl.program_id(0)
    last = pl.num_programs(0) - 1
    slot = s % 2
    tt = x_ref.shape[0]
    rows = EXPERT_TOPK * tt

    @pl.when(s == 0)
    def _():
        zrow[...] = jnp.zeros_like(zrow)

        def tile_of(row):
            return pl.multiple_of(row * ROW_TILES, ROW_TILES)

        def per_expert(e, total):
            def fill(r, _):
                _row_dma(zrow, 0, xg_hbm, tile_of(pad0_ref[e] + r), zsem).start()
                return 0
            lax.fori_loop(0, npad_ref[e], fill, 0)
            return total + npad_ref[e]
        total = lax.fori_loop(0, N_EXPERTS, per_expert, 0)
        _wait_rows(zrow, xg_hbm, zsem, total)

        def fill_unused(r, _):
            row = pad0_ref[N_EXPERTS] + r * SUBLANES
            _row_dma(zrow, 0, xg_hbm, tile_of(row), zsem, SUBLANES).start()
            return 0
        lax.fori_loop(0, npad_ref[N_EXPERTS], fill_unused, 0)
        _wait_rows(zrow, xg_hbm, zsem, npad_ref[N_EXPERTS], SUBLANES)

    src = hbuf.at[slot]
    _store_row_tiles(src, _rms(x_ref[...], g2_ref[...]))
    for r in range(rows):
        dst_slab = pl.multiple_of(dest_ref[s * rows + r], ROW_TILES)
        _row_dma(src, (r % tt) * ROW_TILES, xg_hbm, dst_slab, sem.at[slot]).start(priority=r % 2)

    @pl.when(s > 0)
    def _():
        _wait_rows(hbuf.at[1 - slot], xg_hbm, sem.at[1 - slot], rows)

    @pl.when(s == last)
    def _():
        _wait_rows(src, xg_hbm, sem.at[slot], rows)


def _dispatch(dest, pad0, npad, x, g2, cap):
    n = x.shape[0]
    tt = COMBINE_TILE
    grid_spec = pltpu.PrefetchScalarGridSpec(
        num_scalar_prefetch=3,
        grid=(n // tt,),
        in_specs=[pl.BlockSpec((tt, D_MODEL), lambda s, d, p, c: (s, 0)),
                  pl.BlockSpec((1, D_MODEL), lambda s, d, p, c: (0, 0))],
        out_specs=pl.BlockSpec(memory_space=pl.ANY),
        scratch_shapes=[pltpu.VMEM((2, tt * ROW_TILES, LANES), U32),
                        pltpu.VMEM((SUBLANES * ROW_TILES, LANES), U32),
                        pltpu.SemaphoreType.DMA((2,)),
                        pltpu.SemaphoreType.DMA],
    )
    return pl.pallas_call(
        _dispatch_kernel,
        grid_spec=grid_spec,
        out_shape=jax.ShapeDtypeStruct((cap * ROW_TILES, LANES), U32),
        compiler_params=_params("arbitrary"),
        name="expert_dispatch",
    )(dest, pad0, npad, x, g2)


def _expert_kernel(blk_e_ref, nused_ref, xg_ref, wgu_ref, wd_ref, y_ref, wgu_bf, wd_bf):
    b = pl.program_id(0)
    prev = blk_e_ref[jnp.maximum(b, 1) - 1]

    @pl.when((b == 0) | (blk_e_ref[b] != prev))
    def _():
        wgu_bf[...] = wgu_ref[0, 0].astype(BF16)
        wd_bf[...] = wd_ref[0, 0].astype(BF16)

    @pl.when(b < nused_ref[0])
    def _():
        gu = _dot(_load_row_tiles(xg_ref, 0, EXPERT_ROWS).astype(BF16), wgu_bf[...])
        act = jax.nn.silu(gu[:, :D_EXPERT]) * gu[:, D_EXPERT:]
        _store_row_tiles(y_ref, _dot(act.astype(BF16), wd_bf[...]))

    @pl.when(b >= nused_ref[0])
    def _():
        y_ref[...] = jnp.zeros_like(y_ref)


def _experts(blk_e, nused, xg, w_gate_up, w_down, layer):
    rows = EXPERT_ROWS * ROW_TILES
    nblk = xg.shape[0] // rows
    used = lambda b, n: jnp.minimum(b, n[0] - 1)
    grid_spec = pltpu.PrefetchScalarGridSpec(
        num_scalar_prefetch=2,
        grid=(nblk,),
        in_specs=[pl.BlockSpec((rows, LANES), lambda b, e, n: (used(b, n), 0)),
                  pl.BlockSpec((1, 1, D_MODEL, 2 * D_EXPERT), lambda b, e, n: (layer, e[b], 0, 0)),
                  pl.BlockSpec((1, 1, D_EXPERT, D_MODEL), lambda b, e, n: (layer, e[b], 0, 0))],
        out_specs=pl.BlockSpec((rows, LANES), lambda b, e, n: (b, 0)),
        scratch_shapes=[pltpu.VMEM((D_MODEL, 2 * D_EXPERT), BF16),
                        pltpu.VMEM((D_EXPERT, D_MODEL), BF16)],
    )
    return pl.pallas_call(
        _expert_kernel,
        grid_spec=grid_spec,
        out_shape=jax.ShapeDtypeStruct((nblk * rows, LANES), U32),
        compiler_params=_params("arbitrary"),
        name="experts",
    )(blk_e, nused, xg, w_gate_up, w_down)


def _combine_kernel(pos_ref, x_ref, w_ref, y_hbm, o_ref, ybuf, sem):
    s = pl.program_id(0)
    nsteps = pl.num_programs(0)
    slot = s % 2
    tt = COMBINE_TILE
    rows = EXPERT_TOPK * tt

    def issue(step, slot):
        dst = ybuf.at[slot]
        for r in range(rows):
            src_slab = pl.multiple_of(pos_ref[step * rows + r], ROW_TILES)
            _row_dma(y_hbm, src_slab, dst, r * ROW_TILES, sem.at[slot]).start(priority=r % 2)

    @pl.when(s == 0)
    def _():
        issue(0, 0)

    @pl.when(s + 1 < nsteps)
    def _():
        issue(s + 1, 1 - slot)

    _wait_rows(y_hbm, ybuf.at[slot], sem.at[slot], rows)
    w = w_ref[...]
    got = ybuf.at[slot]
    o_ref[...] = (x_ref[...] + w[:, 0:1] * _load_row_tiles(got, 0, tt)
                  + w[:, 1:2] * _load_row_tiles(got, tt, tt))


def _combine(pos, x, w, y):
    n = x.shape[0]
    tt = COMBINE_TILE
    grid_spec = pltpu.PrefetchScalarGridSpec(
        num_scalar_prefetch=1,
        grid=(n // tt,),
        in_specs=[pl.BlockSpec((tt, D_MODEL), lambda s, p: (s, 0)),
                  pl.BlockSpec((tt, EXPERT_TOPK), lambda s, p: (s, 0)),
                  pl.BlockSpec(memory_space=pl.ANY)],
        out_specs=pl.BlockSpec((tt, D_MODEL), lambda s, p: (s, 0)),
        scratch_shapes=[pltpu.VMEM((2, EXPERT_TOPK * tt * ROW_TILES, LANES), U32),
                        pltpu.SemaphoreType.DMA((2,))],
    )
    return pl.pallas_call(
        _combine_kernel,
        grid_spec=grid_spec,
        out_shape=jax.ShapeDtypeStruct((n, D_MODEL), F32),
        compiler_params=_params("arbitrary"),
        name="moe_combine",
    )(pos, x, w, y)


def _rope_tables(seq):
    inv = ROPE_THETA ** (-jnp.arange(0, HEAD_DIM, 2, dtype=F32) / HEAD_DIM)
    ang = jnp.arange(seq, dtype=F32)[:, None] * inv[None, :]
    cos, sin = jnp.cos(ang), jnp.sin(ang)
    cos = jnp.concatenate([cos, cos] * (LANES // HEAD_DIM), axis=1)
    sin = jnp.concatenate([-sin, sin] * (LANES // HEAD_DIM), axis=1)
    return cos, sin


def _ssm_weights(a_re, a_im, log_dt, b_re, b_im, c_re, c_im):
    dt = jnp.exp(log_dt)[:, None]
    decay = jnp.exp(a_re * dt)
    abar_re = decay * jnp.cos(a_im * dt)
    abar_im = decay * jnp.sin(a_im * dt)
    den = a_re * a_re + a_im * a_im
    num_re = abar_re - 1.0
    f_re = (num_re * a_re + abar_im * a_im) / den
    f_im = (abar_im * a_re - num_re * a_im) / den
    bbar_re = f_re[..., None] * b_re - f_im[..., None] * b_im
    bbar_im = f_re[..., None] * b_im + f_im[..., None] * b_re
    gh = SSM_GROUPS // 2
    eye = jnp.eye(gh, dtype=F32)

    def b_mat(t):
        t = t.reshape(2, gh, SSM_STATE, SSM_GROUP_CH)
        return jnp.einsum('cgph,gk->cghkp', t, eye).reshape(2, SSM_HALF, SSM_HALF_STATE).astype(BF16)

    def c_mat(t):
        t = t.reshape(2, gh, SSM_GROUP_CH, SSM_STATE)
        return jnp.einsum('cghp,gk->cgpkh', t, eye).reshape(2, SSM_HALF_STATE, SSM_HALF).astype(BF16)

    a_vec = lambda t: t.reshape(2, 1, SSM_HALF_STATE)
    return (b_mat(bbar_re), b_mat(bbar_im), c_mat(c_re), c_mat(-c_im), a_vec(abar_re), a_vec(abar_im))


def _route(logits, n):
    expert_id, rank, weights, cnt = _route_rank(logits)
    counts = cnt[0, :N_EXPERTS]
    padded = (counts + EXPERT_ROWS - 1) // EXPERT_ROWS * EXPERT_ROWS
    pend = jnp.cumsum(padded)
    pstart = pend - padded
    experts = jnp.arange(N_EXPERTS, dtype=jnp.int32)
    start_of = jnp.sum(jnp.where(expert_id[..., None] == experts, pstart, 0), axis=-1)
    dest = (start_of + rank).astype(jnp.int32)
    cap = n * EXPERT_TOPK + N_EXPERTS * EXPERT_ROWS
    nblk = cap // EXPERT_ROWS
    blk_row = jnp.arange(nblk, dtype=jnp.int32)[:, None] * EXPERT_ROWS
    blk_e = jnp.minimum(jnp.sum((pend[None, :] <= blk_row).astype(jnp.int32), axis=1), N_EXPERTS - 1)
    nused = (pend[-1:] // EXPERT_ROWS).astype(jnp.int32)
    pos = dest.reshape(n // COMBINE_TILE, COMBINE_TILE, EXPERT_TOPK).transpose(0, 2, 1).reshape(-1)
    pos = pos * ROW_TILES
    pad0 = jnp.concatenate([pstart + counts, pend[-1:]]).astype(jnp.int32)
    npad = jnp.concatenate([padded - counts, (cap - pend[-1:]) // SUBLANES]).astype(jnp.int32)
    return weights, pos, pad0, npad, blk_e, nused, cap


def kernel(x, norm1_g, w_in, ssm_a_re, ssm_a_im, ssm_log_dt, ssm_b_re, ssm_b_im, ssm_c_re, ssm_c_im,
           ssm_d, w_glu, q_norm_g, k_norm_g, w_attn, w_out, norm2_g, router_w_group, router_b_group,
           router_w_expert, router_b_expert, w_gate_up, w_down):
    batch, seq, _ = x.shape
    depth = w_in.shape[0]
    n = batch * seq
    assert batch == SUBLANES and seq % MOBA_BLOCK == 0 and n % TOKEN_TILE == 0
    assert seq // MOBA_BLOCK <= GATE_ROWS and seq % SSM_STEPS == 0

    xt = x.reshape(n, D_MODEL)
    cos, sin = _rope_tables(seq)
    time_major = lambda t: t.reshape(batch, seq, -1).transpose(1, 0, 2).reshape(n, -1)
    batch_major = lambda t: t.reshape(seq, batch, -1).transpose(1, 0, 2).reshape(n, -1)
    idx = jnp.arange(ATTN_WIDTH)
    gsum =(idx[:, None] // HEAD_DIM == idx[None, :] // HEAD_DIM).astype(BF16)
    row = lambda t: t.reshape(1, -1).astype(F32)

    for l in range(depth):
        qg = row(jnp.tile(q_norm_g[l], N_HEADS)) * (HEAD_DIM ** -0.5 * math.log2(math.e))
        kg = row(jnp.tile(k_norm_g[l], N_HEADS))
        u, q, k, v, g = _in_proj(xt, row(norm1_g[l]), w_in[l].astype(BF16), gsum, qg, kg, cos, sin)
        ssm_w = _ssm_weights(ssm_a_re[l], ssm_a_im[l], ssm_log_dt[l], ssm_b_re[l], ssm_b_im[l],
                             ssm_c_re[l], ssm_c_im[l])
        z = batch_major(_ssm(time_major(u), *ssm_w, row(ssm_d[l]), batch))
        attn = _moba(q, k, v, batch)

        w_r = jnp.concatenate([router_w_group[l], router_w_expert[l]], axis=1)
        w_r = jnp.pad(w_r, ((0, 0), (0, ROUTER_LANES - w_r.shape[1])))
        w_rh = w_r.astype(BF16)
        w_rl = (w_r - w_rh.astype(F32)).astype(BF16)
        b_r = jnp.concatenate([router_b_group[l], router_b_expert[l]])
        b_r = row(jnp.pad(b_r, (0, ROUTER_LANES - b_r.shape[0])))
        g2 = row(norm2_g[l])
        xt, logits = _mix(z, attn, g, xt, w_glu[l].astype(BF16), w_attn[l].astype(BF16),
                          w_out[l].astype(BF16), g2, w_rh, w_rl, b_r)

        weights, pos, pad0, npad, blk_e, nused, cap = _route(logits, n)
        xg = _dispatch(pos, pad0, npad, xt, g2, cap)
        y = _experts(blk_e, nused, xg, w_gate_up, w_down, l)
        xt = _combine(pos, xt, weights, y)

    return xt.reshape(batch, seq, D_MODEL)
```

```python
import functools
import math

import jax
import jax.numpy as jnp
from jax import lax
from jax.experimental import pallas as pl
from jax.experimental.pallas import tpu as pltpu

F32 = jnp.float32
BF16 = jnp.bfloat16
U32 = jnp.uint32

D_MODEL = 1024
SSM_GROUPS = 32
SSM_GROUP_CH = 16
SSM_WIDTH = SSM_GROUPS * SSM_GROUP_CH
SSM_STATE = 64
N_HEADS = 8
HEAD_DIM = 64
ATTN_WIDTH = N_HEADS * HEAD_DIM
MOBA_BLOCK = 256
MOBA_TOPK = 3
ROPE_THETA = 10000.0
D_IN = SSM_WIDTH + 3 * ATTN_WIDTH + 2 * D_MODEL
N_GROUPS = 4
EXPERTS_PER_GROUP = 8
N_EXPERTS = N_GROUPS * EXPERTS_PER_GROUP
EXPERT_TOPK = 2
D_EXPERT = 512
EXPERT_ROWS = 256
NORM_EPS = 1e-6

LANES = 128
SUBLANES = 8
VMEM_LIMIT = 56 * 1024 * 1024
MASK_NEG = -1e30

TOKEN_TILE = 512
SSM_STEPS = 64
SSM_HALF = SSM_WIDTH // 2
SSM_HALF_STATE = SSM_GROUPS * SSM_STATE // 2
GATE_ROWS = 16
V_ROWS = HEAD_DIM + 16
SOFTMAX_LAG = 3
VALUES_LAG = 6
ROUTER_LANES = 128
COMBINE_TILE = 256
ROW_TILES = D_MODEL // (2 * LANES)


def _params(*sem):
    return pltpu.CompilerParams(dimension_semantics=sem, vmem_limit_bytes=VMEM_LIMIT)


def _dot(a, b):
    return jnp.dot(a, b, preferred_element_type=F32)


def _rms(x, gain):
    return x * lax.rsqrt(jnp.mean(x * x, axis=-1, keepdims=True) + NORM_EPS) * gain


def _in_proj_kernel(x_ref, g1_ref, w_ref, gs_ref, qg_ref, kg_ref, cos_ref, sin_ref,
                    u_ref, q_ref, k_ref, v_ref, g_ref):
    hb = _rms(x_ref[...], g1_ref[...]).astype(BF16)

    def seg(lo, hi):
        return _dot(hb, w_ref[:, lo:hi])

    o_q = SSM_WIDTH
    o_k = o_q + ATTN_WIDTH
    o_v = o_k + ATTN_WIDTH
    o_g = o_v + ATTN_WIDTH
    u_ref[...] = seg(0, o_q).astype(BF16)
    v_ref[...] = seg(o_v, o_g).astype(BF16)
    g_ref[...] = seg(o_g, D_IN).astype(BF16)

    reps = ATTN_WIDTH // LANES
    cos = jnp.concatenate([cos_ref[...]] * reps, axis=1)
    sin = jnp.concatenate([sin_ref[...]] * reps, axis=1)
    lane = lax.broadcasted_iota(jnp.int32, cos.shape, 1)
    first_half = (lane % HEAD_DIM) < (HEAD_DIM // 2)

    def norm_rope(t, gain):
        ss = _dot((t * t).astype(BF16), gs_ref[...])
        tn = t * lax.rsqrt(ss * (1.0 / HEAD_DIM) + NORM_EPS) * gain
        partner = jnp.where(first_half,
                            pltpu.roll(tn, ATTN_WIDTH - HEAD_DIM // 2, 1),
                            pltpu.roll(tn, HEAD_DIM // 2, 1))
        return tn * cos + partner * sin

    q_ref[...] = norm_rope(seg(o_q, o_k), qg_ref[...]).astype(BF16)
    k_ref[...] = norm_rope(seg(o_k, o_v), kg_ref[...]).astype(BF16)


def _in_proj(x, g1, w_in, gsum, qg, kg, cos, sin):
    n = x.shape[0]
    tm = TOKEN_TILE
    row = lambda i: (i, 0)
    fix = lambda i: (0, 0)
    seq = cos.shape[0]
    tiles_per_seq = seq // tm
    pos = lambda i: (i % tiles_per_seq, 0)
    outs = [jax.ShapeDtypeStruct((n, w), BF16)
            for w in (SSM_WIDTH, ATTN_WIDTH, ATTN_WIDTH, ATTN_WIDTH, 2 * D_MODEL)]
    return pl.pallas_call(
        _in_proj_kernel,
        grid=(n // tm,),
        in_specs=[pl.BlockSpec((tm, D_MODEL), row),
                  pl.BlockSpec((1, D_MODEL), fix),
                  pl.BlockSpec((D_MODEL, D_IN), fix),
                  pl.BlockSpec((ATTN_WIDTH, ATTN_WIDTH), fix),
                  pl.BlockSpec((1, ATTN_WIDTH), fix),
                  pl.BlockSpec((1, ATTN_WIDTH), fix),
                  pl.BlockSpec((tm, LANES), pos),
                  pl.BlockSpec((tm, LANES), pos)],
        out_specs=[pl.BlockSpec((tm, o.shape[1]), row) for o in outs],
        out_shape=outs,
        compiler_params=_params("parallel"),
        name="in_proj",
    )(x, g1, w_in, gsum, qg, kg, cos, sin)


def _ssm_kernel(u_ref, bre_ref, bim_ref, cre_ref, cim_ref, are_ref, aim_ref, d_ref,
                z_ref, h_ref, sre_ref, sim_ref, *, batch):
    @pl.when(pl.program_id(0) == 0)
    def _():
        h_ref[...] = jnp.zeros_like(h_ref)

    steps = u_ref.shape[0] // batch
    ys = []
    for c in range(2):
        uc = u_ref[:, c * SSM_HALF:(c + 1) * SSM_HALF]
        sre_ref[...] = _dot(uc, bre_ref[c])
        sim_ref[...] = _dot(uc, bim_ref[c])
        a_re = jnp.broadcast_to(are_ref[c], (batch, SSM_HALF_STATE))
        a_im = jnp.broadcast_to(aim_ref[c], (batch, SSM_HALF_STATE))

        def step(t, carry):
            h_re, h_im = carry
            r0 = pl.multiple_of(t * batch, batch)
            n_re = a_re * h_re - a_im * h_im + sre_ref[pl.ds(r0, batch), :]
            n_im = a_re * h_im + a_im * h_re + sim_ref[pl.ds(r0, batch), :]
            sre_ref[pl.ds(r0, batch), :] = n_re
            sim_ref[pl.ds(r0, batch), :] = n_im
            return n_re, n_im

        h_re, h_im = lax.fori_loop(0, steps, step, (h_ref[c, 0], h_ref[c, 1]), unroll=True)
        h_ref[c, 0] = h_re
        h_ref[c, 1] = h_im
        ys.append(_dot(sre_ref[...].astype(BF16), cre_ref[c])
                  + _dot(sim_ref[...].astype(BF16), cim_ref[c]))
    y = jnp.concatenate(ys, axis=1) + d_ref[...] * u_ref[...].astype(F32)
    z_ref[...] = jax.nn.gelu(y).astype(BF16)


def _ssm(u, bre, bim, cre, cim, are, aim, d, batch):
    n = u.shape[0]
    rows = SSM_STEPS * batch
    fix3 = lambda i: (0, 0, 0)
    return pl.pallas_call(
        functools.partial(_ssm_kernel, batch=batch),
        grid=(n // rows,),
        in_specs=[pl.BlockSpec((rows, SSM_WIDTH), lambda i: (i, 0)),
                  pl.BlockSpec(bre.shape, fix3), pl.BlockSpec(bim.shape, fix3),
                  pl.BlockSpec(cre.shape, fix3), pl.BlockSpec(cim.shape, fix3),
                  pl.BlockSpec(are.shape, fix3), pl.BlockSpec(aim.shape, fix3),
                  pl.BlockSpec((1, SSM_WIDTH), lambda i: (0, 0))],
        out_specs=pl.BlockSpec((rows, SSM_WIDTH), lambda i: (i, 0)),
        out_shape=jax.ShapeDtypeStruct((n, SSM_WIDTH), BF16),
        scratch_shapes=[pltpu.VMEM((2, 2, batch, SSM_HALF_STATE), F32),
                        pltpu.VMEM((rows, SSM_HALF_STATE), F32),
                        pltpu.VMEM((rows, SSM_HALF_STATE), F32)],
        compiler_params=_params("arbitrary"),
        name="s5_scan",
    )(u, bre, bim, cre, cim, are, aim, d)


def _moba_kernel(q_ref, k_ref, v_ref, o_ref, kpad_ref, vt_ref, km_ref, qt_ref, m_ref, alpha_ref, acc_ref,
                 s_ref, p_ref):
    i = pl.program_id(1)
    blk = MOBA_BLOCK
    nblk = k_ref.shape[0] // blk
    n_sel = min(MOBA_TOPK, nblk - 1)

    @pl.when(i == 0)
    def _build():
        lane = lax.broadcasted_iota(jnp.int32, (blk, LANES), 1)
        is_head_lane = lane < HEAD_DIM
        km_ref[...] = jnp.zeros_like(km_ref)
        ones_row = (lax.broadcasted_iota(jnp.int32, (V_ROWS - HEAD_DIM, blk), 0) == 0)

        def per_block(j, _):
            r0 = pl.multiple_of(j * blk, blk)
            vt = v_ref[pl.ds(r0, blk), :].astype(F32).T
            for h in range(N_HEADS):
                p = h // 2
                kk = k_ref[pl.ds(r0, blk), p * LANES:(p + 1) * LANES].astype(F32)
                if h % 2:
                    kk = pltpu.roll(kk, HEAD_DIM, 1)
                kk = jnp.where(is_head_lane, kk, 0.0)
                km_ref[h, pl.ds(j, 1), :] = jnp.mean(kk, axis=0, keepdims=True)
                kk = jnp.where(lane == HEAD_DIM + j, 1.0, kk)
                kpad_ref[j, :, h * LANES:(h + 1) * LANES] = kk.astype(BF16)
                vt_ref[j, h, 0:HEAD_DIM, :] = vt[h * HEAD_DIM:(h + 1) * HEAD_DIM, :].astype(BF16)
                vt_ref[j, h, HEAD_DIM:V_ROWS, :] = ones_row.astype(BF16)
            return 0

        lax.fori_loop(0, nblk, per_block, 0)

    qt = q_ref[...].astype(F32).T
    blk_row = lax.broadcasted_iota(jnp.int32, (GATE_ROWS, blk), 0)
    past = blk_row < i
    blk_row_f = blk_row.astype(F32)
    zeros_q = jnp.zeros((LANES - HEAD_DIM, blk), F32)
    zeros_pad = jnp.zeros((LANES - HEAD_DIM - GATE_ROWS, blk), F32)
    for h in range(N_HEADS):
        qh = qt[h * HEAD_DIM:(h + 1) * HEAD_DIM, :]
        q_pad = jnp.concatenate([qh, zeros_q], axis=0).astype(BF16)
        km = km_ref[h]
        km_hi = km.astype(BF16)
        km_lo = (km - km_hi.astype(F32)).astype(BF16)
        gate = jnp.where(past, _dot(km_hi, q_pad) + _dot(km_lo, q_pad), -jnp.inf)
        sel = jnp.zeros(gate.shape, jnp.bool_)
        for _ in range(n_sel):
            best = jnp.max(gate, axis=0, keepdims=True)
            cand = (gate == best) & (best > -jnp.inf)
            first = jnp.min(jnp.where(cand, blk_row_f, float(GATE_ROWS)), axis=0, keepdims=True)
            pick = blk_row_f == first
            sel = sel | pick
            gate = jnp.where(pick, -jnp.inf, gate)
        bias = jnp.where(past & ~sel, MASK_NEG, 0.0)
        qt_ref[h] = jnp.concatenate([qh, bias, zeros_pad], axis=0).astype(BF16)
        m_ref[h] = jnp.full((1, blk), -jnp.inf, F32)
        acc_ref[h] = jnp.zeros((V_ROWS, blk), F32)

    n_s, n_p = s_ref.shape[0], p_ref.shape[0]

    def scores(j, h, u):
        s_ref[u % n_s] = _dot(kpad_ref[j, :, h * LANES:(h + 1) * LANES], qt_ref[h])

    def softmax(h, u, mask):
        st = s_ref[u % n_s]
        if mask is not None:
            st = jnp.where(mask, st, MASK_NEG)
        m_old = m_ref[h]
        m_new = jnp.maximum(m_old, jnp.max(st, axis=0, keepdims=True))
        alpha_ref[h] = jnp.exp2(m_old - m_new)
        p_ref[u % n_p] = jnp.exp2((st - m_new).astype(BF16))
        m_ref[h] = m_new

    def weighted_values(j, h, u):
        acc_ref[h] = alpha_ref[h] * acc_ref[h] + _dot(vt_ref[j, h], p_ref[u % n_p])

    def kv_blocks(js, last_mask=None):
        units = [(j, h, last_mask if b == len(js) - 1 else None)
                 for b, j in enumerate(js) for h in range(N_HEADS)]
        for step in range(len(units) + VALUES_LAG):
            if step < len(units):
                scores(units[step][0], units[step][1], step)
            u = step - SOFTMAX_LAG
            if 0 <= u < len(units):
                softmax(units[u][1], u, units[u][2])
            u = step - VALUES_LAG
            if 0 <= u < len(units):
                weighted_values(units[u][0], units[u][1], u)

    def octet(g, _):
        kv_blocks([g * 8 + d for d in range(8)])
        return 0

    def quad(g, _):
        kv_blocks([g * 4 + d for d in range(4)])
        return 0

    n_octets = i // 8
    lax.fori_loop(0, n_octets, octet, 0)
    lax.fori_loop(n_octets * 2, i // 4, quad, 0)

    key_pos = lax.broadcasted_iota(jnp.int32, (blk, blk), 0)
    qry_pos = lax.broadcasted_iota(jnp.int32, (blk, blk), 1)
    causal = key_pos <= qry_pos
    left = i % 4

    @pl.when(left == 0)
    def _():
        kv_blocks([i], causal)

    @pl.when(left == 1)
    def _():
        kv_blocks([i - 1, i], causal)

    @pl.when(left == 2)
    def _():
        kv_blocks([i - 2, i - 1])
        kv_blocks([i], causal)

    @pl.when(left == 3)
    def _():
        kv_blocks([i - 3, i - 2, i - 1, i], causal)
    outs = []
    for h in range(N_HEADS):
        acc = acc_ref[h]
        outs.append(acc[0:HEAD_DIM, :] / acc[HEAD_DIM:HEAD_DIM + 1, :])
    o_ref[...] = jnp.concatenate(outs, axis=0).T.astype(BF16)


def _moba(q, k, v, batch):
    n = q.shape[0]
    seq = n // batch
    tq = MOBA_BLOCK
    nblk = seq // MOBA_BLOCK
    return pl.pallas_call(
        _moba_kernel,
        grid=(batch, nblk),
        in_specs=[pl.BlockSpec((tq, ATTN_WIDTH), lambda b, i: (b * nblk + i, 0)),
                  pl.BlockSpec((seq, ATTN_WIDTH), lambda b, i: (b, 0)),
                  pl.BlockSpec((seq, ATTN_WIDTH), lambda b, i: (b, 0))],
        out_specs=pl.BlockSpec((tq, ATTN_WIDTH), lambda b, i: (b * nblk + i, 0)),
        out_shape=jax.ShapeDtypeStruct((n, ATTN_WIDTH), BF16),
        scratch_shapes=[pltpu.VMEM((nblk, MOBA_BLOCK, N_HEADS * LANES), BF16),
                        pltpu.VMEM((nblk, N_HEADS, V_ROWS, MOBA_BLOCK), BF16),
                        pltpu.VMEM((N_HEADS, GATE_ROWS, LANES), F32),
                        pltpu.VMEM((N_HEADS, LANES, MOBA_BLOCK), BF16),
                        pltpu.VMEM((N_HEADS, 1, MOBA_BLOCK), F32),
                        pltpu.VMEM((N_HEADS, 1, MOBA_BLOCK), F32),
                        pltpu.VMEM((N_HEADS, V_ROWS, MOBA_BLOCK), F32),
                        pltpu.VMEM((VALUES_LAG, MOBA_BLOCK, MOBA_BLOCK), F32),
                        pltpu.VMEM((VALUES_LAG - SOFTMAX_LAG + 1, MOBA_BLOCK, MOBA_BLOCK), BF16)],
        compiler_params=_params("parallel", "arbitrary"),
        name="moba_attention",
    )(q, k, v)


def _mix_kernel(z_ref, a_ref, g_ref, x_ref, wglu_ref, wattn_ref, wout_ref, g2_ref,
                wrh_ref, wrl_ref, rb_ref, xo_ref, lg_ref):
    glu = _dot(z_ref[...], wglu_ref[...])
    y_ssm = glu[:, :D_MODEL] * jax.nn.sigmoid(glu[:, D_MODEL:])
    y_attn = _dot(a_ref[...], wattn_ref[...])
    g = g_ref[...].astype(F32)
    mixed = jax.nn.sigmoid(g[:, :D_MODEL]) * y_ssm + jax.nn.sigmoid(g[:, D_MODEL:]) * y_attn
    x = x_ref[...] + _dot(mixed.astype(BF16), wout_ref[...])
    xo_ref[...] = x
    h2 = _rms(x, g2_ref[...])
    hi = h2.astype(BF16)
    lo = (h2 - hi.astype(F32)).astype(BF16)
    lg_ref[...] = (_dot(hi, wrh_ref[...]) + _dot(lo, wrh_ref[...]) + _dot(hi, wrl_ref[...])
                   + rb_ref[...])


def _mix(z, a, g, x, wglu, wattn, wout, g2, wrh, wrl, rb):
    n = x.shape[0]
    tm = TOKEN_TILE
    row = lambda i: (i, 0)
    fix = lambda i: (0, 0)
    full = lambda t: pl.BlockSpec(t.shape, fix)
    return pl.pallas_call(
        _mix_kernel,
        grid=(n // tm,),
        in_specs=[pl.BlockSpec((tm, SSM_WIDTH), row), pl.BlockSpec((tm, ATTN_WIDTH), row),
                  pl.BlockSpec((tm, 2 * D_MODEL), row), pl.BlockSpec((tm, D_MODEL), row),
                  full(wglu), full(wattn), full(wout), full(g2), full(wrh), full(wrl), full(rb)],
        out_specs=[pl.BlockSpec((tm, D_MODEL), row), pl.BlockSpec((tm, ROUTER_LANES), row)],
        out_shape=[jax.ShapeDtypeStruct((n, D_MODEL), F32),
                   jax.ShapeDtypeStruct((n, ROUTER_LANES), F32)],
        compiler_params=_params("parallel"),
        name="mix_out_router",
    )(z, a, g, x, wglu, wattn, wout, g2, wrh, wrl, rb)


def _route_kernel(lg_ref, eid_ref, rank_ref, w_ref, cnt_ref, carry_ref):
    @pl.when(pl.program_id(0) == 0)
    def _():
        carry_ref[...] = jnp.zeros_like(carry_ref)

    lg = lg_ref[...]
    t = lg.shape[0]
    lane = lax.broadcasted_iota(jnp.int32, (t, LANES), 1).astype(F32)

    def first_lane(hit):
        return jnp.min(jnp.where(hit, lane, float(LANES)), axis=1, keepdims=True)

    g_log = jnp.where(lane < N_GROUPS, lg, -jnp.inf)
    g_max = jnp.max(g_log, axis=1, keepdims=True)
    g_top = first_lane(g_log == g_max)
    p_g = 1.0 / jnp.sum(jnp.exp(g_log - g_max), axis=1, keepdims=True)

    lo = N_GROUPS + g_top * EXPERTS_PER_GROUP
    in_group = (lane >= lo) & (lane < lo + EXPERTS_PER_GROUP)
    e_log = jnp.where(in_group, lg, -jnp.inf)
    e_exp = jnp.exp(e_log - jnp.max(e_log, axis=1, keepdims=True))
    prob = jnp.where(in_group, e_exp / jnp.sum(e_exp, axis=1, keepdims=True), -1.0)
    p1 = jnp.max(prob, axis=1, keepdims=True)
    l1 = first_lane(prob == p1)
    prob = jnp.where(lane == l1, -1.0, prob)
    p2 = jnp.max(prob, axis=1, keepdims=True)
    l2 = first_lane(prob == p2)
    w_ref[...] = jnp.concatenate([p_g * (p1 / (p1 + p2)), p_g * (p2 / (p1 + p2))], axis=1)
    eid = [l1 - N_GROUPS, l2 - N_GROUPS]
    eid_ref[...] = jnp.concatenate(eid, axis=1).astype(jnp.int32)

    earlier = (lax.broadcasted_iota(jnp.int32, (t, t), 1)
               < lax.broadcasted_iota(jnp.int32, (t, t), 0)).astype(BF16)
    carry = carry_ref[...]
    ranks = []
    for e in eid:
        hit = lane == e
        onehot = hit.astype(F32)
        before = _dot(earlier, onehot.astype(BF16)) + carry
        ranks.append(jnp.sum(jnp.where(hit, before, 0.0), axis=1, keepdims=True))
        carry = carry + jnp.sum(onehot, axis=0, keepdims=True)
    carry_ref[...] = carry
    rank_ref[...] = jnp.concatenate(ranks, axis=1).astype(jnp.int32)
    cnt_ref[...] = carry.astype(jnp.int32)


def _route_rank(logits):
    n = logits.shape[0]
    t = TOKEN_TILE
    pair = pl.BlockSpec((t, EXPERT_TOPK), lambda i: (i, 0))
    return pl.pallas_call(
        _route_kernel,
        grid=(n // t,),
        in_specs=[pl.BlockSpec((t, ROUTER_LANES), lambda i: (i, 0))],
        out_specs=[pair, pair, pair, pl.BlockSpec((1, LANES), lambda i: (0, 0))],
        out_shape=[jax.ShapeDtypeStruct((n, EXPERT_TOPK), jnp.int32),
                   jax.ShapeDtypeStruct((n, EXPERT_TOPK), jnp.int32),
                   jax.ShapeDtypeStruct((n, EXPERT_TOPK), F32),
                   jax.ShapeDtypeStruct((1, LANES), jnp.int32)],
        scratch_shapes=[pltpu.VMEM((1, LANES), F32)],
        compiler_params=_params("arbitrary"),
        name="route_rank",
    )(logits)


def _store_row_tiles(ref, value):
    rows, half = value.shape[0], value.shape[1] // 2
    as_bits = lambda v: lax.bitcast_convert_type(v.astype(BF16).astype(F32), U32)
    words = as_bits(value[:, half:]) | (as_bits(value[:, :half]) >> 16)
    for c in range(ROW_TILES):
        ref[pl.ds(c, rows, stride=ROW_TILES), :] = words[:, c * LANES:(c + 1) * LANES]


def _load_row_tiles(ref, start, rows):
    words = jnp.concatenate(
        [ref[pl.ds(start * ROW_TILES + c, rows, stride=ROW_TILES), :] for c in range(ROW_TILES)], axis=1)
    low = lax.bitcast_convert_type(words << 16, F32)
    high = lax.bitcast_convert_type(words & jnp.uint32(0xFFFF0000), F32)
    return jnp.concatenate([low, high], axis=1)


def _row_dma(src, src_slab, dst, dst_slab, sem, rows=1):
    n = rows * ROW_TILES
    return pltpu.make_async_copy(src.at[pl.ds(src_slab, n), :], dst.at[pl.ds(dst_slab, n), :], sem)


def _wait_rows(src, dst, sem, count, rows=1):
    def body(r, _):
        _row_dma(src, 0, dst, 0, sem, rows).wait()
        return 0
    lax.fori_loop(0, count, body, 0, unroll=8 if isinstance(count, int) else 1)


def _dispatch_kernel(dest_ref, pad0_ref, npad_ref, x_ref, g2_ref, xg_hbm, hbuf, zrow, sem, zsem):
    s = pl.program_id(0)
    last = pl.num_programs(0) - 1
    slot = s % 2
    tt = x_ref.shape[0]
    rows = EXPERT_TOPK * tt

    @pl.when(s == 0)
    def _():
        zrow[...] = jnp.zeros_like(zrow)

        def tile_of(row):
            return pl.multiple_of(row * ROW_TILES, ROW_TILES)

        def per_expert(e, total):
            def fill(r, _):
                _row_dma(zrow, 0, xg_hbm, tile_of(pad0_ref[e] + r), zsem).start()
                return 0
            lax.fori_loop(0, npad_ref[e], fill, 0)
            return total + npad_ref[e]
        total = lax.fori_loop(0, N_EXPERTS, per_expert, 0)
        _wait_rows(zrow, xg_hbm, zsem, total)

        def fill_unused(r, _):
            row = pad0_ref[N_EXPERTS] + r * SUBLANES
            _row_dma(zrow, 0, xg_hbm, tile_of(row), zsem, SUBLANES).start()
            return 0
        lax.fori_loop(0, npad_ref[N_EXPERTS], fill_unused, 0)
        _wait_rows(zrow, xg_hbm, zsem, npad_ref[N_EXPERTS], SUBLANES)

    src = hbuf.at[slot]
    _store_row_tiles(src, _rms(x_ref[...], g2_ref[...]))
    for r in range(rows):
        dst_slab = pl.multiple_of(dest_ref[s * rows + r], ROW_TILES)
        _row_dma(src, (r % tt) * ROW_TILES, xg_hbm, dst_slab, sem.at[slot]).start(priority=r % 2)

    @pl.when(s > 0)
    def _():
        _wait_rows(hbuf.at[1 - slot], xg_hbm, sem.at[1 - slot], rows)

    @pl.when(s == last)
    def _():
        _wait_rows(src, xg_hbm, sem.at[slot], rows)


def _dispatch(dest, pad0, npad, x, g2, cap):
    n = x.shape[0]
    tt = COMBINE_TILE
    grid_spec = pltpu.PrefetchScalarGridSpec(
        num_scalar_prefetch=3,
        grid=(n // tt,),
        in_specs=[pl.BlockSpec((tt, D_MODEL), lambda s, d, p, c: (s, 0)),
                  pl.BlockSpec((1, D_MODEL), lambda s, d, p, c: (0, 0))],
        out_specs=pl.BlockSpec(memory_space=pl.ANY),
        scratch_shapes=[pltpu.VMEM((2, tt * ROW_TILES, LANES), U32),
                        pltpu.VMEM((SUBLANES * ROW_TILES, LANES), U32),
                        pltpu.SemaphoreType.DMA((2,)),
                        pltpu.SemaphoreType.DMA],
    )
    return pl.pallas_call(
        _dispatch_kernel,
        grid_spec=grid_spec,
        out_shape=jax.ShapeDtypeStruct((cap * ROW_TILES, LANES), U32),
        compiler_params=_params("arbitrary"),
        name="expert_dispatch",
    )(dest, pad0, npad, x, g2)


def _expert_kernel(blk_e_ref, nused_ref, xg_ref, wgu_ref, wd_ref, y_ref, wgu_bf, wd_bf):
    b = pl.program_id(0)
    prev = blk_e_ref[jnp.maximum(b, 1) - 1]

    @pl.when((b == 0) | (blk_e_ref[b] != prev))
    def _():
        wgu_bf[...] = wgu_ref[0, 0].astype(BF16)
        wd_bf[...] = wd_ref[0, 0].astype(BF16)

    @pl.when(b < nused_ref[0])
    def _():
        gu = _dot(_load_row_tiles(xg_ref, 0, EXPERT_ROWS).astype(BF16), wgu_bf[...])
        act = jax.nn.silu(gu[:, :D_EXPERT]) * gu[:, D_EXPERT:]
        _store_row_tiles(y_ref, _dot(act.astype(BF16), wd_bf[...]))

    @pl.when(b >= nused_ref[0])
    def _():
        y_ref[...] = jnp.zeros_like(y_ref)


def _experts(blk_e, nused, xg, w_gate_up, w_down, layer):
    rows = EXPERT_ROWS * ROW_TILES
    nblk = xg.shape[0] // rows
    used = lambda b, n: jnp.minimum(b, n[0] - 1)
    grid_spec = pltpu.PrefetchScalarGridSpec(
        num_scalar_prefetch=2,
        grid=(nblk,),
        in_specs=[pl.BlockSpec((rows, LANES), lambda b, e, n: (used(b, n), 0)),
                  pl.BlockSpec((1, 1, D_MODEL, 2 * D_EXPERT), lambda b, e, n: (layer, e[b], 0, 0)),
                  pl.BlockSpec((1, 1, D_EXPERT, D_MODEL), lambda b, e, n: (layer, e[b], 0, 0))],
        out_specs=pl.BlockSpec((rows, LANES), lambda b, e, n: (b, 0)),
        scratch_shapes=[pltpu.VMEM((D_MODEL, 2 * D_EXPERT), BF16),
                        pltpu.VMEM((D_EXPERT, D_MODEL), BF16)],
    )
    return pl.pallas_call(
        _expert_kernel,
        grid_spec=grid_spec,
        out_shape=jax.ShapeDtypeStruct((nblk * rows, LANES), U32),
        compiler_params=_params("arbitrary"),
        name="experts",
    )(blk_e, nused, xg, w_gate_up, w_down)


def _combine_kernel(pos_ref, x_ref, w_ref, y_hbm, o_ref, ybuf, sem):
    s = pl.program_id(0)
    nsteps = pl.num_programs(0)
    slot = s % 2
    tt = COMBINE_TILE
    rows = EXPERT_TOPK * tt

    def issue(step, slot):
        dst = ybuf.at[slot]
        for r in range(rows):
            src_slab = pl.multiple_of(pos_ref[step * rows + r], ROW_TILES)
            _row_dma(y_hbm, src_slab, dst, r * ROW_TILES, sem.at[slot]).start(priority=r % 2)

    @pl.when(s == 0)
    def _():
        issue(0, 0)

    @pl.when(s + 1 < nsteps)
    def _():
        issue(s + 1, 1 - slot)

    _wait_rows(y_hbm, ybuf.at[slot], sem.at[slot], rows)
    w = w_ref[...]
    got = ybuf.at[slot]
    o_ref[...] = (x_ref[...] + w[:, 0:1] * _load_row_tiles(got, 0, tt)
                  + w[:, 1:2] * _load_row_tiles(got, tt, tt))


def _combine(pos, x, w, y):
    n = x.shape[0]
    tt = COMBINE_TILE
    grid_spec = pltpu.PrefetchScalarGridSpec(
        num_scalar_prefetch=1,
        grid=(n // tt,),
        in_specs=[pl.BlockSpec((tt, D_MODEL), lambda s, p: (s, 0)),
                  pl.BlockSpec((tt, EXPERT_TOPK), lambda s, p: (s, 0)),
                  pl.BlockSpec(memory_space=pl.ANY)],
        out_specs=pl.BlockSpec((tt, D_MODEL), lambda s, p: (s, 0)),
        scratch_shapes=[pltpu.VMEM((2, EXPERT_TOPK * tt * ROW_TILES, LANES), U32),
                        pltpu.SemaphoreType.DMA((2,))],
    )
    return pl.pallas_call(
        _combine_kernel,
        grid_spec=grid_spec,
        out_shape=jax.ShapeDtypeStruct((n, D_MODEL), F32),
        compiler_params=_params("arbitrary"),
        name="moe_combine",
    )(pos, x, w, y)


def _rope_tables(seq):
    inv = ROPE_THETA ** (-jnp.arange(0, HEAD_DIM, 2, dtype=F32) / HEAD_DIM)
    ang = jnp.arange(seq, dtype=F32)[:, None] * inv[None, :]
    cos, sin = jnp.cos(ang), jnp.sin(ang)
    cos = jnp.concatenate([cos, cos] * (LANES // HEAD_DIM), axis=1)
    sin = jnp.concatenate([-sin, sin] * (LANES // HEAD_DIM), axis=1)
    return cos, sin


def _ssm_weights(a_re, a_im, log_dt, b_re, b_im, c_re, c_im):
    dt = jnp.exp(log_dt)[:, None]
    decay = jnp.exp(a_re * dt)
    abar_re = decay * jnp.cos(a_im * dt)
    abar_im = decay * jnp.sin(a_im * dt)
    den = a_re * a_re + a_im * a_im
    num_re = abar_re - 1.0
    f_re = (num_re * a_re + abar_im * a_im) / den
    f_im = (abar_im * a_re - num_re * a_im) / den
    bbar_re = f_re[..., None] * b_re - f_im[..., None] * b_im
    bbar_im = f_re[..., None] * b_im + f_im[..., None] * b_re
    gh = SSM_GROUPS // 2
    eye = jnp.eye(gh, dtype=F32)

    def b_mat(t):
        t = t.reshape(2, gh, SSM_STATE, SSM_GROUP_CH)
        return jnp.einsum('cgph,gk->cghkp', t, eye).reshape(2, SSM_HALF, SSM_HALF_STATE).astype(BF16)

    def c_mat(t):
        t = t.reshape(2, gh, SSM_GROUP_CH, SSM_STATE)
        return jnp.einsum('cghp,gk->cgpkh', t, eye).reshape(2, SSM_HALF_STATE, SSM_HALF).astype(BF16)

    a_vec = lambda t: t.reshape(2, 1, SSM_HALF_STATE)
    return (b_mat(bbar_re), b_mat(bbar_im), c_mat(c_re), c_mat(-c_im), a_vec(abar_re), a_vec(abar_im))


def _route(logits, n):
    expert_id, rank, weights, cnt = _route_rank(logits)
    counts = cnt[0, :N_EXPERTS]
    padded = (counts + EXPERT_ROWS - 1) // EXPERT_ROWS * EXPERT_ROWS
    pend = jnp.cumsum(padded)
    pstart = pend - padded
    experts = jnp.arange(N_EXPERTS, dtype=jnp.int32)
    start_of = jnp.sum(jnp.where(expert_id[..., None] == experts, pstart, 0), axis=-1)
    dest = (start_of + rank).astype(jnp.int32)
    cap = n * EXPERT_TOPK + N_EXPERTS * EXPERT_ROWS
    nblk = cap // EXPERT_ROWS
    blk_row = jnp.arange(nblk, dtype=jnp.int32)[:, None] * EXPERT_ROWS
    blk_e = jnp.minimum(jnp.sum((pend[None, :] <= blk_row).astype(jnp.int32), axis=1), N_EXPERTS - 1)
    nused = (pend[-1:] // EXPERT_ROWS).astype(jnp.int32)
    pos = dest.reshape(n // COMBINE_TILE, COMBINE_TILE, EXPERT_TOPK).transpose(0, 2, 1).reshape(-1)
    pos = pos * ROW_TILES
    pad0 = jnp.concatenate([pstart + counts, pend[-1:]]).astype(jnp.int32)
    npad = jnp.concatenate([padded - counts, (cap - pend[-1:]) // SUBLANES]).astype(jnp.int32)
    return weights, pos, pad0, npad, blk_e, nused, cap


def kernel(x, norm1_g, w_in, ssm_a_re, ssm_a_im, ssm_log_dt, ssm_b_re, ssm_b_im, ssm_c_re, ssm_c_im,
           ssm_d, w_glu, q_norm_g, k_norm_g, w_attn, w_out, norm2_g, router_w_group, router_b_group,
           router_w_expert, router_b_expert, w_gate_up, w_down):
    batch, seq, _ = x.shape
    depth = w_in.shape[0]
    n = batch * seq
    assert batch == SUBLANES and seq % MOBA_BLOCK == 0 and n % TOKEN_TILE == 0
    assert seq // MOBA_BLOCK <= GATE_ROWS and seq % SSM_STEPS == 0

    xt = x.reshape(n, D_MODEL)
    cos, sin = _rope_tables(seq)
    time_major = lambda t: t.reshape(batch, seq, -1).transpose(1, 0, 2).reshape(n, -1)
    batch_major = lambda t: t.reshape(seq, batch, -1).transpose(1, 0, 2).reshape(n, -1)
    idx = jnp.arange(ATTN_WIDTH)
    gsum =(idx[:, None] // HEAD_DIM == idx[None, :] // HEAD_DIM).astype(BF16)
    row = lambda t: t.reshape(1, -1).astype(F32)

    for l in range(depth):
        qg = row(jnp.tile(q_norm_g[l], N_HEADS)) * (HEAD_DIM ** -0.5 * math.log2(math.e))
        kg = row(jnp.tile(k_norm_g[l], N_HEADS))
        u, q, k, v, g = _in_proj(xt, row(norm1_g[l]), w_in[l].astype(BF16), gsum, qg, kg, cos, sin)
        ssm_w = _ssm_weights(ssm_a_re[l], ssm_a_im[l], ssm_log_dt[l], ssm_b_re[l], ssm_b_im[l],
                             ssm_c_re[l], ssm_c_im[l])
        z = batch_major(_ssm(time_major(u), *ssm_w, row(ssm_d[l]), batch))
        attn = _moba(q, k, v, batch)

        w_r = jnp.concatenate([router_w_group[l], router_w_expert[l]], axis=1)
        w_r = jnp.pad(w_r, ((0, 0), (0, ROUTER_LANES - w_r.shape[1])))
        w_rh = w_r.astype(BF16)
        w_rl = (w_r - w_rh.astype(F32)).astype(BF16)
        b_r = jnp.concatenate([router_b_group[l], router_b_expert[l]])
        b_r = row(jnp.pad(b_r, (0, ROUTER_LANES - b_r.shape[0])))
        g2 = row(norm2_g[l])
        xt, logits = _mix(z, attn, g, xt, w_glu[l].astype(BF16), w_attn[l].astype(BF16),
                          w_out[l].astype(BF16), g2, w_rh, w_rl, b_r)

        weights, pos, pad0, npad, blk_e, nused, cap = _route(logits, n)
        xg = _dispatch(pos, pad0, npad, xt, g2, cap)
        y = _experts(blk_e, nused, xg, w_gate_up, w_down, l)
        xt = _combine(pos, xt, weights, y)

    return xt.reshape(batch, seq, D_MODEL)
```

```python
import functools
import math

import jax
import jax.numpy as jnp
from jax import lax
from jax.experimental import pallas as pl
from jax.experimental.pallas import tpu as pltpu

F32 = jnp.float32
BF16 = jnp.bfloat16
U32 = jnp.uint32

D_MODEL = 1024
SSM_GROUPS = 32
SSM_GROUP_CH = 16
SSM_WIDTH = SSM_GROUPS * SSM_GROUP_CH
SSM_STATE = 64
N_HEADS = 8
HEAD_DIM = 64
ATTN_WIDTH = N_HEADS * HEAD_DIM
MOBA_BLOCK = 256
MOBA_TOPK = 3
ROPE_THETA = 10000.0
D_IN = SSM_WIDTH + 3 * ATTN_WIDTH + 2 * D_MODEL
N_GROUPS = 4
EXPERTS_PER_GROUP = 8
N_EXPERTS = N_GROUPS * EXPERTS_PER_GROUP
EXPERT_TOPK = 2
D_EXPERT = 512
EXPERT_ROWS = 256
NORM_EPS = 1e-6

LANES = 128
SUBLANES = 8
VMEM_LIMIT = 56 * 1024 * 1024
MASK_NEG = -1e30

TOKEN_TILE = 512
SSM_STEPS = 64
SSM_HALF = SSM_WIDTH // 2
SSM_HALF_STATE = SSM_GROUPS * SSM_STATE // 2
GATE_ROWS = 16
V_ROWS = HEAD_DIM + 16
SOFTMAX_LAG = 3
VALUES_LAG = 6
ROUTER_LANES = 128
COMBINE_TILE = 256
ROW_TILES = D_MODEL // (2 * LANES)


def _params(*sem):
    return pltpu.CompilerParams(dimension_semantics=sem, vmem_limit_bytes=VMEM_LIMIT)


def _dot(a, b):
    return jnp.dot(a, b, preferred_element_type=F32)


def _rms(x, gain):
    return x * lax.rsqrt(jnp.mean(x * x, axis=-1, keepdims=True) + NORM_EPS) * gain


def _in_proj_kernel(x_ref, g1_ref, w_ref, gs_ref, qg_ref, kg_ref, cos_ref, sin_ref,
                    u_ref, q_ref, k_ref, v_ref, g_ref):
    hb = _rms(x_ref[...], g1_ref[...]).astype(BF16)

    def seg(lo, hi):
        return _dot(hb, w_ref[:, lo:hi])

    o_q = SSM_WIDTH
    o_k = o_q + ATTN_WIDTH
    o_v = o_k + ATTN_WIDTH
    o_g = o_v + ATTN_WIDTH
    u_ref[...] = seg(0, o_q).astype(BF16)
    v_ref[...] = seg(o_v, o_g).astype(BF16)
    g_ref[...] = seg(o_g, D_IN).astype(BF16)

    reps = ATTN_WIDTH // LANES
    cos = jnp.concatenate([cos_ref[...]] * reps, axis=1)
    sin = jnp.concatenate([sin_ref[...]] * reps, axis=1)
    lane = lax.broadcasted_iota(jnp.int32, cos.shape, 1)
    first_half = (lane % HEAD_DIM) < (HEAD_DIM // 2)

    def norm_rope(t, gain):
        ss = _dot((t * t).astype(BF16), gs_ref[...])
        tn = t * lax.rsqrt(ss * (1.0 / HEAD_DIM) + NORM_EPS) * gain
        partner = jnp.where(first_half,
                            pltpu.roll(tn, ATTN_WIDTH - HEAD_DIM // 2, 1),
                            pltpu.roll(tn, HEAD_DIM // 2, 1))
        return tn * cos + partner * sin

    q_ref[...] = norm_rope(seg(o_q, o_k), qg_ref[...]).astype(BF16)
    k_ref[...] = norm_rope(seg(o_k, o_v), kg_ref[...]).astype(BF16)


def _in_proj(x, g1, w_in, gsum, qg, kg, cos, sin):
    n = x.shape[0]
    tm = TOKEN_TILE
    row = lambda i: (i, 0)
    fix = lambda i: (0, 0)
    seq = cos.shape[0]
    tiles_per_seq = seq // tm
    pos = lambda i: (i % tiles_per_seq, 0)
    outs = [jax.ShapeDtypeStruct((n, w), BF16)
            for w in (SSM_WIDTH, ATTN_WIDTH, ATTN_WIDTH, ATTN_WIDTH, 2 * D_MODEL)]
    return pl.pallas_call(
        _in_proj_kernel,
        grid=(n // tm,),
        in_specs=[pl.BlockSpec((tm, D_MODEL), row),
                  pl.BlockSpec((1, D_MODEL), fix),
                  pl.BlockSpec((D_MODEL, D_IN), fix),
                  pl.BlockSpec((ATTN_WIDTH, ATTN_WIDTH), fix),
                  pl.BlockSpec((1, ATTN_WIDTH), fix),
                  pl.BlockSpec((1, ATTN_WIDTH), fix),
                  pl.BlockSpec((tm, LANES), pos),
                  pl.BlockSpec((tm, LANES), pos)],
        out_specs=[pl.BlockSpec((tm, o.shape[1]), row) for o in outs],
        out_shape=outs,
        compiler_params=_params("parallel"),
        name="in_proj",
    )(x, g1, w_in, gsum, qg, kg, cos, sin)


def _ssm_kernel(u_ref, bre_ref, bim_ref, cre_ref, cim_ref, are_ref, aim_ref, d_ref,
                z_ref, h_ref, sre_ref, sim_ref, *, batch):
    @pl.when(pl.program_id(0) == 0)
    def _():
        h_ref[...] = jnp.zeros_like(h_ref)

    steps = u_ref.shape[0] // batch
    ys = []
    for c in range(2):
        uc = u_ref[:, c * SSM_HALF:(c + 1) * SSM_HALF]
        sre_ref[...] = _dot(uc, bre_ref[c])
        sim_ref[...] = _dot(uc, bim_ref[c])
        a_re = jnp.broadcast_to(are_ref[c], (batch, SSM_HALF_STATE))
        a_im = jnp.broadcast_to(aim_ref[c], (batch, SSM_HALF_STATE))

        def step(t, carry):
            h_re, h_im = carry
            r0 = pl.multiple_of(t * batch, batch)
            n_re = a_re * h_re - a_im * h_im + sre_ref[pl.ds(r0, batch), :]
            n_im = a_re * h_im + a_im * h_re + sim_ref[pl.ds(r0, batch), :]
            sre_ref[pl.ds(r0, batch), :] = n_re
            sim_ref[pl.ds(r0, batch), :] = n_im
            return n_re, n_im

        h_re, h_im = lax.fori_loop(0, steps, step, (h_ref[c, 0], h_ref[c, 1]), unroll=True)
        h_ref[c, 0] = h_re
        h_ref[c, 1] = h_im
        ys.append(_dot(sre_ref[...].astype(BF16), cre_ref[c])
                  + _dot(sim_ref[...].astype(BF16), cim_ref[c]))
    y = jnp.concatenate(ys, axis=1) + d_ref[...] * u_ref[...].astype(F32)
    z_ref[...] = jax.nn.gelu(y).astype(BF16)


def _ssm(u, bre, bim, cre, cim, are, aim, d, batch):
    n = u.shape[0]
    rows = SSM_STEPS * batch
    fix3 = lambda i: (0, 0, 0)
    return pl.pallas_call(
        functools.partial(_ssm_kernel, batch=batch),
        grid=(n // rows,),
        in_specs=[pl.BlockSpec((rows, SSM_WIDTH), lambda i: (i, 0)),
                  pl.BlockSpec(bre.shape, fix3), pl.BlockSpec(bim.shape, fix3),
                  pl.BlockSpec(cre.shape, fix3), pl.BlockSpec(cim.shape, fix3),
                  pl.BlockSpec(are.shape, fix3), pl.BlockSpec(aim.shape, fix3),
                  pl.BlockSpec((1, SSM_WIDTH), lambda i: (0, 0))],
        out_specs=pl.BlockSpec((rows, SSM_WIDTH), lambda i: (i, 0)),
        out_shape=jax.ShapeDtypeStruct((n, SSM_WIDTH), BF16),
        scratch_shapes=[pltpu.VMEM((2, 2, batch, SSM_HALF_STATE), F32),
                        pltpu.VMEM((rows, SSM_HALF_STATE), F32),
                        pltpu.VMEM((rows, SSM_HALF_STATE), F32)],
        compiler_params=_params("arbitrary"),
        name="s5_scan",
    )(u, bre, bim, cre, cim, are, aim, d)


def _moba_kernel(q_ref, k_ref, v_ref, o_ref, kpad_ref, vt_ref, km_ref, qt_ref, m_ref, alpha_ref, acc_ref,
                 s_ref, p_ref):
    i = pl.program_id(1)
    blk = MOBA_BLOCK
    nblk = k_ref.shape[0] // blk
    n_sel = min(MOBA_TOPK, nblk - 1)

    @pl.when(i == 0)
    def _build():
        lane = lax.broadcasted_iota(jnp.int32, (blk, LANES), 1)
        is_head_lane = lane < HEAD_DIM
        km_ref[...] = jnp.zeros_like(km_ref)
        ones_row = (lax.broadcasted_iota(jnp.int32, (V_ROWS - HEAD_DIM, blk), 0) == 0)

        def per_block(j, _):
            r0 = pl.multiple_of(j * blk, blk)
            vt = v_ref[pl.ds(r0, blk), :].astype(F32).T
            for h in range(N_HEADS):
                p = h // 2
                kk = k_ref[pl.ds(r0, blk), p * LANES:(p + 1) * LANES].astype(F32)
                if h % 2:
                    kk = pltpu.roll(kk, HEAD_DIM, 1)
                kk = jnp.where(is_head_lane, kk, 0.0)
                km_ref[h, pl.ds(j, 1), :] = jnp.mean(kk, axis=0, keepdims=True)
                kk = jnp.where(lane == HEAD_DIM + j, 1.0, kk)
                kpad_ref[j, :, h * LANES:(h + 1) * LANES] = kk.astype(BF16)
                vt_ref[j, h, 0:HEAD_DIM, :] = vt[h * HEAD_DIM:(h + 1) * HEAD_DIM, :].astype(BF16)
                vt_ref[j, h, HEAD_DIM:V_ROWS, :] = ones_row.astype(BF16)
            return 0

        lax.fori_loop(0, nblk, per_block, 0)

    qt = q_ref[...].astype(F32).T
    blk_row = lax.broadcasted_iota(jnp.int32, (GATE_ROWS, blk), 0)
    past = blk_row < i
    blk_row_f = blk_row.astype(F32)
    zeros_q = jnp.zeros((LANES - HEAD_DIM, blk), F32)
    zeros_pad = jnp.zeros((LANES - HEAD_DIM - GATE_ROWS, blk), F32)
    for h in range(N_HEADS):
        qh = qt[h * HEAD_DIM:(h + 1) * HEAD_DIM, :]
        q_pad = jnp.concatenate([qh, zeros_q], axis=0).astype(BF16)
        km = km_ref[h]
        km_hi = km.astype(BF16)
        km_lo = (km - km_hi.astype(F32)).astype(BF16)
        gate = jnp.where(past, _dot(km_hi, q_pad) + _dot(km_lo, q_pad), -jnp.inf)
        sel = jnp.zeros(gate.shape, jnp.bool_)
        for _ in range(n_sel):
            best = jnp.max(gate, axis=0, keepdims=True)
            cand = (gate == best) & (best > -jnp.inf)
            first = jnp.min(jnp.where(cand, blk_row_f, float(GATE_ROWS)), axis=0, keepdims=True)
            pick = blk_row_f == first
            sel = sel | pick
            gate = jnp.where(pick, -jnp.inf, gate)
        bias = jnp.where(past & ~sel, MASK_NEG, 0.0)
        qt_ref[h] = jnp.concatenate([qh, bias, zeros_pad], axis=0).astype(BF16)
        m_ref[h] = jnp.full((1, blk), -jnp.inf, F32)
        acc_ref[h] = jnp.zeros((V_ROWS, blk), F32)

    n_s, n_p = s_ref.shape[0], p_ref.shape[0]

    def scores(j, h, u):
        s_ref[u % n_s] = _dot(kpad_ref[j, :, h * LANES:(h + 1) * LANES], qt_ref[h])

    def softmax(h, u, mask):
        st = s_ref[u % n_s]
        if mask is not None:
            st = jnp.where(mask, st, MASK_NEG)
        m_old = m_ref[h]
        m_new = jnp.maximum(m_old, jnp.max(st, axis=0, keepdims=True))
        alpha_ref[h] = jnp.exp2(m_old - m_new)
        p_ref[u % n_p] = jnp.exp2((st - m_new).astype(BF16))
        m_ref[h] = m_new

    def weighted_values(j, h, u):
        acc_ref[h] = alpha_ref[h] * acc_ref[h] + _dot(vt_ref[j, h], p_ref[u % n_p])

    def kv_blocks(js, last_mask=None):
        units = [(j, h, last_mask if b == len(js) - 1 else None)
                 for b, j in enumerate(js) for h in range(N_HEADS)]
        for step in range(len(units) + VALUES_LAG):
            if step < len(units):
                scores(units[step][0], units[step][1], step)
            u = step - SOFTMAX_LAG
            if 0 <= u < len(units):
                softmax(units[u][1], u, units[u][2])
            u = step - VALUES_LAG
            if 0 <= u < len(units):
                weighted_values(units[u][0], units[u][1], u)

    def octet(g, _):
        kv_blocks([g * 8 + d for d in range(8)])
        return 0

    def quad(g, _):
        kv_blocks([g * 4 + d for d in range(4)])
        return 0

    n_octets = i // 8
    lax.fori_loop(0, n_octets, octet, 0)
    lax.fori_loop(n_octets * 2, i // 4, quad, 0)

    key_pos = lax.broadcasted_iota(jnp.int32, (blk, blk), 0)
    qry_pos = lax.broadcasted_iota(jnp.int32, (blk, blk), 1)
    causal = key_pos <= qry_pos
    left = i % 4

    @pl.when(left == 0)
    def _():
        kv_blocks([i], causal)

    @pl.when(left == 1)
    def _():
        kv_blocks([i - 1, i], causal)

    @pl.when(left == 2)
    def _():
        kv_blocks([i - 2, i - 1])
        kv_blocks([i], causal)

    @pl.when(left == 3)
    def _():
        kv_blocks([i - 3, i - 2, i - 1, i], causal)
    outs = []
    for h in range(N_HEADS):
        acc = acc_ref[h]
        outs.append(acc[0:HEAD_DIM, :] / acc[HEAD_DIM:HEAD_DIM + 1, :])
    o_ref[...] = jnp.concatenate(outs, axis=0).T.astype(BF16)


def _moba(q, k, v, batch):
    n = q.shape[0]
    seq = n // batch
    tq = MOBA_BLOCK
    nblk = seq // MOBA_BLOCK
    return pl.pallas_call(
        _moba_kernel,
        grid=(batch, nblk),
        in_specs=[pl.BlockSpec((tq, ATTN_WIDTH), lambda b, i: (b * nblk + i, 0)),
                  pl.BlockSpec((seq, ATTN_WIDTH), lambda b, i: (b, 0)),
                  pl.BlockSpec((seq, ATTN_WIDTH), lambda b, i: (b, 0))],
        out_specs=pl.BlockSpec((tq, ATTN_WIDTH), lambda b, i: (b * nblk + i, 0)),
        out_shape=jax.ShapeDtypeStruct((n, ATTN_WIDTH), BF16),
        scratch_shapes=[pltpu.VMEM((nblk, MOBA_BLOCK, N_HEADS * LANES), BF16),
                        pltpu.VMEM((nblk, N_HEADS, V_ROWS, MOBA_BLOCK), BF16),
                        pltpu.VMEM((N_HEADS, GATE_ROWS, LANES), F32),
                        pltpu.VMEM((N_HEADS, LANES, MOBA_BLOCK), BF16),
                        pltpu.VMEM((N_HEADS, 1, MOBA_BLOCK), F32),
                        pltpu.VMEM((N_HEADS, 1, MOBA_BLOCK), F32),
                        pltpu.VMEM((N_HEADS, V_ROWS, MOBA_BLOCK), F32),
                        pltpu.VMEM((VALUES_LAG, MOBA_BLOCK, MOBA_BLOCK), F32),
                        pltpu.VMEM((VALUES_LAG - SOFTMAX_LAG + 1, MOBA_BLOCK, MOBA_BLOCK), BF16)],
        compiler_params=_params("parallel", "arbitrary"),
        name="moba_attention",
    )(q, k, v)


def _mix_kernel(z_ref, a_ref, g_ref, x_ref, wglu_ref, wattn_ref, wout_ref, g2_ref,
                wrh_ref, wrl_ref, rb_ref, xo_ref, lg_ref):
    glu = _dot(z_ref[...], wglu_ref[...])
    y_ssm = glu[:, :D_MODEL] * jax.nn.sigmoid(glu[:, D_MODEL:])
    y_attn = _dot(a_ref[...], wattn_ref[...])
    g = g_ref[...].astype(F32)
    mixed = jax.nn.sigmoid(g[:, :D_MODEL]) * y_ssm + jax.nn.sigmoid(g[:, D_MODEL:]) * y_attn
    x = x_ref[...] + _dot(mixed.astype(BF16), wout_ref[...])
    xo_ref[...] = x
    h2 = _rms(x, g2_ref[...])
    hi = h2.astype(BF16)
    lo = (h2 - hi.astype(F32)).astype(BF16)
    lg_ref[...] = (_dot(hi, wrh_ref[...]) + _dot(lo, wrh_ref[...]) + _dot(hi, wrl_ref[...])
                   + rb_ref[...])


def _mix(z, a, g, x, wglu, wattn, wout, g2, wrh, wrl, rb):
    n = x.shape[0]
    tm = TOKEN_TILE
    row = lambda i: (i, 0)
    fix = lambda i: (0, 0)
    full = lambda t: pl.BlockSpec(t.shape, fix)
    return pl.pallas_call(
        _mix_kernel,
        grid=(n // tm,),
        in_specs=[pl.BlockSpec((tm, SSM_WIDTH), row), pl.BlockSpec((tm, ATTN_WIDTH), row),
                  pl.BlockSpec((tm, 2 * D_MODEL), row), pl.BlockSpec((tm, D_MODEL), row),
                  full(wglu), full(wattn), full(wout), full(g2), full(wrh), full(wrl), full(rb)],
        out_specs=[pl.BlockSpec((tm, D_MODEL), row), pl.BlockSpec((tm, ROUTER_LANES), row)],
        out_shape=[jax.ShapeDtypeStruct((n, D_MODEL), F32),
                   jax.ShapeDtypeStruct((n, ROUTER_LANES), F32)],
        compiler_params=_params("parallel"),
        name="mix_out_router",
    )(z, a, g, x, wglu, wattn, wout, g2, wrh, wrl, rb)


def _route_kernel(lg_ref, eid_ref, rank_ref, w_ref, cnt_ref, carry_ref):
    @pl.when(pl.program_id(0) == 0)
    def _():
        carry_ref[...] = jnp.zeros_like(carry_ref)

    lg = lg_ref[...]
    t = lg.shape[0]
    lane = lax.broadcasted_iota(jnp.int32, (t, LANES), 1).astype(F32)

    def first_lane(hit):
        return jnp.min(jnp.where(hit, lane, float(LANES)), axis=1, keepdims=True)

    g_log = jnp.where(lane < N_GROUPS, lg, -jnp.inf)
    g_max = jnp.max(g_log, axis=1, keepdims=True)
    g_top = first_lane(g_log == g_max)
    p_g = 1.0 / jnp.sum(jnp.exp(g_log - g_max), axis=1, keepdims=True)

    lo = N_GROUPS + g_top * EXPERTS_PER_GROUP
    in_group = (lane >= lo) & (lane < lo + EXPERTS_PER_GROUP)
    e_log = jnp.where(in_group, lg, -jnp.inf)
    e_exp = jnp.exp(e_log - jnp.max(e_log, axis=1, keepdims=True))
    prob = jnp.where(in_group, e_exp / jnp.sum(e_exp, axis=1, keepdims=True), -1.0)
    p1 = jnp.max(prob, axis=1, keepdims=True)
    l1 = first_lane(prob == p1)
    prob = jnp.where(lane == l1, -1.0, prob)
    p2 = jnp.max(prob, axis=1, keepdims=True)
    l2 = first_lane(prob == p2)
    w_ref[...] = jnp.concatenate([p_g * (p1 / (p1 + p2)), p_g * (p2 / (p1 + p2))], axis=1)
    eid = [l1 - N_GROUPS, l2 - N_GROUPS]
    eid_ref[...] = jnp.concatenate(eid, axis=1).astype(jnp.int32)

    earlier = (lax.broadcasted_iota(jnp.int32, (t, t), 1)
               < lax.broadcasted_iota(jnp.int32, (t, t), 0)).astype(BF16)
    carry = carry_ref[...]
    ranks = []
    for e in eid:
        hit = lane == e
        onehot = hit.astype(F32)
        before = _dot(earlier, onehot.astype(BF16)) + carry
        ranks.append(jnp.sum(jnp.where(hit, before, 0.0), axis=1, keepdims=True))
        carry = carry + jnp.sum(onehot, axis=0, keepdims=True)
    carry_ref[...] = carry
    rank_ref[...] = jnp.concatenate(ranks, axis=1).astype(jnp.int32)
    cnt_ref[...] = carry.astype(jnp.int32)


def _route_rank(logits):
    n = logits.shape[0]
    t = TOKEN_TILE
    pair = pl.BlockSpec((t, EXPERT_TOPK), lambda i: (i, 0))
    return pl.pallas_call(
        _route_kernel,
        grid=(n // t,),
        in_specs=[pl.BlockSpec((t, ROUTER_LANES), lambda i: (i, 0))],
        out_specs=[pair, pair, pair, pl.BlockSpec((1, LANES), lambda i: (0, 0))],
        out_shape=[jax.ShapeDtypeStruct((n, EXPERT_TOPK), jnp.int32),
                   jax.ShapeDtypeStruct((n, EXPERT_TOPK), jnp.int32),
                   jax.ShapeDtypeStruct((n, EXPERT_TOPK), F32),
                   jax.ShapeDtypeStruct((1, LANES), jnp.int32)],
        scratch_shapes=[pltpu.VMEM((1, LANES), F32)],
        compiler_params=_params("arbitrary"),
        name="route_rank",
    )(logits)


def _store_row_tiles(ref, value):
    rows, half = value.shape[0], value.shape[1] // 2
    as_bits = lambda v: lax.bitcast_convert_type(v.astype(BF16).astype(F32), U32)
    words = as_bits(value[:, half:]) | (as_bits(value[:, :half]) >> 16)
    for c in range(ROW_TILES):
        ref[pl.ds(c, rows, stride=ROW_TILES), :] = words[:, c * LANES:(c + 1) * LANES]


def _load_row_tiles(ref, start, rows):
    words = jnp.concatenate(
        [ref[pl.ds(start * ROW_TILES + c, rows, stride=ROW_TILES), :] for c in range(ROW_TILES)], axis=1)
    low = lax.bitcast_convert_type(words << 16, F32)
    high = lax.bitcast_convert_type(words & jnp.uint32(0xFFFF0000), F32)
    return jnp.concatenate([low, high], axis=1)


def _row_dma(src, src_slab, dst, dst_slab, sem, rows=1):
    n = rows * ROW_TILES
    return pltpu.make_async_copy(src.at[pl.ds(src_slab, n), :], dst.at[pl.ds(dst_slab, n), :], sem)


def _wait_rows(src, dst, sem, count, rows=1):
    def body(r, _):
        _row_dma(src, 0, dst, 0, sem, rows).wait()
        return 0
    lax.fori_loop(0, count, body, 0, unroll=8 if isinstance(count, int) else 1)


def _dispatch_kernel(dest_ref, pad0_ref, npad_ref, x_ref, g2_ref, xg_hbm, hbuf, zrow, sem, zsem):
    s = pl.program_id(0)
    last = pl.num_programs(0) - 1
    slot = s % 2
    tt = x_ref.shape[0]
    rows = EXPERT_TOPK * tt

    @pl.when(s == 0)
    def _():
        zrow[...] = jnp.zeros_like(zrow)

        def tile_of(row):
            return pl.multiple_of(row * ROW_TILES, ROW_TILES)

        def per_expert(e, totals):
            groups = lax.shift_right_logical(npad_ref[e], 3)
            singles = npad_ref[e] - groups * SUBLANES

            def fill_group(r, _):
                row = pad0_ref[e] + r * SUBLANES
                _row_dma(zrow, 0, xg_hbm, tile_of(row), zsem, SUBLANES).start()
                return 0

            def fill_single(r, _):
                row = pad0_ref[e] + groups * SUBLANES + r
                _row_dma(zrow, 0, xg_hbm, tile_of(row), zsem).start()
                return 0
            lax.fori_loop(0, groups, fill_group, 0)
            lax.fori_loop(0, singles, fill_single, 0)
            return totals[0] + groups, totals[1] + singles
        n_groups, n_singles = lax.fori_loop(0, N_EXPERTS, per_expert, (0, 0))
        _wait_rows(zrow, xg_hbm, zsem, n_groups, SUBLANES)
        _wait_rows(zrow, xg_hbm, zsem, n_singles)

        def fill_unused(r, _):
            row = pad0_ref[N_EXPERTS] + r * SUBLANES
            _row_dma(zrow, 0, xg_hbm, tile_of(row), zsem, SUBLANES).start()
            return 0
        lax.fori_loop(0, npad_ref[N_EXPERTS], fill_unused, 0)
        _wait_rows(zrow, xg_hbm, zsem, npad_ref[N_EXPERTS], SUBLANES)

    src = hbuf.at[slot]
    _store_row_tiles(src, _rms(x_ref[...], g2_ref[...]))
    for r in range(rows):
        dst_slab = pl.multiple_of(dest_ref[s * rows + r], ROW_TILES)
        _row_dma(src, (r % tt) * ROW_TILES, xg_hbm, dst_slab, sem.at[slot]).start(priority=r % 2)

    @pl.when(s > 0)
    def _():
        _wait_rows(hbuf.at[1 - slot], xg_hbm, sem.at[1 - slot], rows)

    @pl.when(s == last)
    def _():
        _wait_rows(src, xg_hbm, sem.at[slot], rows)


def _dispatch(dest, pad0, npad, x, g2, cap):
    n = x.shape[0]
    tt = COMBINE_TILE
    grid_spec = pltpu.PrefetchScalarGridSpec(
        num_scalar_prefetch=3,
        grid=(n // tt,),
        in_specs=[pl.BlockSpec((tt, D_MODEL), lambda s, d, p, c: (s, 0)),
                  pl.BlockSpec((1, D_MODEL), lambda s, d, p, c: (0, 0))],
        out_specs=pl.BlockSpec(memory_space=pl.ANY),
        scratch_shapes=[pltpu.VMEM((2, tt * ROW_TILES, LANES), U32),
                        pltpu.VMEM((SUBLANES * ROW_TILES, LANES), U32),
                        pltpu.SemaphoreType.DMA((2,)),
                        pltpu.SemaphoreType.DMA],
    )
    return pl.pallas_call(
        _dispatch_kernel,
        grid_spec=grid_spec,
        out_shape=jax.ShapeDtypeStruct((cap * ROW_TILES, LANES), U32),
        compiler_params=_params("arbitrary"),
        name="expert_dispatch",
    )(dest, pad0, npad, x, g2)


def _expert_kernel(blk_e_ref, nused_ref, xg_ref, wgu_ref, wd_ref, y_ref, wgu_bf, wd_bf):
    b = pl.program_id(0)
    prev = blk_e_ref[jnp.maximum(b, 1) - 1]

    @pl.when((b == 0) | (blk_e_ref[b] != prev))
    def _():
        wgu_bf[...] = wgu_ref[0, 0].astype(BF16)
        wd_bf[...] = wd_ref[0, 0].astype(BF16)

    @pl.when(b < nused_ref[0])
    def _():
        gu = _dot(_load_row_tiles(xg_ref, 0, EXPERT_ROWS).astype(BF16), wgu_bf[...])
        act = jax.nn.silu(gu[:, :D_EXPERT]) * gu[:, D_EXPERT:]
        _store_row_tiles(y_ref, _dot(act.astype(BF16), wd_bf[...]))

    @pl.when(b >= nused_ref[0])
    def _():
        y_ref[...] = jnp.zeros_like(y_ref)


def _experts(blk_e, nused, xg, w_gate_up, w_down, layer):
    rows = EXPERT_ROWS * ROW_TILES
    nblk = xg.shape[0] // rows
    used = lambda b, n: jnp.minimum(b, n[0] - 1)
    grid_spec = pltpu.PrefetchScalarGridSpec(
        num_scalar_prefetch=2,
        grid=(nblk,),
        in_specs=[pl.BlockSpec((rows, LANES), lambda b, e, n: (used(b, n), 0)),
                  pl.BlockSpec((1, 1, D_MODEL, 2 * D_EXPERT), lambda b, e, n: (layer, e[b], 0, 0)),
                  pl.BlockSpec((1, 1, D_EXPERT, D_MODEL), lambda b, e, n: (layer, e[b], 0, 0))],
        out_specs=pl.BlockSpec((rows, LANES), lambda b, e, n: (b, 0)),
        scratch_shapes=[pltpu.VMEM((D_MODEL, 2 * D_EXPERT), BF16),
                        pltpu.VMEM((D_EXPERT, D_MODEL), BF16)],
    )
    return pl.pallas_call(
        _expert_kernel,
        grid_spec=grid_spec,
        out_shape=jax.ShapeDtypeStruct((nblk * rows, LANES), U32),
        compiler_params=_params("arbitrary"),
        name="experts",
    )(blk_e, nused, xg, w_gate_up, w_down)


def _combine_kernel(pos_ref, x_ref, w_ref, y_hbm, o_ref, ybuf, sem):
    s = pl.program_id(0)
    nsteps = pl.num_programs(0)
    slot = s % 2
    tt = COMBINE_TILE
    rows = EXPERT_TOPK * tt

    def issue(step, slot):
        dst = ybuf.at[slot]
        for r in range(rows):
            src_slab = pl.multiple_of(pos_ref[step * rows + r], ROW_TILES)
            _row_dma(y_hbm, src_slab, dst, r * ROW_TILES, sem.at[slot]).start(priority=r % 2)

    @pl.when(s == 0)
    def _():
        issue(0, 0)

    @pl.when(s + 1 < nsteps)
    def _():
        issue(s + 1, 1 - slot)

    _wait_rows(y_hbm, ybuf.at[slot], sem.at[slot], rows)
    w = w_ref[...]
    got = ybuf.at[slot]
    o_ref[...] = (x_ref[...] + w[:, 0:1] * _load_row_tiles(got, 0, tt)
                  + w[:, 1:2] * _load_row_tiles(got, tt, tt))


def _combine(pos, x, w, y):
    n = x.shape[0]
    tt = COMBINE_TILE
    grid_spec = pltpu.PrefetchScalarGridSpec(
        num_scalar_prefetch=1,
        grid=(n // tt,),
        in_specs=[pl.BlockSpec((tt, D_MODEL), lambda s, p: (s, 0)),
                  pl.BlockSpec((tt, EXPERT_TOPK), lambda s, p: (s, 0)),
                  pl.BlockSpec(memory_space=pl.ANY)],
        out_specs=pl.BlockSpec((tt, D_MODEL), lambda s, p: (s, 0)),
        scratch_shapes=[pltpu.VMEM((2, EXPERT_TOPK * tt * ROW_TILES, LANES), U32),
                        pltpu.SemaphoreType.DMA((2,))],
    )
    return pl.pallas_call(
        _combine_kernel,
        grid_spec=grid_spec,
        out_shape=jax.ShapeDtypeStruct((n, D_MODEL), F32),
        compiler_params=_params("arbitrary"),
        name="moe_combine",
    )(pos, x, w, y)


def _rope_tables(seq):
    inv = ROPE_THETA ** (-jnp.arange(0, HEAD_DIM, 2, dtype=F32) / HEAD_DIM)
    ang = jnp.arange(seq, dtype=F32)[:, None] * inv[None, :]
    cos, sin = jnp.cos(ang), jnp.sin(ang)
    cos = jnp.concatenate([cos, cos] * (LANES // HEAD_DIM), axis=1)
    sin = jnp.concatenate([-sin, sin] * (LANES // HEAD_DIM), axis=1)
    return cos, sin


def _ssm_weights(a_re, a_im, log_dt, b_re, b_im, c_re, c_im):
    dt = jnp.exp(log_dt)[:, None]
    decay = jnp.exp(a_re * dt)
    abar_re = decay * jnp.cos(a_im * dt)
    abar_im = decay * jnp.sin(a_im * dt)
    den = a_re * a_re + a_im * a_im
    num_re = abar_re - 1.0
    f_re = (num_re * a_re + abar_im * a_im) / den
    f_im = (abar_im * a_re - num_re * a_im) / den
    bbar_re = f_re[..., None] * b_re - f_im[..., None] * b_im
    bbar_im = f_re[..., None] * b_im + f_im[..., None] * b_re
    gh = SSM_GROUPS // 2
    eye = jnp.eye(gh, dtype=F32)

    def b_mat(t):
        t = t.reshape(2, gh, SSM_STATE, SSM_GROUP_CH)
        return jnp.einsum('cgph,gk->cghkp', t, eye).reshape(2, SSM_HALF, SSM_HALF_STATE).astype(BF16)

    def c_mat(t):
        t = t.reshape(2, gh, SSM_GROUP_CH, SSM_STATE)
        return jnp.einsum('cghp,gk->cgpkh', t, eye).reshape(2, SSM_HALF_STATE, SSM_HALF).astype(BF16)

    a_vec = lambda t: t.reshape(2, 1, SSM_HALF_STATE)
    return (b_mat(bbar_re), b_mat(bbar_im), c_mat(c_re), c_mat(-c_im), a_vec(abar_re), a_vec(abar_im))


def _route(logits, n):
    expert_id, rank, weights, cnt = _route_rank(logits)
    counts = cnt[0, :N_EXPERTS]
    padded = (counts + EXPERT_ROWS - 1) // EXPERT_ROWS * EXPERT_ROWS
    pend = jnp.cumsum(padded)
    pstart = pend - padded
    experts = jnp.arange(N_EXPERTS, dtype=jnp.int32)
    start_of = jnp.sum(jnp.where(expert_id[..., None] == experts, pstart, 0), axis=-1)
    dest = (start_of + rank).astype(jnp.int32)
    cap = n * EXPERT_TOPK + N_EXPERTS * EXPERT_ROWS
    nblk = cap // EXPERT_ROWS
    blk_row = jnp.arange(nblk, dtype=jnp.int32)[:, None] * EXPERT_ROWS
    blk_e = jnp.minimum(jnp.sum((pend[None, :] <= blk_row).astype(jnp.int32), axis=1), N_EXPERTS - 1)
    nused = (pend[-1:] // EXPERT_ROWS).astype(jnp.int32)
    pos = dest.reshape(n // COMBINE_TILE, COMBINE_TILE, EXPERT_TOPK).transpose(0, 2, 1).reshape(-1)
    pos = pos * ROW_TILES
    pad0 = jnp.concatenate([pstart + counts, pend[-1:]]).astype(jnp.int32)
    npad = jnp.concatenate([padded - counts, (cap - pend[-1:]) // SUBLANES]).astype(jnp.int32)
    return weights, pos, pad0, npad, blk_e, nused, cap


def kernel(x, norm1_g, w_in, ssm_a_re, ssm_a_im, ssm_log_dt, ssm_b_re, ssm_b_im, ssm_c_re, ssm_c_im,
           ssm_d, w_glu, q_norm_g, k_norm_g, w_attn, w_out, norm2_g, router_w_group, router_b_group,
           router_w_expert, router_b_expert, w_gate_up, w_down):
    batch, seq, _ = x.shape
    depth = w_in.shape[0]
    n = batch * seq
    assert batch == SUBLANES and seq % MOBA_BLOCK == 0 and n % TOKEN_TILE == 0
    assert seq // MOBA_BLOCK <= GATE_ROWS and seq % SSM_STEPS == 0

    xt = x.reshape(n, D_MODEL)
    cos, sin = _rope_tables(seq)
    time_major = lambda t: t.reshape(batch, seq, -1).transpose(1, 0, 2).reshape(n, -1)
    batch_major = lambda t: t.reshape(seq, batch, -1).transpose(1, 0, 2).reshape(n, -1)
    idx = jnp.arange(ATTN_WIDTH)
    gsum =(idx[:, None] // HEAD_DIM == idx[None, :] // HEAD_DIM).astype(BF16)
    row = lambda t: t.reshape(1, -1).astype(F32)

    for l in range(depth):
        qg = row(jnp.tile(q_norm_g[l], N_HEADS)) * (HEAD_DIM ** -0.5 * math.log2(math.e))
        kg = row(jnp.tile(k_norm_g[l], N_HEADS))
        u, q, k, v, g = _in_proj(xt, row(norm1_g[l]), w_in[l].astype(BF16), gsum, qg, kg, cos, sin)
        ssm_w = _ssm_weights(ssm_a_re[l], ssm_a_im[l], ssm_log_dt[l], ssm_b_re[l], ssm_b_im[l],
                             ssm_c_re[l], ssm_c_im[l])
        z = batch_major(_ssm(time_major(u), *ssm_w, row(ssm_d[l]), batch))
        attn = _moba(q, k, v, batch)

        w_r = jnp.concatenate([router_w_group[l], router_w_expert[l]], axis=1)
        w_r = jnp.pad(w_r, ((0, 0), (0, ROUTER_LANES - w_r.shape[1])))
        w_rh = w_r.astype(BF16)
        w_rl = (w_r - w_rh.astype(F32)).astype(BF16)
        b_r = jnp.concatenate([router_b_group[l], router_b_expert[l]])
        b_r = row(jnp.pad(b_r, (0, ROUTER_LANES - b_r.shape[0])))
        g2 = row(norm2_g[l])
        xt, logits = _mix(z, attn, g, xt, w_glu[l].astype(BF16), w_attn[l].astype(BF16),
                          w_out[l].astype(BF16), g2, w_rh, w_rl, b_r)

        weights, pos, pad0, npad, blk_e, nused, cap = _route(logits, n)
        xg = _dispatch(pos, pad0, npad, xt, g2, cap)
        y = _experts(blk_e, nused, xg, w_gate_up, w_down, l)
        xt = _combine(pos, xt, weights, y)

    return xt.reshape(batch, seq, D_MODEL)
```
